```python
import math
import jax, jax.numpy as jnp
from jax import lax
import numpy as np

D_MODEL = 1024
BATCH = 8
SEQ = 8192
DEPTH = 2

HEAD_DIM = 64
ROPE_THETA = 500000.0
ROPE_DIM = HEAD_DIM // 4
NORM_EPS = 1e-6
Q_BLOCK = 128

A_PAIRS = ((128, 1), (512, 4), (2048, 16))
A_GROUPS = len(A_PAIRS)
A_HEADS = 4
A_OUT = A_HEADS * HEAD_DIM
B_HEADS = 4
B_OUT = B_HEADS * HEAD_DIM
IDX_HEADS = 8
IDX_DIM = 64
TOPK_MAX = 256
C_HEADS = 4
C_VDIM = 2 * HEAD_DIM
C_OUT = C_HEADS * C_VDIM
N_BRANCH = 3

A_IN = A_GROUPS * 3 * A_HEADS * HEAD_DIM
B_IN = 3 * B_HEADS * HEAD_DIM
IDX_IN = IDX_HEADS * IDX_DIM + IDX_DIM + IDX_HEADS
C_QK = C_HEADS * 2 * HEAD_DIM
C_IN = 2 * C_QK + C_OUT
GATE_IN = N_BRANCH * D_MODEL
IN_WIDTH = A_IN + B_IN + IDX_IN + C_IN + GATE_IN
IN_OFFSETS = [A_IN, A_IN + B_IN, A_IN + B_IN + IDX_IN, A_IN + B_IN + IDX_IN + C_IN]

D_FF = ((8 * D_MODEL // 3 + 127) // 128) * 128
CONV_WIDTH = 3

kernel_name = "hybrid_gated_dilated_dsa_diff_block"


def rms_norm(x, g):
    xf = x.astype(jnp.float32)
    y = xf * lax.rsqrt(jnp.mean(xf * xf, axis=-1, keepdims=True) + NORM_EPS)
    return (y * g.astype(jnp.float32)).astype(x.dtype)


def rope_tables(positions):
    inv = ROPE_THETA ** (-jnp.arange(0, ROPE_DIM, 2, dtype=jnp.float32) / ROPE_DIM)
    ang = positions.astype(jnp.float32)[..., None] * inv
    return jnp.cos(ang)[:, :, None, :], jnp.sin(ang)[:, :, None, :]


def apply_partial_rope(x, cos, sin):
    half = ROPE_DIM // 2
    x1 = x[..., :half].astype(jnp.float32)
    x2 = x[..., half:ROPE_DIM].astype(jnp.float32)
    rot = jnp.concatenate([x1 * cos - x2 * sin, x2 * cos + x1 * sin], axis=-1).astype(x.dtype)
    return jnp.concatenate([rot, x[..., ROPE_DIM:]], axis=-1)


def to_blocks(t):
    b, s = t.shape[:2]
    return jnp.moveaxis(t.reshape(b, s // Q_BLOCK, Q_BLOCK, *t.shape[2:]), 1, 0)


def from_blocks(t):
    t = jnp.moveaxis(t, 0, 1)
    return t.reshape(t.shape[0], t.shape[1] * t.shape[2], *t.shape[3:])


def banded_causal_attention(q, k, v, window):
    n, l, h, dh = q.shape
    nb = -(-l // Q_BLOCK)
    pad = nb * Q_BLOCK - l
    padf = lambda t: jnp.pad(t, ((0, 0), (0, pad), (0, 0), (0, 0))).reshape(n, nb, Q_BLOCK, h, dh)
    qb, kb, vb = padf(q), padf(k), padf(v)

    def with_prev(t):
        prev = jnp.pad(t[:, :-1], ((0, 0), (1, 0), (0, 0), (0, 0), (0, 0)))
        return jnp.concatenate([prev, t], axis=2)

    kk, vv = with_prev(kb), with_prev(vb)
    s = jnp.einsum('nbqhd,nbkhd->nbhqk', qb, kk).astype(jnp.float32) * (dh ** -0.5)
    qi = jnp.arange(Q_BLOCK)[:, None] + Q_BLOCK
    kj = jnp.arange(2 * Q_BLOCK)[None, :]
    dist = qi - kj
    band = (dist >= 0) & (dist <= window)
    inside = (jnp.arange(nb)[:, None] * Q_BLOCK - Q_BLOCK + kj) >= 0
    mask = band[None, :, :] & inside[:, None, :]
    s = jnp.where(mask[None, :, None], s, -jnp.inf)
    lse = jax.nn.logsumexp(s, axis=-1)
    p = jnp.exp(s - lse[..., None]).astype(v.dtype)
    o = jnp.einsum('nbhqk,nbkhd->nbqhd', p, vv).reshape(n, nb * Q_BLOCK, h, dh)[:, :l]
    lse = jnp.moveaxis(lse, 2, 3).reshape(n, nb * Q_BLOCK, h)[:, :l]
    return o, lse


def dilated_causal_attention(q, k, v, window, dilation):
    b, s, h, dh = q.shape
    m = s // dilation
    fold = lambda t: t.reshape(b, m, dilation, h, dh).transpose(0, 2, 1, 3, 4).reshape(b * dilation, m, h, dh)
    o, lse = banded_causal_attention(fold(q), fold(k), fold(v), window // dilation)
    o = o.reshape(b, dilation, m, h, dh).transpose(0, 2, 1, 3, 4).reshape(b, s, h, dh)
    lse = lse.reshape(b, dilation, m, h).transpose(0, 2, 1, 3).reshape(b, s, h)
    return o, lse


def mixer_dilated(qkv_a, cos, sin):
    b, s = qkv_a.shape[:2]
    outs, lses = [], []
    for g, (window, dilation) in enumerate(A_PAIRS):
        q = apply_partial_rope(qkv_a[:, :, g, 0], cos, sin)
        k = apply_partial_rope(qkv_a[:, :, g, 1], cos, sin)
        o, lse = dilated_causal_attention(q, k, qkv_a[:, :, g, 2], window, dilation)
        outs.append(o)
        lses.append(lse)
    alpha = jax.nn.softmax(jnp.stack(lses), axis=0).astype(qkv_a.dtype)
    o = jnp.einsum('gbsh,gbshd->bshd', alpha, jnp.stack(outs))
    return o.reshape(b, s, A_OUT)


def mixer_sparse(q, k, v, q_idx, k_idx, w_idx):
    b, s = q.shape[:2]
    topk = min(TOPK_MAX, s // 4)
    kpos = jnp.arange(s)
    starts = jnp.arange(s // Q_BLOCK) * Q_BLOCK

    def block(args):
        qb, qib, wb, start = args
        qpos = start + jnp.arange(Q_BLOCK)
        logits = jnp.einsum('bqhd,bsd->bqhs', qib, k_idx).astype(jnp.float32)
        score = jnp.einsum('bqhs,bqh->bqs', jax.nn.relu(logits), wb.astype(jnp.float32)) * (IDX_DIM ** -0.5)
        score = jnp.where(kpos[None, None, :] <= qpos[None, :, None], score, -jnp.inf)
        _, idx = lax.top_k(score, topk)
        k_sel = jax.vmap(lambda kk, ii: kk[ii])(k, idx)
        v_sel = jax.vmap(lambda vv, ii: vv[ii])(v, idx)
        att = jnp.einsum('bqhd,bqkhd->bhqk', qb, k_sel).astype(jnp.float32) * (HEAD_DIM ** -0.5)
        valid = idx <= qpos[None, :, None]
        att = jnp.where(valid[:, None], att, -jnp.inf)
        p = jax.nn.softmax(att, axis=-1).astype(v.dtype)
        return jnp.einsum('bhqk,bqkhd->bqhd', p, v_sel)

    o = lax.map(block, (to_blocks(q), to_blocks(q_idx), to_blocks(w_idx), starts))
    return from_blocks(o).reshape(b, s, B_OUT)


def mixer_diff(q, k, v, lam, lam_init, subln_g):
    b, s = q.shape[:2]
    kpos = jnp.arange(s)
    starts = jnp.arange(s // Q_BLOCK) * Q_BLOCK

    def block(args):
        qb, start = args
        qpos = start + jnp.arange(Q_BLOCK)
        sc = jnp.einsum('bqhcd,bkhcd->bhcqk', qb, k).astype(jnp.float32) * (HEAD_DIM ** -0.5)
        sc = jnp.where(kpos[None, :] <= qpos[:, None], sc, -jnp.inf)
        p = jax.nn.softmax(sc, axis=-1)
        a = (p[:, :, 0] - lam * p[:, :, 1]).astype(v.dtype)
        return jnp.einsum('bhqk,bkhe->bqhe', a, v)

    o = from_blocks(lax.map(block, (to_blocks(q), starts)))
    o = rms_norm(o, subln_g) * (1.0 - lam_init)
    return o.reshape(b, s, C_OUT)


def mixer_block(xn, cos, sin, w_in, w_br_a, w_br_b, w_br_c, w_out,
                lam_q1, lam_k1, lam_q2, lam_k2, lam_init, subln_g):
    b, s, _ = xn.shape
    proj = xn @ w_in
    pa, pb, pidx, pc, pg = jnp.split(proj, IN_OFFSETS, axis=-1)
    o_a = mixer_dilated(pa.reshape(b, s, A_GROUPS, 3, A_HEADS, HEAD_DIM), cos, sin)
    qkv_b = pb.reshape(b, s, 3, B_HEADS, HEAD_DIM)
    q_b = apply_partial_rope(qkv_b[:, :, 0], cos, sin)
    k_b = apply_partial_rope(qkv_b[:, :, 1], cos, sin)
    q_idx, k_idx, w_idx = jnp.split(pidx, [IDX_HEADS * IDX_DIM, IDX_HEADS * IDX_DIM + IDX_DIM], axis=-1)
    q_idx = apply_partial_rope(q_idx.reshape(b, s, IDX_HEADS, IDX_DIM), cos, sin)
    k_idx = apply_partial_rope(k_idx.reshape(b, s, 1, IDX_DIM), cos, sin)[:, :, 0]
    w_idx = w_idx * (IDX_HEADS ** -0.5)
    o_b = mixer_sparse(q_b, k_b, qkv_b[:, :, 2], q_idx, k_idx, w_idx)
    qc, kc, vc = jnp.split(pc, [C_QK, 2 * C_QK], axis=-1)
    qc = apply_partial_rope(qc.reshape(b, s, C_HEADS * 2, HEAD_DIM), cos, sin).reshape(b, s, C_HEADS, 2, HEAD_DIM)
    kc = apply_partial_rope(kc.reshape(b, s, C_HEADS * 2, HEAD_DIM), cos, sin).reshape(b, s, C_HEADS, 2, HEAD_DIM)
    vc = vc.reshape(b, s, C_HEADS, C_VDIM)
    f32 = jnp.float32
    lam = (jnp.exp(jnp.sum(lam_q1.astype(f32) * lam_k1.astype(f32)))
           - jnp.exp(jnp.sum(lam_q2.astype(f32) * lam_k2.astype(f32))) + lam_init)
    o_c = mixer_diff(qc, kc, vc, lam, lam_init, subln_g)
    gates = jax.nn.sigmoid(pg.reshape(b, s, N_BRANCH, D_MODEL))
    y = (gates[:, :, 0] * (o_a @ w_br_a) + gates[:, :, 1] * (o_b @ w_br_b)
         + gates[:, :, 2] * (o_c @ w_br_c))
    return y @ w_out


def conv_ffn(x, w_up, conv_w, conv_b, w_down):
    s = x.shape[1]
    h = x @ w_up
    hp = jnp.pad(h, ((0, 0), (CONV_WIDTH - 1, 0), (0, 0)))
    h = sum(hp[:, j:j + s] * conv_w[j] for j in range(CONV_WIDTH)) + conv_b
    g, u = jnp.split(h, 2, axis=-1)
    return (jax.nn.gelu(g, approximate=True) * u) @ w_down


def setup_inputs(seed: int = 0) -> dict:
    key = jax.random.key(seed)
    ks = jax.random.split(key, 24)
    f32 = jnp.float32
    dense = lambda k, shape, fan_in: jax.random.normal(k, shape, f32) * (fan_in ** -0.5)
    gain = lambda k, shape: 1.0 + 0.02 * jax.random.normal(k, shape, f32)
    return {
        "x": jax.random.normal(ks[0], (BATCH, SEQ, D_MODEL), f32),
        "positions": jnp.broadcast_to(jnp.arange(SEQ, dtype=jnp.int32), (BATCH, SEQ)),
        "w_in": dense(ks[1], (DEPTH, D_MODEL, IN_WIDTH), D_MODEL),
        "w_br_a": dense(ks[2], (DEPTH, A_OUT, D_MODEL), A_OUT),
        "w_br_b": dense(ks[3], (DEPTH, B_OUT, D_MODEL), B_OUT),
        "w_br_c": dense(ks[4], (DEPTH, C_OUT, D_MODEL), C_OUT),
        "w_out": dense(ks[5], (DEPTH, D_MODEL, D_MODEL), D_MODEL),
        "lam_q1": 0.1 * jax.random.normal(ks[6], (DEPTH, HEAD_DIM), f32),
        "lam_k1": 0.1 * jax.random.normal(ks[7], (DEPTH, HEAD_DIM), f32),
        "lam_q2": 0.1 * jax.random.normal(ks[8], (DEPTH, HEAD_DIM), f32),
        "lam_k2": 0.1 * jax.random.normal(ks[9], (DEPTH, HEAD_DIM), f32),
        "subln_g": gain(ks[10], (DEPTH, C_VDIM)),
        "norm_mix_pre": gain(ks[11], (DEPTH, D_MODEL)),
        "norm_mix_post": gain(ks[12], (DEPTH, D_MODEL)),
        "norm_ffn_pre": gain(ks[13], (DEPTH, D_MODEL)),
        "norm_ffn_post": gain(ks[14], (DEPTH, D_MODEL)),
        "w_ffn_up": dense(ks[15], (DEPTH, D_MODEL, 2 * D_FF), D_MODEL),
        "conv_w": dense(ks[16], (DEPTH, CONV_WIDTH, 2 * D_FF), CONV_WIDTH),
        "conv_b": 0.02 * jax.random.normal(ks[17], (DEPTH, 2 * D_FF), f32),
        "w_ffn_down": dense(ks[18], (DEPTH, D_FF, D_MODEL), D_FF),
    }


def reference(x, positions, w_in, w_br_a, w_br_b, w_br_c, w_out, lam_q1, lam_k1, lam_q2, lam_k2,
              subln_g, norm_mix_pre, norm_mix_post, norm_ffn_pre, norm_ffn_post,
              w_ffn_up, conv_w, conv_b, w_ffn_down):
    cos, sin = rope_tables(positions)
    for layer in range(DEPTH):
        lam_init = 0.8 - 0.6 * math.exp(-0.3 * layer)
        h = mixer_block(rms_norm(x, norm_mix_pre[layer]), cos, sin, w_in[layer], w_br_a[layer],
                        w_br_b[layer], w_br_c[layer], w_out[layer], lam_q1[layer], lam_k1[layer],
                        lam_q2[layer], lam_k2[layer], lam_init, subln_g[layer])
        x = x + rms_norm(h, norm_mix_post[layer])
        h = conv_ffn(rms_norm(x, norm_ffn_pre[layer]), w_ffn_up[layer], conv_w[layer],
                     conv_b[layer], w_ffn_down[layer])
        x = x + rms_norm(h, norm_ffn_post[layer])
    return x
```

```python
import functools
import math

import jax
import jax.numpy as jnp
from jax import lax
from jax.experimental import pallas as pl
from jax.experimental.pallas import tpu as pltpu

D_MODEL = 1024
HEAD_DIM = 64
ROPE_THETA = 500000.0
ROPE_DIM = HEAD_DIM // 4
NORM_EPS = 1e-6
Q_BLOCK = 128

A_PAIRS = ((128, 1), (512, 4), (2048, 16))
A_GROUPS = len(A_PAIRS)
A_HEADS = 4
A_OUT = A_HEADS * HEAD_DIM
B_HEADS = 4
B_OUT = B_HEADS * HEAD_DIM
IDX_HEADS = 8
IDX_DIM = 64
TOPK_MAX = 256
C_HEADS = 4
C_VDIM = 2 * HEAD_DIM
C_OUT = C_HEADS * C_VDIM
N_BRANCH = 3

A_IN = A_GROUPS * 3 * A_HEADS * HEAD_DIM
B_IN = 3 * B_HEADS * HEAD_DIM
IDX_Q = IDX_HEADS * IDX_DIM
IDX_IN = IDX_Q + IDX_DIM + IDX_HEADS
C_QK = C_HEADS * 2 * HEAD_DIM
C_IN = 2 * C_QK + C_OUT
GATE_IN = N_BRANCH * D_MODEL
D_FF = ((8 * D_MODEL // 3 + 127) // 128) * 128
CONV_WIDTH = 3

LANES = 128
SCALE = HEAD_DIM ** -0.5
NEG = -1e30
INT_MIN = -2 ** 31
VMEM_LIMIT = 56 * 1024 * 1024

F32 = jnp.float32
BF16 = jnp.bfloat16
NT_DIMS = (((1,), (1,)), ((), ()))


def _nt_dot(a, b):
    return lax.dot_general(a, b, NT_DIMS, preferred_element_type=F32)


def _rms(x, g):
    return x * lax.rsqrt(jnp.mean(x * x, axis=-1, keepdims=True) + NORM_EPS) * g


def _inproj_kernel(x_ref, g_ref, ct_ref, s1_ref, s2_ref, wa_ref, wb_ref, wqi_ref, wki_ref, wwi_ref, wc_ref,
                   pa_ref, pb_ref, qi_ref, ki_ref, wi_ref, pc_ref):
    xn = _rms(x_ref[...], g_ref[...]).astype(BF16)
    ct, s1, s2 = ct_ref[...], s1_ref[...], s2_ref[...]

    def rope(y):
        return y * ct + pltpu.roll(y, LANES - ROPE_DIM // 2, 1) * s1 + pltpu.roll(y, ROPE_DIM // 2, 1) * s2

    def project(w_ref, out_ref, n_cols, mode_of_col):
        for c in range(0, n_cols, 2 * LANES):
            y = jnp.dot(xn, w_ref[:, c:c + 2 * LANES], preferred_element_type=F32)
            for h in range(2):
                col = c + h * LANES
                z = y[:, h * LANES:(h + 1) * LANES]
                mode = mode_of_col(col)
                if mode != "v":
                    z = rope(z)
                if mode == "q":
                    z = z * SCALE
                out_ref[:, col:col + LANES] = z.astype(out_ref.dtype)

    qkv_mode = lambda width: (lambda col: ("q", "k", "v")[(col // width) % 3])
    project(wa_ref, pa_ref, A_IN, qkv_mode(A_OUT))
    project(wb_ref, pb_ref, B_IN, qkv_mode(B_OUT))
    project(wqi_ref, qi_ref, IDX_Q, lambda col: "k")
    project(wc_ref, pc_ref, C_IN, qkv_mode(C_QK))
    ki = jnp.dot(xn, wki_ref[...], preferred_element_type=F32)
    ki_ref[...] = rope(ki).astype(BF16)
    wi = jnp.dot(xn, wwi_ref[...], preferred_element_type=F32)
    wi_ref[...] = (wi * (IDX_HEADS ** -0.5)) * (IDX_DIM ** -0.5)


def _inproj(x2, gain, tables, w, tm=512):
    n = x2.shape[0]
    row = lambda width: pl.BlockSpec((tm, width), lambda i: (i, 0))
    full = lambda a: pl.BlockSpec(a.shape, lambda i: (0, 0))
    outs = [(A_IN, BF16), (B_IN, BF16), (IDX_Q, BF16), (LANES, BF16), (LANES, F32), (C_IN, BF16)]
    return pl.pallas_call(
        _inproj_kernel,
        grid=(n // tm,),
        in_specs=[row(D_MODEL), full(gain), row(LANES), row(LANES), row(LANES)] + [full(a) for a in w],
        out_specs=[row(wd) for wd, _ in outs],
        out_shape=[jax.ShapeDtypeStruct((n, wd), dt) for wd, dt in outs],
        compiler_params=pltpu.CompilerParams(dimension_semantics=("arbitrary",), vmem_limit_bytes=VMEM_LIMIT),
        name="inproj",
    )(x2, gain, *tables, *w)


def _softmax_init(m_ref, l_ref, acc_ref):
    m_ref[...] = jnp.full(m_ref.shape, NEG, F32)
    l_ref[...] = jnp.zeros(l_ref.shape, F32)
    acc_ref[...] = jnp.zeros(acc_ref.shape, F32)


def _softmax_update(h, s, v, m_ref, l_ref, acc_ref):
    m_old = m_ref[h]
    m_new = jnp.maximum(m_old, jnp.max(s, axis=1, keepdims=True))
    p = jnp.exp(s - m_new)
    alpha = jnp.exp(m_old - m_new)
    l_ref[h] = alpha * l_ref[h] + jnp.sum(p, axis=1, keepdims=True)
    acc_ref[h] = alpha * acc_ref[h] + jnp.dot(p.astype(BF16), v, preferred_element_type=F32)
    m_ref[h] = m_new


def _head_masks(width):
    lane = lax.broadcasted_iota(jnp.int32, (1, width), 1)
    return [lane // HEAD_DIM == h for h in range(width // HEAD_DIM)]


def _dilated_kernel(*refs):
    group_refs = [refs[5 * g:5 * g + 5] for g in range(A_GROUPS)]
    o_ref, m_ref, l_ref, acc_ref = refs[5 * A_GROUPS:]
    i = pl.program_id(1)
    qs = i * Q_BLOCK
    hmask = _head_masks(A_OUT)
    rows = lax.broadcasted_iota(jnp.int32, (Q_BLOCK, Q_BLOCK), 0)
    cols = lax.broadcasted_iota(jnp.int32, (Q_BLOCK, Q_BLOCK), 1)
    diff = rows - cols
    _softmax_init(m_ref, l_ref, acc_ref)

    for (window, dil), (q_ref, kp_ref, kc_ref, vp_ref, vc_ref) in zip(A_PAIRS, group_refs):
        nblk = window // Q_BLOCK
        q = q_ref[...]
        qm = [jnp.where(hm, q, jnp.zeros_like(q)) for hm in hmask]
        on_stride = ((diff & (dil - 1)) == 0) if dil > 1 else None
        off_blk = (qs % window) // Q_BLOCK

        def block(k_ref, v_ref, local_blk, mask):
            start = pl.multiple_of(local_blk * Q_BLOCK, Q_BLOCK)
            k = k_ref[pl.ds(start, Q_BLOCK), :]
            v = v_ref[pl.ds(start, Q_BLOCK), :]
            for h in range(A_HEADS):
                s = _nt_dot(qm[h], k)
                if mask is not None:
                    s = jnp.where(mask, s, NEG)
                _softmax_update(h, s, v, m_ref, l_ref, acc_ref)

        def and_stride(m):
            return m if on_stride is None else (m & on_stride)

        block(kc_ref, vc_ref, off_blk, and_stride(diff >= 0))

        def mid_cur(t, carry):
            block(kc_ref, vc_ref, off_blk - t, on_stride)
            return carry

        def mid_prev(t, carry):
            block(kp_ref, vp_ref, off_blk - t + nblk, on_stride)
            return carry

        if nblk > 1:
            lax.fori_loop(1, off_blk + 1, mid_cur, 0)
            lax.fori_loop(off_blk + 1, jnp.minimum(nblk - 1, i) + 1, mid_prev, 0)

        @pl.when(i >= nblk)
        def _():
            block(kp_ref, vp_ref, off_blk, and_stride(diff <= 0))

    out = jnp.zeros((Q_BLOCK, A_OUT), F32)
    for h in range(A_HEADS):
        out = out + jnp.where(hmask[h], acc_ref[h] / l_ref[h], 0.0)
    o_ref[...] = out.astype(o_ref.dtype)


def _dilated_attention(pa3):
    b, s, _ = pa3.shape
    in_specs, args = [], []
    for g, (window, _) in enumerate(A_PAIRS):
        per = window // Q_BLOCK
        cur = lambda bi, i, per=per: (i // per)
        prev = lambda bi, i, per=per: jnp.maximum(i // per - 1, 0)
        in_specs.append(pl.BlockSpec((None, Q_BLOCK, A_OUT), lambda bi, i, g=g: (bi, i, 3 * g)))
        for col in (3 * g + 1, 3 * g + 2):
            for which in (prev, cur):
                in_specs.append(pl.BlockSpec((None, window, A_OUT),
                                             lambda bi, i, col=col, which=which: (bi, which(bi, i), col)))
        args += [pa3] * 5
    return pl.pallas_call(
        _dilated_kernel,
        grid=(b, s // Q_BLOCK),
        in_specs=in_specs,
        out_specs=pl.BlockSpec((None, Q_BLOCK, A_OUT), lambda bi, i: (bi, i, 0)),
        out_shape=jax.ShapeDtypeStruct((b, s, A_OUT), BF16),
        scratch_shapes=[pltpu.VMEM((A_HEADS, Q_BLOCK, 1), F32), pltpu.VMEM((A_HEADS, Q_BLOCK, 1), F32),
                        pltpu.VMEM((A_HEADS, Q_BLOCK, A_OUT), F32)],
        compiler_params=pltpu.CompilerParams(dimension_semantics=("arbitrary", "arbitrary"),
                                             vmem_limit_bytes=VMEM_LIMIT),
        name="dilated_attn",
    )(*args)


SEL_CHUNK = 512


def _sparse_kernel(qi_ref, wi_ref, ki_ref, q_ref, k_ref, v_ref, o_ref, key_ref, m_ref, l_ref, acc_ref, *, topk):
    i = pl.program_id(1)
    qs = i * Q_BLOCK
    nch = qs // SEL_CHUNK + 1
    lane = lax.broadcasted_iota(jnp.int32, (1, LANES), 1)
    qpos = qs + lax.broadcasted_iota(jnp.int32, (Q_BLOCK, 1), 0)

    qi = qi_ref[...]
    stacked = []
    for h in range(IDX_HEADS):
        pair = qi[:, (h // 2) * LANES:(h // 2 + 1) * LANES]
        keep = (lane < IDX_DIM) if h % 2 == 0 else (lane >= IDX_DIM)
        stacked.append(jnp.where(keep, pair, jnp.zeros_like(pair)))
    stacked = jnp.concatenate(stacked, axis=0)
    w = wi_ref[...]
    wcol = [w[:, h:h + 1] for h in range(IDX_HEADS)]

    def score_chunk(c, carry):
        start = pl.multiple_of(c * SEL_CHUNK, SEL_CHUNK)
        kc = ki_ref[pl.ds(start, SEL_CHUNK), :]
        logits = _nt_dot(stacked, kc)
        score = jnp.zeros((Q_BLOCK, SEL_CHUNK), F32)
        for h in range(IDX_HEADS):
            score = score + jnp.maximum(logits[h * Q_BLOCK:(h + 1) * Q_BLOCK], 0.0) * wcol[h]
        bits = lax.bitcast_convert_type(score, jnp.int32)
        key = bits ^ ((bits >> 31) & 0x7FFFFFFF)
        kpos = start + lax.broadcasted_iota(jnp.int32, (1, SEL_CHUNK), 1)
        key_ref[:, pl.ds(start, SEL_CHUNK)] = jnp.where(kpos <= qpos, key, INT_MIN)
        return carry

    lax.fori_loop(0, nch, score_chunk, 0)

    def count(pred):
        def body(c, acc):
            start = pl.multiple_of(c * SEL_CHUNK, SEL_CHUNK)
            hit = jnp.where(pred(key_ref[:, pl.ds(start, SEL_CHUNK)]), 1.0, 0.0)
            for j in range(SEL_CHUNK // LANES):
                acc = acc + hit[:, j * LANES:(j + 1) * LANES]
            return acc
        acc = lax.fori_loop(0, nch, body, jnp.zeros((Q_BLOCK, LANES), F32))
        return jnp.sum(acc, axis=1, keepdims=True)

    kf = float(topk)
    cnt0 = count(lambda blk: blk >= 0)
    thr = jnp.where(cnt0 >= kf, 0, INT_MIN).astype(jnp.int32)
    cnt_thr = jnp.where(cnt0 >= kf, cnt0, kf)

    def bit_step(t, carry):
        thr, cnt_thr = carry
        cand = thr + jnp.left_shift(jnp.int32(1), 30 - t)
        cnt = count(lambda blk: blk >= cand)
        ok = cnt >= kf
        return jnp.where(ok, cand, thr), jnp.where(ok, cnt, cnt_thr)

    thr, cnt_thr = lax.fori_loop(0, 31, bit_step, (thr, cnt_thr))
    thr = jnp.maximum(thr, INT_MIN + 1)
    has_ties = jnp.max(cnt_thr) > kf

    q = q_ref[...]
    hmask = _head_masks(B_OUT)
    qm = [jnp.where(hm, q, jnp.zeros_like(q)) for hm in hmask]

    def attend(c, sel):
        start = pl.multiple_of(c * SEL_CHUNK, SEL_CHUNK)
        k = k_ref[pl.ds(start, SEL_CHUNK), :]
        v = v_ref[pl.ds(start, SEL_CHUNK), :]
        for h in range(B_HEADS):
            s = jnp.where(sel, _nt_dot(qm[h], k), NEG)
            _softmax_update(h, s, v, m_ref, l_ref, acc_ref)

    _softmax_init(m_ref, l_ref, acc_ref)

    @pl.when(jnp.logical_not(has_ties))
    def _():
        def body(c, carry):
            start = pl.multiple_of(c * SEL_CHUNK, SEL_CHUNK)
            attend(c, key_ref[:, pl.ds(start, SEL_CHUNK)] >= thr)
            return carry
        lax.fori_loop(0, nch, body, 0)

    @pl.when(has_ties)
    def _():
        need = kf - count(lambda blk: blk > thr)
        r = lax.broadcasted_iota(jnp.int32, (SEL_CHUNK, SEL_CHUNK), 0)
        cc = lax.broadcasted_iota(jnp.int32, (SEL_CHUNK, SEL_CHUNK), 1)
        before = jnp.where(r < cc, 1.0, 0.0).astype(BF16)

        def body(c, seen):
            start = pl.multiple_of(c * SEL_CHUNK, SEL_CHUNK)
            blk = key_ref[:, pl.ds(start, SEL_CHUNK)]
            eq = blk == thr
            eqf = jnp.where(eq, 1.0, 0.0)
            rank = seen + jnp.dot(eqf.astype(BF16), before, preferred_element_type=F32)
            attend(c, (blk > thr) | (eq & (rank < need)))
            return seen + jnp.sum(eqf, axis=1, keepdims=True)
        lax.fori_loop(0, nch, body, jnp.zeros((Q_BLOCK, 1), F32))

    out = jnp.zeros((Q_BLOCK, B_OUT), F32)
    for h in range(B_HEADS):
        out = out + jnp.where(hmask[h], acc_ref[h] / l_ref[h], 0.0)
    o_ref[...] = out.astype(o_ref.dtype)


def _sparse_attention(qi3, wi3, ki3, pb3):
    b, s, _ = pb3.shape
    topk = min(TOPK_MAX, s // 4)
    qblk = lambda width, col: pl.BlockSpec((None, Q_BLOCK, width), lambda bi, i: (bi, i, col))
    seq = lambda width, col: pl.BlockSpec((None, s, width), lambda bi, i: (bi, 0, col))
    return pl.pallas_call(
        functools.partial(_sparse_kernel, topk=topk),
        grid=(b, s // Q_BLOCK),
        in_specs=[qblk(IDX_Q, 0), qblk(LANES, 0), seq(LANES, 0), qblk(B_OUT, 0), seq(B_OUT, 1), seq(B_OUT, 2)],
        out_specs=pl.BlockSpec((None, Q_BLOCK, B_OUT), lambda bi, i: (bi, i, 0)),
        out_shape=jax.ShapeDtypeStruct((b, s, B_OUT), BF16),
        scratch_shapes=[pltpu.VMEM((Q_BLOCK, s), jnp.int32),
                        pltpu.VMEM((B_HEADS, Q_BLOCK, 1), F32), pltpu.VMEM((B_HEADS, Q_BLOCK, 1), F32),
                        pltpu.VMEM((B_HEADS, Q_BLOCK, B_OUT), F32)],
        compiler_params=pltpu.CompilerParams(dimension_semantics=("arbitrary", "arbitrary"),
                                             vmem_limit_bytes=VMEM_LIMIT),
        name="sparse_attn",
    )(qi3, wi3, ki3, pb3, pb3, pb3)


DIFF_TQ = 256
DIFF_TK = 256


def _diff_kernel(lam_ref, g_ref, q_ref, k_ref, v_ref, o_ref, m_ref, l_ref, acc_ref, *, out_scale):
    i = pl.program_id(2)
    q = q_ref[...]
    lane = lax.broadcasted_iota(jnp.int32, (1, LANES), 1)
    qm = [jnp.where(lane < HEAD_DIM, q, jnp.zeros_like(q)), jnp.where(lane >= HEAD_DIM, q, jnp.zeros_like(q))]
    _softmax_init(m_ref, l_ref, acc_ref)

    def block(kb, mask):
        start = pl.multiple_of(kb * DIFF_TK, DIFF_TK)
        k = k_ref[pl.ds(start, DIFF_TK), :]
        v = v_ref[pl.ds(start, DIFF_TK), :]
        for c in range(2):
            s = _nt_dot(qm[c], k)
            if mask is not None:
                s = jnp.where(mask, s, NEG)
            _softmax_update(c, s, v, m_ref, l_ref, acc_ref)

    def body(kb, carry):
        block(kb, None)
        return carry

    lax.fori_loop(0, i, body, 0)
    rows = lax.broadcasted_iota(jnp.int32, (DIFF_TQ, DIFF_TK), 0)
    cols = lax.broadcasted_iota(jnp.int32, (DIFF_TQ, DIFF_TK), 1)
    block(i, rows >= cols)

    o = acc_ref[0] / l_ref[0] - lam_ref[0] * (acc_ref[1] / l_ref[1])
    o_ref[...] = (_rms(o, g_ref[...]) * out_scale).astype(o_ref.dtype)


def _diff_attention(pc3, lam, subln_g, lam_init):
    b, s, _ = pc3.shape
    nq = C_QK // LANES
    return pl.pallas_call(
        functools.partial(_diff_kernel, out_scale=1.0 - lam_init),
        grid=(b, C_HEADS, s // DIFF_TQ),
        in_specs=[pl.BlockSpec(memory_space=pltpu.SMEM),
                  pl.BlockSpec((1, C_VDIM), lambda bi, h, i: (0, 0)),
                  pl.BlockSpec((None, DIFF_TQ, LANES), lambda bi, h, i: (bi, i, h)),
                  pl.BlockSpec((None, s, LANES), lambda bi, h, i: (bi, 0, nq + h)),
                  pl.BlockSpec((None, s, C_VDIM), lambda bi, h, i: (bi, 0, 2 * nq + h))],
        out_specs=pl.BlockSpec((None, DIFF_TQ, C_VDIM), lambda bi, h, i: (bi, i, h)),
        out_shape=jax.ShapeDtypeStruct((b, s, C_OUT), BF16),
        scratch_shapes=[pltpu.VMEM((2, DIFF_TQ, 1), F32), pltpu.VMEM((2, DIFF_TQ, 1), F32),
                        pltpu.VMEM((2, DIFF_TQ, C_VDIM), F32)],
        compiler_params=pltpu.CompilerParams(dimension_semantics=("arbitrary", "arbitrary", "arbitrary"),
                                             vmem_limit_bytes=VMEM_LIMIT),
        name="diff_attn",
    )(lam, subln_g, pc3, pc3, pc3)


def _merge_kernel(x_ref, gpre_ref, gpost_ref, oa_ref, ob_ref, oc_ref, wg_ref, wa_ref, wb_ref, wc_ref, wo_ref,
                  out_ref, y_ref, h_ref):
    x = x_ref[...]
    xn = _rms(x, gpre_ref[...]).astype(BF16)
    branches = ((oa_ref, wa_ref), (ob_ref, wb_ref), (oc_ref, wc_ref))
    cw = 2 * LANES
    for c in range(0, D_MODEL, cw):
        y = jnp.zeros((x.shape[0], cw), F32)
        for j, (o_ref, w_ref) in enumerate(branches):
            logit = jnp.dot(xn, wg_ref[:, j * D_MODEL + c:j * D_MODEL + c + cw], preferred_element_type=F32)
            gate = 1.0 / (1.0 + jnp.exp(-logit))
            y = y + gate * jnp.dot(o_ref[...], w_ref[:, c:c + cw], preferred_element_type=F32)
        y_ref[:, c:c + cw] = y.astype(BF16)
    ssq = jnp.zeros((x.shape[0], 1), F32)
    for c in range(0, D_MODEL, cw):
        h = jnp.dot(y_ref[...], wo_ref[:, c:c + cw], preferred_element_type=F32)
        h_ref[:, c:c + cw] = h
        ssq = ssq + jnp.sum(h * h, axis=1, keepdims=True)
    inv = lax.rsqrt(ssq * (1.0 / D_MODEL) + NORM_EPS)
    out_ref[...] = x + h_ref[...] * inv * gpost_ref[...]


def _merge(x2, gpre, gpost, oa, ob, oc, w, tm=512):
    n = x2.shape[0]
    row = lambda width: pl.BlockSpec((tm, width), lambda i: (i, 0))
    full = lambda a: pl.BlockSpec(a.shape, lambda i: (0, 0))
    return pl.pallas_call(
        _merge_kernel,
        grid=(n // tm,),
        in_specs=[row(D_MODEL), full(gpre), full(gpost), row(A_OUT), row(B_OUT), row(C_OUT)] + [full(a) for a in w],
        out_specs=row(D_MODEL),
        out_shape=jax.ShapeDtypeStruct((n, D_MODEL), F32),
        scratch_shapes=[pltpu.VMEM((tm, D_MODEL), BF16), pltpu.VMEM((tm, D_MODEL), F32)],
        compiler_params=pltpu.CompilerParams(dimension_semantics=("arbitrary",), vmem_limit_bytes=VMEM_LIMIT),
        name="merge",
    )(x2, gpre, gpost, oa, ob, oc, *w)


FFN_CHUNK = 256
CARRY_ROWS = 8


def _ffn_kernel(x_ref, gpre_ref, gpost_ref, wup_ref, cw_ref, cb_ref, wdn_ref, out_ref, carry_ref, acc_ref,
                *, tiles_per_seq):
    x = x_ref[...]
    tm = x.shape[0]
    xn = _rms(x, gpre_ref[...]).astype(BF16)
    first = pl.program_id(0) % tiles_per_seq == 0
    row = lax.broadcasted_iota(jnp.int32, (tm, 1), 0)

    def conv(col):
        h = jnp.dot(xn, wup_ref[:, col:col + FFN_CHUNK], preferred_element_type=F32)
        prev = jnp.where(first, 0.0, carry_ref[:, col:col + FFN_CHUNK])
        carry_ref[:, col:col + FFN_CHUNK] = h[tm - CARRY_ROWS:, :]
        p1 = prev[CARRY_ROWS - 1:CARRY_ROWS, :]
        p2 = prev[CARRY_ROWS - 2:CARRY_ROWS - 1, :]
        h1 = jnp.where(row == 0, p1, pltpu.roll(h, 1, 0))
        h2 = jnp.where(row == 0, p2, jnp.where(row == 1, p1, pltpu.roll(h, 2, 0)))
        w = cw_ref[:, col:col + FFN_CHUNK]
        return h2 * w[0:1, :] + h1 * w[1:2, :] + h * w[2:3, :] + cb_ref[:, col:col + FFN_CHUNK]

    for c in range(0, D_FF, FFN_CHUNK):
        g = conv(c)
        u = conv(D_FF + c)
        gelu = 0.5 * g * (1.0 + jnp.tanh(math.sqrt(2.0 / math.pi) * (g + 0.044715 * (g * g * g))))
        part = jnp.dot((gelu * u).astype(BF16), wdn_ref[c:c + FFN_CHUNK, :], preferred_element_type=F32)
        if c == 0:
            acc_ref[...] = part
        else:
            acc_ref[...] += part
    out_ref[...] = x + _rms(acc_ref[...], gpost_ref[...])


def _ffn(x2, gpre, gpost, wup, conv_w, conv_b, wdn, seq_len, tm=512):
    n = x2.shape[0]
    row = lambda width: pl.BlockSpec((tm, width), lambda i: (i, 0))
    full = lambda a: pl.BlockSpec(a.shape, lambda i: (0, 0))
    return pl.pallas_call(
        functools.partial(_ffn_kernel, tiles_per_seq=seq_len // tm),
        grid=(n // tm,),
        in_specs=[row(D_MODEL), full(gpre), full(gpost), full(wup), full(conv_w), full(conv_b), full(wdn)],
        out_specs=row(D_MODEL),
        out_shape=jax.ShapeDtypeStruct((n, D_MODEL), F32),
        scratch_shapes=[pltpu.VMEM((CARRY_ROWS, 2 * D_FF), F32), pltpu.VMEM((tm, D_MODEL), F32)],
        compiler_params=pltpu.CompilerParams(dimension_semantics=("arbitrary",), vmem_limit_bytes=VMEM_LIMIT),
        name="conv_ffn",
    )(x2, gpre, gpost, wup, conv_w, conv_b, wdn)


def _rope_lane_tables(positions):
    half = ROPE_DIM // 2
    inv = ROPE_THETA ** (-jnp.arange(0, ROPE_DIM, 2, dtype=F32) / ROPE_DIM)
    ang = positions.astype(F32).reshape(-1, 1) * inv
    cos, sin = jnp.cos(ang), jnp.sin(ang)
    n = ang.shape[0]
    rest = HEAD_DIM - ROPE_DIM
    ones, zeros, zh = jnp.ones((n, rest), F32), jnp.zeros((n, rest), F32), jnp.zeros((n, half), F32)
    ct = jnp.concatenate([cos, cos, ones], axis=1)
    s1 = jnp.concatenate([-sin, zh, zeros], axis=1)
    s2 = jnp.concatenate([zh, sin, zeros], axis=1)
    return tuple(jnp.tile(t, (1, LANES // HEAD_DIM)) for t in (ct, s1, s2))


def _split_w_in(w_in):
    o_b, o_i, o_c, o_g = A_IN, A_IN + B_IN, A_IN + B_IN + IDX_IN, A_IN + B_IN + IDX_IN + C_IN
    wk = w_in[:, o_i + IDX_Q:o_i + IDX_Q + IDX_DIM]
    ww = w_in[:, o_i + IDX_Q + IDX_DIM:o_c]
    proj = (w_in[:, :o_b], w_in[:, o_b:o_i], w_in[:, o_i:o_i + IDX_Q],
            jnp.concatenate([wk, wk], axis=1),
            jnp.pad(ww, ((0, 0), (0, LANES - IDX_HEADS))),
            w_in[:, o_c:o_g])
    return tuple(a.astype(BF16) for a in proj), w_in[:, o_g:].astype(BF16)


def kernel(x, positions, w_in, w_br_a, w_br_b, w_br_c, w_out, lam_q1, lam_k1, lam_q2, lam_k2, subln_g,
           norm_mix_pre, norm_mix_post, norm_ffn_pre, norm_ffn_post, w_ffn_up, conv_w, conv_b, w_ffn_down):
    b, s, d = x.shape
    depth = w_in.shape[0]
    assert d == D_MODEL and s % A_PAIRS[-1][0] == 0 and s % SEL_CHUNK == 0 and s % DIFF_TQ == 0
    tables = _rope_lane_tables(positions)
    x2 = x.reshape(b * s, d)
    vec = lambda a: a.reshape(1, -1)
    for layer in range(depth):
        lam_init = 0.8 - 0.6 * math.exp(-0.3 * layer)
        lam = (jnp.exp(jnp.sum(lam_q1[layer] * lam_k1[layer])) - jnp.exp(jnp.sum(lam_q2[layer] * lam_k2[layer]))
               + lam_init).reshape(1).astype(F32)
        w_proj, w_gate = _split_w_in(w_in[layer])
        pa, pb, qi, ki, wi, pc = _inproj(x2, vec(norm_mix_pre[layer]), tables, w_proj)
        seq = lambda a: a.reshape(b, s, a.shape[-1])
        oa = _dilated_attention(seq(pa))
        ob = _sparse_attention(seq(qi), seq(wi), seq(ki), seq(pb))
        oc = _diff_attention(seq(pc), lam, vec(subln_g[layer]), lam_init)
        flat = lambda a: a.reshape(b * s, a.shape[-1])
        w_merge = (w_gate, w_br_a[layer].astype(BF16), w_br_b[layer].astype(BF16), w_br_c[layer].astype(BF16),
                   w_out[layer].astype(BF16))
        x2 = _merge(x2, vec(norm_mix_pre[layer]), vec(norm_mix_post[layer]), flat(oa), flat(ob), flat(oc), w_merge)
        x2 = _ffn(x2, vec(norm_ffn_pre[layer]), vec(norm_ffn_post[layer]), w_ffn_up[layer].astype(BF16),
                  conv_w[layer], vec(conv_b[layer]), w_ffn_down[layer].astype(BF16), s)
    return x2.reshape(b, s, d)
```

```python
import functools
import math

import jax
import jax.numpy as jnp
from jax import lax
from jax.experimental import pallas as pl
from jax.experimental.pallas import tpu as pltpu

D_MODEL = 1024
HEAD_DIM = 64
ROPE_THETA = 500000.0
ROPE_DIM = HEAD_DIM // 4
NORM_EPS = 1e-6
Q_BLOCK = 128

A_PAIRS = ((128, 1), (512, 4), (2048, 16))
A_GROUPS = len(A_PAIRS)
A_HEADS = 4
A_OUT = A_HEADS * HEAD_DIM
B_HEADS = 4
B_OUT = B_HEADS * HEAD_DIM
IDX_HEADS = 8
IDX_DIM = 64
TOPK_MAX = 256
C_HEADS = 4
C_VDIM = 2 * HEAD_DIM
C_OUT = C_HEADS * C_VDIM
N_BRANCH = 3

A_IN = A_GROUPS * 3 * A_HEADS * HEAD_DIM
B_IN = 3 * B_HEADS * HEAD_DIM
IDX_Q = IDX_HEADS * IDX_DIM
IDX_IN = IDX_Q + IDX_DIM + IDX_HEADS
C_QK = C_HEADS * 2 * HEAD_DIM
C_IN = 2 * C_QK + C_OUT
GATE_IN = N_BRANCH * D_MODEL
D_FF = ((8 * D_MODEL // 3 + 127) // 128) * 128
CONV_WIDTH = 3

LANES = 128
SCALE = HEAD_DIM ** -0.5
NEG = -1e30
INT_MIN = -2 ** 31
VMEM_LIMIT = 56 * 1024 * 1024

F32 = jnp.float32
BF16 = jnp.bfloat16
NT_DIMS = (((1,), (1,)), ((), ()))


def _nt_dot(a, b):
    return lax.dot_general(a, b, NT_DIMS, preferred_element_type=F32)


def _rms(x, g):
    return x * lax.rsqrt(jnp.mean(x * x, axis=-1, keepdims=True) + NORM_EPS) * g


def _inproj_kernel(x_ref, g_ref, ct_ref, s1_ref, s2_ref, wa_ref, wb_ref, wqi_ref, wki_ref, wwi_ref, wc_ref,
                   pa_ref, pb_ref, qi_ref, ki_ref, wi_ref, pc_ref):
    xn = _rms(x_ref[...], g_ref[...]).astype(BF16)
    ct, s1, s2 = ct_ref[...], s1_ref[...], s2_ref[...]

    def rope(y):
        return y * ct + pltpu.roll(y, LANES - ROPE_DIM // 2, 1) * s1 + pltpu.roll(y, ROPE_DIM // 2, 1) * s2

    def project(w_ref, out_ref, n_cols, mode_of_col):
        for c in range(0, n_cols, 2 * LANES):
            y = jnp.dot(xn, w_ref[:, c:c + 2 * LANES], preferred_element_type=F32)
            for h in range(2):
                col = c + h * LANES
                z = y[:, h * LANES:(h + 1) * LANES]
                mode = mode_of_col(col)
                if mode != "v":
                    z = rope(z)
                if mode == "q":
                    z = z * SCALE
                out_ref[:, col:col + LANES] = z.astype(out_ref.dtype)

    qkv_mode = lambda width: (lambda col: ("q", "k", "v")[(col // width) % 3])
    project(wa_ref, pa_ref, A_IN, qkv_mode(A_OUT))
    project(wb_ref, pb_ref, B_IN, qkv_mode(B_OUT))
    project(wqi_ref, qi_ref, IDX_Q, lambda col: "k")
    project(wc_ref, pc_ref, C_IN, qkv_mode(C_QK))
    ki = jnp.dot(xn, wki_ref[...], preferred_element_type=F32)
    ki_ref[...] = rope(ki).astype(BF16)
    wi = jnp.dot(xn, wwi_ref[...], preferred_element_type=F32)
    wi_ref[...] = (wi * (IDX_HEADS ** -0.5)) * (IDX_DIM ** -0.5)


def _inproj(x2, gain, tables, w, tm=512):
    n = x2.shape[0]
    row = lambda width: pl.BlockSpec((tm, width), lambda i: (i, 0))
    full = lambda a: pl.BlockSpec(a.shape, lambda i: (0, 0))
    outs = [(A_IN, BF16), (B_IN, BF16), (IDX_Q, BF16), (LANES, BF16), (LANES, F32), (C_IN, BF16)]
    return pl.pallas_call(
        _inproj_kernel,
        grid=(n // tm,),
        in_specs=[row(D_MODEL), full(gain), row(LANES), row(LANES), row(LANES)] + [full(a) for a in w],
        out_specs=[row(wd) for wd, _ in outs],
        out_shape=[jax.ShapeDtypeStruct((n, wd), dt) for wd, dt in outs],
        compiler_params=pltpu.CompilerParams(dimension_semantics=("arbitrary",), vmem_limit_bytes=VMEM_LIMIT),
        name="inproj",
    )(x2, gain, *tables, *w)


def _head_masks(width):
    lane = lax.broadcasted_iota(jnp.int32, (1, width), 1)
    return [lane // HEAD_DIM == h for h in range(width // HEAD_DIM)]


def _stack_heads(q):
    zero = jnp.zeros_like(q)
    return jnp.concatenate([jnp.where(hm, q, zero) for hm in _head_masks(q.shape[1])], axis=0)


def _unstack_heads(x, rows):
    hmask = _head_masks(HEAD_DIM * (x.shape[0] // rows))
    out = jnp.where(hmask[0], x[0:rows], 0.0)
    for h in range(1, len(hmask)):
        out = jnp.where(hmask[h], x[h * rows:(h + 1) * rows], out)
    return out


def _lane_tile(x, width):
    reps = width // LANES
    return x if reps == 1 else jnp.concatenate([x] * reps, axis=1)


def _flash_init(m_ref, l_ref, acc_ref):
    m_ref[...] = jnp.full(m_ref.shape, NEG, F32)
    l_ref[...] = jnp.zeros(l_ref.shape, F32)
    acc_ref[...] = jnp.zeros(acc_ref.shape, F32)


def _flash_step(s, v, m_ref, l_ref, acc_ref):
    tk = s.shape[1]
    m_old = m_ref[...]
    m_new = jnp.maximum(m_old, jnp.max(s, axis=1, keepdims=True))
    p = jnp.exp(s - _lane_tile(m_new, tk))
    alpha = jnp.exp(m_old - m_new)
    psum = p[:, :LANES]
    for j in range(1, tk // LANES):
        psum = psum + p[:, j * LANES:(j + 1) * LANES]
    l_ref[...] = alpha * l_ref[...] + psum
    acc_ref[...] = (_lane_tile(alpha, acc_ref.shape[1]) * acc_ref[...]
                    + jnp.dot(p.astype(BF16), v, preferred_element_type=F32))
    m_ref[...] = m_new


def _flash_result(l_ref, acc_ref):
    return acc_ref[...] / jnp.sum(l_ref[...], axis=1, keepdims=True)


A_TILE = max(w for w, _ in A_PAIRS)


def _dilated_kernel(*refs):
    group_refs = [refs[5 * g:5 * g + 5] for g in range(A_GROUPS)]
    o_ref, qs_ref, ks_ref, vs_ref, m_sc, l_sc, n_sc = refs[5 * A_GROUPS:]
    j = pl.program_id(1)
    qi = lax.broadcasted_iota(jnp.int32, (Q_BLOCK, 2 * Q_BLOCK), 0)
    kj = lax.broadcasted_iota(jnp.int32, (Q_BLOCK, 2 * Q_BLOCK), 1)
    rel = kj - qi
    band = (rel >= 0) & (rel <= Q_BLOCK)
    bias_band = jnp.where(band, 0.0, NEG)
    bias_first = jnp.where(band & (kj >= Q_BLOCK), 0.0, NEG)
    halves = A_OUT // LANES

    def get(ref, rows):
        return jnp.concatenate([ref[hh, rows, :] for hh in range(halves)], axis=1)

    def put(ref, rows, val):
        for hh in range(halves):
            ref[hh, rows, :] = val[:, hh * LANES:(hh + 1) * LANES]

    for g, ((window, dil), (q_ref, kp_ref, kc_ref, vp_ref, vc_ref)) in enumerate(zip(A_PAIRS, group_refs)):
        assert window == Q_BLOCK * dil
        put(qs_ref, slice(0, A_TILE), q_ref[...].astype(F32))
        put(ks_ref, slice(0, window), kp_ref[...].astype(F32))
        put(ks_ref, slice(window, window + A_TILE), kc_ref[...].astype(F32))
        put(vs_ref, slice(0, window), vp_ref[...].astype(F32))
        put(vs_ref, slice(window, window + A_TILE), vc_ref[...].astype(F32))
        shift = dil.bit_length() - 1

        def problem(p, carry, g=g, window=window, dil=dil, shift=shift):
            sub = p >> shift
            base = sub * window + (p & (dil - 1))
            stride = dil if dil > 1 else None
            q_rows = pl.ds(base, Q_BLOCK, stride=stride)
            k_rows = pl.ds(base, 2 * Q_BLOCK, stride=stride)
            qstack = _stack_heads(get(qs_ref, q_rows).astype(BF16))
            k = get(ks_ref, k_rows).astype(BF16)
            v = get(vs_ref, k_rows).astype(BF16)
            bias = jnp.where((sub > 0) | (j > 0), bias_band, bias_first)
            s = _nt_dot(qstack, k) + jnp.concatenate([bias] * A_HEADS, axis=0)
            m = jnp.max(s, axis=1, keepdims=True)
            e = jnp.exp(s - m)
            l = jnp.sum(e, axis=1, keepdims=True)
            num = jnp.dot(e.astype(BF16), v, preferred_element_type=F32)
            m_n, l_n, num_n = (_unstack_heads(t, Q_BLOCK) for t in (m, l, num))
            if g == 0:
                put(m_sc, q_rows, m_n)
                put(l_sc, q_rows, l_n)
                put(n_sc, q_rows, num_n)
            else:
                m_o = get(m_sc, q_rows)
                m_x = jnp.maximum(m_o, m_n)
                a, b = jnp.exp(m_o - m_x), jnp.exp(m_n - m_x)
                put(m_sc, q_rows, m_x)
                put(l_sc, q_rows, a * get(l_sc, q_rows) + b * l_n)
                put(n_sc, q_rows, a * get(n_sc, q_rows) + b * num_n)
            return carry

        lax.fori_loop(0, A_TILE // Q_BLOCK, problem, 0)

    all_rows = slice(0, A_TILE)
    o_ref[...] = (get(n_sc, all_rows) / get(l_sc, all_rows)).astype(o_ref.dtype)


def _dilated_attention(pa3):
    b, s, _ = pa3.shape
    in_specs = []
    for g, (window, _) in enumerate(A_PAIRS):
        per = A_TILE // window
        cur = lambda bi, j, col: (bi, j, col)
        prev = lambda bi, j, col, per=per: (bi, jnp.maximum(j * per - 1, 0), col)
        in_specs.append(pl.BlockSpec((None, A_TILE, A_OUT), functools.partial(cur, col=3 * g)))
        for col in (3 * g + 1, 3 * g + 2):
            in_specs.append(pl.BlockSpec((None, window, A_OUT), functools.partial(prev, col=col)))
            in_specs.append(pl.BlockSpec((None, A_TILE, A_OUT), functools.partial(cur, col=col)))
    stage = lambda rows: pltpu.VMEM((A_OUT // LANES, rows, LANES), F32)
    return pl.pallas_call(
        _dilated_kernel,
        grid=(b, s // A_TILE),
        in_specs=in_specs,
        out_specs=pl.BlockSpec((None, A_TILE, A_OUT), lambda bi, j: (bi, j, 0)),
        out_shape=jax.ShapeDtypeStruct((b, s, A_OUT), BF16),
        scratch_shapes=[stage(A_TILE), stage(2 * A_TILE), stage(2 * A_TILE),
                        stage(A_TILE), stage(A_TILE), stage(A_TILE)],
        compiler_params=pltpu.CompilerParams(dimension_semantics=("arbitrary", "arbitrary"),
                                             vmem_limit_bytes=VMEM_LIMIT),
        name="dilated_attn",
    )(*([pa3] * (5 * A_GROUPS)))


SEL_CHUNK = 512
ATT_CHUNK = 256


def _sparse_kernel(qi_ref, wi_ref, ki_ref, q_ref, k_ref, v_ref, o_ref, key_ref, m_ref, l_ref, acc_ref, *, topk):
    i = pl.program_id(1)
    qs = i * Q_BLOCK
    nch = qs // SEL_CHUNK + 1
    nat = qs // ATT_CHUNK + 1
    lane = lax.broadcasted_iota(jnp.int32, (1, LANES), 1)
    qpos = qs + lax.broadcasted_iota(jnp.int32, (Q_BLOCK, 1), 0)

    qi = qi_ref[...]
    stacked = []
    for h in range(IDX_HEADS):
        pair = qi[:, (h // 2) * LANES:(h // 2 + 1) * LANES]
        keep = (lane < IDX_DIM) if h % 2 == 0 else (lane >= IDX_DIM)
        stacked.append(jnp.where(keep, pair, jnp.zeros_like(pair)))
    stacked = jnp.concatenate(stacked, axis=0)
    w = wi_ref[...]
    wcol = [w[:, h:h + 1] for h in range(IDX_HEADS)]

    def score_chunk(c, carry):
        start = pl.multiple_of(c * SEL_CHUNK, SEL_CHUNK)
        kc = ki_ref[pl.ds(start, SEL_CHUNK), :]
        logits = _nt_dot(stacked, kc)
        score = jnp.zeros((Q_BLOCK, SEL_CHUNK), F32)
        for h in range(IDX_HEADS):
            score = score + jnp.maximum(logits[h * Q_BLOCK:(h + 1) * Q_BLOCK], 0.0) * wcol[h]
        bits = lax.bitcast_convert_type(score, jnp.int32)
        key = bits ^ ((bits >> 31) & 0x7FFFFFFF)
        kpos = start + lax.broadcasted_iota(jnp.int32, (1, SEL_CHUNK), 1)
        key_ref[:, pl.ds(start, SEL_CHUNK)] = jnp.where(kpos <= qpos, key, INT_MIN)
        return carry

    lax.fori_loop(0, nch, score_chunk, 0)

    def count(pred):
        def body(c, acc):
            start = pl.multiple_of(c * SEL_CHUNK, SEL_CHUNK)
            hit = jnp.where(pred(key_ref[:, pl.ds(start, SEL_CHUNK)]), 1.0, 0.0)
            for jj in range(SEL_CHUNK // LANES):
                acc = acc + hit[:, jj * LANES:(jj + 1) * LANES]
            return acc
        acc = lax.fori_loop(0, nch, body, jnp.zeros((Q_BLOCK, LANES), F32))
        return jnp.sum(acc, axis=1, keepdims=True)

    kf = float(topk)
    cnt0 = count(lambda blk: blk >= 0)
    thr0 = jnp.where(cnt0 >= kf, 0, INT_MIN).astype(jnp.int32)
    cnt_thr0 = jnp.where(cnt0 >= kf, cnt0, kf)

    def unresolved(cnt_thr):
        return jnp.max(cnt_thr) > kf

    def bit_cond(carry):
        bit, _, _, more = carry
        return (bit >= 0) & more

    def bit_step(carry):
        bit, thr, cnt_thr, _ = carry
        cand = thr + jnp.left_shift(jnp.int32(1), bit)
        cnt = count(lambda blk: blk >= cand)
        ok = cnt >= kf
        cnt_thr = jnp.where(ok, cnt, cnt_thr)
        return bit - 1, jnp.where(ok, cand, thr), cnt_thr, unresolved(cnt_thr)

    _, thr, cnt_thr, has_ties = lax.while_loop(bit_cond, bit_step,
                                               (jnp.int32(30), thr0, cnt_thr0, unresolved(cnt_thr0)))
    thr = jnp.maximum(thr, INT_MIN + 1)

    qstack = _stack_heads(q_ref[...])

    def attend(c, sel):
        start = pl.multiple_of(c * ATT_CHUNK, ATT_CHUNK)
        k = k_ref[pl.ds(start, ATT_CHUNK), :]
        v = v_ref[pl.ds(start, ATT_CHUNK), :]
        bias = jnp.where(sel, 0.0, NEG)
        s = _nt_dot(qstack, k) + jnp.concatenate([bias] * B_HEADS, axis=0)
        _flash_step(s, v, m_ref, l_ref, acc_ref)

    _flash_init(m_ref, l_ref, acc_ref)

    @pl.when(jnp.logical_not(has_ties))
    def _():
        def body(c, carry):
            start = pl.multiple_of(c * ATT_CHUNK, ATT_CHUNK)
            attend(c, key_ref[:, pl.ds(start, ATT_CHUNK)] >= thr)
            return carry
        lax.fori_loop(0, nat, body, 0)

    @pl.when(has_ties)
    def _():
        need = kf - count(lambda blk: blk > thr)
        r = lax.broadcasted_iota(jnp.int32, (ATT_CHUNK, ATT_CHUNK), 0)
        cc = lax.broadcasted_iota(jnp.int32, (ATT_CHUNK, ATT_CHUNK), 1)
        before = jnp.where(r < cc, 1.0, 0.0).astype(BF16)

        def body(c, seen):
            start = pl.multiple_of(c * ATT_CHUNK, ATT_CHUNK)
            blk = key_ref[:, pl.ds(start, ATT_CHUNK)]
            eq = blk == thr
            eqf = jnp.where(eq, 1.0, 0.0)
            rank = seen + jnp.dot(eqf.astype(BF16), before, preferred_element_type=F32)
            attend(c, (blk > thr) | (eq & (rank < need)))
            return seen + jnp.sum(eqf, axis=1, keepdims=True)
        lax.fori_loop(0, nat, body, jnp.zeros((Q_BLOCK, 1), F32))

    o_ref[...] = _unstack_heads(_flash_result(l_ref, acc_ref), Q_BLOCK).astype(o_ref.dtype)


def _sparse_attention(qi3, wi3, ki3, pb3):
    b, s, _ = pb3.shape
    topk = min(TOPK_MAX, s // 4)
    rows = B_HEADS * Q_BLOCK
    qblk = lambda width, col: pl.BlockSpec((None, Q_BLOCK, width), lambda bi, i: (bi, i, col))
    seq = lambda width, col: pl.BlockSpec((None, s, width), lambda bi, i: (bi, 0, col))
    return pl.pallas_call(
        functools.partial(_sparse_kernel, topk=topk),
        grid=(b, s // Q_BLOCK),
        in_specs=[qblk(IDX_Q, 0), qblk(LANES, 0), seq(LANES, 0), qblk(B_OUT, 0), seq(B_OUT, 1), seq(B_OUT, 2)],
        out_specs=pl.BlockSpec((None, Q_BLOCK, B_OUT), lambda bi, i: (bi, i, 0)),
        out_shape=jax.ShapeDtypeStruct((b, s, B_OUT), BF16),
        scratch_shapes=[pltpu.VMEM((Q_BLOCK, s), jnp.int32),
                        pltpu.VMEM((rows, LANES), F32), pltpu.VMEM((rows, LANES), F32),
                        pltpu.VMEM((rows, B_OUT), F32)],
        compiler_params=pltpu.CompilerParams(dimension_semantics=("arbitrary", "arbitrary"),
                                             vmem_limit_bytes=VMEM_LIMIT),
        name="sparse_attn",
    )(qi3, wi3, ki3, pb3, pb3, pb3)


DIFF_TQ = 256
DIFF_TK = 256


def _diff_kernel(lam_ref, g_ref, q_ref, k_ref, v_ref, o_ref, m_ref, l_ref, acc_ref, *, out_scale):
    i = pl.program_id(2)
    qstack = _stack_heads(q_ref[...])
    _flash_init(m_ref, l_ref, acc_ref)

    def block(kb, bias):
        start = pl.multiple_of(kb * DIFF_TK, DIFF_TK)
        s = _nt_dot(qstack, k_ref[pl.ds(start, DIFF_TK), :])
        if bias is not None:
            s = s + bias
        _flash_step(s, v_ref[pl.ds(start, DIFF_TK), :], m_ref, l_ref, acc_ref)

    def body(kb, carry):
        block(kb, None)
        return carry

    lax.fori_loop(0, i, body, 0)
    rows = lax.broadcasted_iota(jnp.int32, (2 * DIFF_TQ, DIFF_TK), 0) & (DIFF_TQ - 1)
    cols = lax.broadcasted_iota(jnp.int32, (2 * DIFF_TQ, DIFF_TK), 1)
    block(i, jnp.where(rows >= cols, 0.0, NEG))

    res = _flash_result(l_ref, acc_ref)
    o = res[:DIFF_TQ] - lam_ref[0] * res[DIFF_TQ:]
    o_ref[...] = (_rms(o, g_ref[...]) * out_scale).astype(o_ref.dtype)


def _diff_attention(pc3, lam, subln_g, lam_init):
    b, s, _ = pc3.shape
    nq = C_QK // LANES
    return pl.pallas_call(
        functools.partial(_diff_kernel, out_scale=1.0 - lam_init),
        grid=(b, C_HEADS, s // DIFF_TQ),
        in_specs=[pl.BlockSpec(memory_space=pltpu.SMEM),
                  pl.BlockSpec((1, C_VDIM), lambda bi, h, i: (0, 0)),
                  pl.BlockSpec((None, DIFF_TQ, LANES), lambda bi, h, i: (bi, i, h)),
                  pl.BlockSpec((None, s, LANES), lambda bi, h, i: (bi, 0, nq + h)),
                  pl.BlockSpec((None, s, C_VDIM), lambda bi, h, i: (bi, 0, 2 * nq + h))],
        out_specs=pl.BlockSpec((None, DIFF_TQ, C_VDIM), lambda bi, h, i: (bi, i, h)),
        out_shape=jax.ShapeDtypeStruct((b, s, C_OUT), BF16),
        scratch_shapes=[pltpu.VMEM((2 * DIFF_TQ, LANES), F32), pltpu.VMEM((2 * DIFF_TQ, LANES), F32),
                        pltpu.VMEM((2 * DIFF_TQ, C_VDIM), F32)],
        compiler_params=pltpu.CompilerParams(dimension_semantics=("arbitrary", "arbitrary", "arbitrary"),
                                             vmem_limit_bytes=VMEM_LIMIT),
        name="diff_attn",
    )(lam, subln_g, pc3, pc3, pc3)


def _merge_kernel(x_ref, gpre_ref, gpost_ref, oa_ref, ob_ref, oc_ref, wg_ref, wa_ref, wb_ref, wc_ref, wo_ref,
                  out_ref, y_ref, h_ref):
    x = x_ref[...]
    xn = _rms(x, gpre_ref[...]).astype(BF16)
    branches = ((oa_ref, wa_ref), (ob_ref, wb_ref), (oc_ref, wc_ref))
    cw = 2 * LANES
    for c in range(0, D_MODEL, cw):
        y = jnp.zeros((x.shape[0], cw), F32)
        for j, (o_ref, w_ref) in enumerate(branches):
            logit = jnp.dot(xn, wg_ref[:, j * D_MODEL + c:j * D_MODEL + c + cw], preferred_element_type=F32)
            gate = 1.0 / (1.0 + jnp.exp(-logit))
            y = y + gate * jnp.dot(o_ref[...], w_ref[:, c:c + cw], preferred_element_type=F32)
        y_ref[:, c:c + cw] = y.astype(BF16)
    ssq = jnp.zeros((x.shape[0], 1), F32)
    for c in range(0, D_MODEL, cw):
        h = jnp.dot(y_ref[...], wo_ref[:, c:c + cw], preferred_element_type=F32)
        h_ref[:, c:c + cw] = h
        ssq = ssq + jnp.sum(h * h, axis=1, keepdims=True)
    inv = lax.rsqrt(ssq * (1.0 / D_MODEL) + NORM_EPS)
    out_ref[...] = x + h_ref[...] * inv * gpost_ref[...]


def _merge(x2, gpre, gpost, oa, ob, oc, w, tm=512):
    n = x2.shape[0]
    row = lambda width: pl.BlockSpec((tm, width), lambda i: (i, 0))
    full = lambda a: pl.BlockSpec(a.shape, lambda i: (0, 0))
    return pl.pallas_call(
        _merge_kernel,
        grid=(n // tm,),
        in_specs=[row(D_MODEL), full(gpre), full(gpost), row(A_OUT), row(B_OUT), row(C_OUT)] + [full(a) for a in w],
        out_specs=row(D_MODEL),
        out_shape=jax.ShapeDtypeStruct((n, D_MODEL), F32),
        scratch_shapes=[pltpu.VMEM((tm, D_MODEL), BF16), pltpu.VMEM((tm, D_MODEL), F32)],
        compiler_params=pltpu.CompilerParams(dimension_semantics=("arbitrary",), vmem_limit_bytes=VMEM_LIMIT),
        name="merge",
    )(x2, gpre, gpost, oa, ob, oc, *w)


FFN_CHUNK = 256
CARRY_ROWS = 8


def _ffn_kernel(x_ref, gpre_ref, gpost_ref, wup_ref, cw_ref, cb_ref, wdn_ref, out_ref, carry_ref, acc_ref,
                *, tiles_per_seq):
    x = x_ref[...]
    tm = x.shape[0]
    xn = _rms(x, gpre_ref[...]).astype(BF16)
    first = pl.program_id(0) % tiles_per_seq == 0
    row = lax.broadcasted_iota(jnp.int32, (tm, 1), 0)

    def conv(col):
        h = jnp.dot(xn, wup_ref[:, col:col + FFN_CHUNK], preferred_element_type=F32)
        prev = jnp.where(first, 0.0, carry_ref[:, col:col + FFN_CHUNK])
        carry_ref[:, col:col + FFN_CHUNK] = h[tm - CARRY_ROWS:, :]
        p1 = prev[CARRY_ROWS - 1:CARRY_ROWS, :]
        p2 = prev[CARRY_ROWS - 2:CARRY_ROWS - 1, :]
        h1 = jnp.where(row == 0, p1, pltpu.roll(h, 1, 0))
        h2 = jnp.where(row == 0, p2, jnp.where(row == 1, p1, pltpu.roll(h, 2, 0)))
        w = cw_ref[:, col:col + FFN_CHUNK]
        return h2 * w[0:1, :] + h1 * w[1:2, :] + h * w[2:3, :] + cb_ref[:, col:col + FFN_CHUNK]

    for c in range(0, D_FF, FFN_CHUNK):
        g = conv(c)
        u = conv(D_FF + c)
        gelu = 0.5 * g * (1.0 + jnp.tanh(math.sqrt(2.0 / math.pi) * (g + 0.044715 * (g * g * g))))
        part = jnp.dot((gelu * u).astype(BF16), wdn_ref[c:c + FFN_CHUNK, :], preferred_element_type=F32)
        if c == 0:
            acc_ref[...] = part
        else:
            acc_ref[...] += part
    out_ref[...] = x + _rms(acc_ref[...], gpost_ref[...])


def _ffn(x2, gpre, gpost, wup, conv_w, conv_b, wdn, seq_len, tm=512):
    n = x2.shape[0]
    row = lambda width: pl.BlockSpec((tm, width), lambda i: (i, 0))
    full = lambda a: pl.BlockSpec(a.shape, lambda i: (0, 0))
    return pl.pallas_call(
        functools.partial(_ffn_kernel, tiles_per_seq=seq_len // tm),
        grid=(n // tm,),
        in_specs=[row(D_MODEL), full(gpre), full(gpost), full(wup), full(conv_w), full(conv_b), full(wdn)],
        out_specs=row(D_MODEL),
        out_shape=jax.ShapeDtypeStruct((n, D_MODEL), F32),
        scratch_shapes=[pltpu.VMEM((CARRY_ROWS, 2 * D_FF), F32), pltpu.VMEM((tm, D_MODEL), F32)],
        compiler_params=pltpu.CompilerParams(dimension_semantics=("arbitrary",), vmem_limit_bytes=VMEM_LIMIT),
        name="conv_ffn",
    )(x2, gpre, gpost, wup, conv_w, conv_b, wdn)


def _rope_lane_tables(positions):
    half = ROPE_DIM // 2
    inv = ROPE_THETA ** (-jnp.arange(0, ROPE_DIM, 2, dtype=F32) / ROPE_DIM)
    ang = positions.astype(F32).reshape(-1, 1) * inv
    cos, sin = jnp.cos(ang), jnp.sin(ang)
    n = ang.shape[0]
    rest = HEAD_DIM - ROPE_DIM
    ones, zeros, zh = jnp.ones((n, rest), F32), jnp.zeros((n, rest), F32), jnp.zeros((n, half), F32)
    ct = jnp.concatenate([cos, cos, ones], axis=1)
    s1 = jnp.concatenate([-sin, zh, zeros], axis=1)
    s2 = jnp.concatenate([zh, sin, zeros], axis=1)
    return tuple(jnp.tile(t, (1, LANES // HEAD_DIM)) for t in (ct, s1, s2))


def _split_w_in(w_in):
    o_b, o_i, o_c, o_g = A_IN, A_IN + B_IN, A_IN + B_IN + IDX_IN, A_IN + B_IN + IDX_IN + C_IN
    wk = w_in[:, o_i + IDX_Q:o_i + IDX_Q + IDX_DIM]
    ww = w_in[:, o_i + IDX_Q + IDX_DIM:o_c]
    proj = (w_in[:, :o_b], w_in[:, o_b:o_i], w_in[:, o_i:o_i + IDX_Q],
            jnp.concatenate([wk, wk], axis=1),
            jnp.pad(ww, ((0, 0), (0, LANES - IDX_HEADS))),
            w_in[:, o_c:o_g])
    return tuple(a.astype(BF16) for a in proj), w_in[:, o_g:].astype(BF16)


def kernel(x, positions, w_in, w_br_a, w_br_b, w_br_c, w_out, lam_q1, lam_k1, lam_q2, lam_k2, subln_g,
           norm_mix_pre, norm_mix_post, norm_ffn_pre, norm_ffn_post, w_ffn_up, conv_w, conv_b, w_ffn_down):
    b, s, d = x.shape
    depth = w_in.shape[0]
    assert d == D_MODEL and s % A_TILE == 0 and s % SEL_CHUNK == 0 and s % DIFF_TQ == 0
    tables = _rope_lane_tables(positions)
    x2 = x.reshape(b * s, d)
    vec = lambda a: a.reshape(1, -1)
    for layer in range(depth):
        lam_init = 0.8 - 0.6 * math.exp(-0.3 * layer)
        lam = (jnp.exp(jnp.sum(lam_q1[layer] * lam_k1[layer])) - jnp.exp(jnp.sum(lam_q2[layer] * lam_k2[layer]))
               + lam_init).reshape(1).astype(F32)
        w_proj, w_gate = _split_w_in(w_in[layer])
        pa, pb, qi, ki, wi, pc = _inproj(x2, vec(norm_mix_pre[layer]), tables, w_proj)
        seq = lambda a: a.reshape(b, s, a.shape[-1])
        oa = _dilated_attention(seq(pa))
        ob = _sparse_attention(seq(qi), seq(wi), seq(ki), seq(pb))
        oc = _diff_attention(seq(pc), lam, vec(subln_g[layer]), lam_init)
        flat = lambda a: a.reshape(b * s, a.shape[-1])
        w_merge = (w_gate, w_br_a[layer].astype(BF16), w_br_b[layer].astype(BF16), w_br_c[layer].astype(BF16),
                   w_out[layer].astype(BF16))
        x2 = _merge(x2, vec(norm_mix_pre[layer]), vec(norm_mix_post[layer]), flat(oa), flat(ob), flat(oc), w_merge)
        x2 = _ffn(x2, vec(norm_ffn_pre[layer]), vec(norm_ffn_post[layer]), w_ffn_up[layer].astype(BF16),
                  conv_w[layer], vec(conv_b[layer]), w_ffn_down[layer].astype(BF16), s)
    return x2.reshape(b, s, d)
```

```python
import functools
import math

import jax
import jax.numpy as jnp
from jax import lax
from jax.experimental import pallas as pl
from jax.experimental.pallas import tpu as pltpu

D_MODEL = 1024
HEAD_DIM = 64
ROPE_THETA = 500000.0
ROPE_DIM = HEAD_DIM // 4
NORM_EPS = 1e-6
Q_BLOCK = 128

A_PAIRS = ((128, 1), (512, 4), (2048, 16))
A_GROUPS = len(A_PAIRS)
A_HEADS = 4
A_OUT = A_HEADS * HEAD_DIM
B_HEADS = 4
B_OUT = B_HEADS * HEAD_DIM
IDX_HEADS = 8
IDX_DIM = 64
TOPK_MAX = 256
C_HEADS = 4
C_VDIM = 2 * HEAD_DIM
C_OUT = C_HEADS * C_VDIM
N_BRANCH = 3

A_IN = A_GROUPS * 3 * A_HEADS * HEAD_DIM
B_IN = 3 * B_HEADS * HEAD_DIM
IDX_Q = IDX_HEADS * IDX_DIM
IDX_IN = IDX_Q + IDX_DIM + IDX_HEADS
C_QK = C_HEADS * 2 * HEAD_DIM
C_IN = 2 * C_QK + C_OUT
GATE_IN = N_BRANCH * D_MODEL
D_FF = ((8 * D_MODEL // 3 + 127) // 128) * 128
CONV_WIDTH = 3

LANES = 128
SCALE = HEAD_DIM ** -0.5
NEG = -1e30
INT_MIN = -2 ** 31
VMEM_LIMIT = 56 * 1024 * 1024

F32 = jnp.float32
BF16 = jnp.bfloat16
NT_DIMS = (((1,), (1,)), ((), ()))


def _nt_dot(a, b):
    return lax.dot_general(a, b, NT_DIMS, preferred_element_type=F32)


def _rms(x, g):
    return x * lax.rsqrt(jnp.mean(x * x, axis=-1, keepdims=True) + NORM_EPS) * g


def _inproj_kernel(x_ref, g_ref, ct_ref, s1_ref, s2_ref, wa_ref, wb_ref, wqi_ref, wki_ref, wwi_ref, wc_ref,
                   pa_ref, pb_ref, qi_ref, ki_ref, wi_ref, pc_ref):
    xn = _rms(x_ref[...], g_ref[...]).astype(BF16)
    ct, s1, s2 = ct_ref[...], s1_ref[...], s2_ref[...]

    def rope(y):
        return y * ct + pltpu.roll(y, LANES - ROPE_DIM // 2, 1) * s1 + pltpu.roll(y, ROPE_DIM // 2, 1) * s2

    def project(w_ref, out_ref, n_cols, mode_of_col):
        for c in range(0, n_cols, 2 * LANES):
            y = jnp.dot(xn, w_ref[:, c:c + 2 * LANES], preferred_element_type=F32)
            for h in range(2):
                col = c + h * LANES
                z = y[:, h * LANES:(h + 1) * LANES]
                mode = mode_of_col(col)
                if mode != "v":
                    z = rope(z)
                if mode == "q":
                    z = z * SCALE
                out_ref[:, col:col + LANES] = z.astype(out_ref.dtype)

    qkv_mode = lambda width: (lambda col: ("q", "k", "v")[(col // width) % 3])
    project(wa_ref, pa_ref, A_IN, qkv_mode(A_OUT))
    project(wb_ref, pb_ref, B_IN, qkv_mode(B_OUT))
    project(wqi_ref, qi_ref, IDX_Q, lambda col: "k")
    project(wc_ref, pc_ref, C_IN, qkv_mode(C_QK))
    ki = jnp.dot(xn, wki_ref[...], preferred_element_type=F32)
    ki_ref[...] = rope(ki).astype(BF16)
    wi = jnp.dot(xn, wwi_ref[...], preferred_element_type=F32)
    wi_ref[...] = (wi * (IDX_HEADS ** -0.5)) * (IDX_DIM ** -0.5)


def _inproj(x2, gain, tables, w, tm=512):
    n = x2.shape[0]
    row = lambda width: pl.BlockSpec((tm, width), lambda i: (i, 0))
    full = lambda a: pl.BlockSpec(a.shape, lambda i: (0, 0))
    outs = [(A_IN, BF16), (B_IN, BF16), (IDX_Q, BF16), (LANES, BF16), (LANES, F32), (C_IN, BF16)]
    return pl.pallas_call(
        _inproj_kernel,
        grid=(n // tm,),
        in_specs=[row(D_MODEL), full(gain), row(LANES), row(LANES), row(LANES)] + [full(a) for a in w],
        out_specs=[row(wd) for wd, _ in outs],
        out_shape=[jax.ShapeDtypeStruct((n, wd), dt) for wd, dt in outs],
        compiler_params=pltpu.CompilerParams(dimension_semantics=("arbitrary",), vmem_limit_bytes=VMEM_LIMIT),
        name="inproj",
    )(x2, gain, *tables, *w)


def _head_masks(width):
    lane = lax.broadcasted_iota(jnp.int32, (1, width), 1)
    return [lane // HEAD_DIM == h for h in range(width // HEAD_DIM)]


def _stack_heads(q):
    zero = jnp.zeros_like(q)
    return jnp.concatenate([jnp.where(hm, q, zero) for hm in _head_masks(q.shape[1])], axis=0)


def _unstack_heads(x, rows):
    hmask = _head_masks(HEAD_DIM * (x.shape[0] // rows))
    out = jnp.where(hmask[0], x[0:rows], 0.0)
    for h in range(1, len(hmask)):
        out = jnp.where(hmask[h], x[h * rows:(h + 1) * rows], out)
    return out


def _lane_tile(x, width):
    reps = width // LANES
    return x if reps == 1 else jnp.concatenate([x] * reps, axis=1)


def _flash_init(m_ref, l_ref, acc_ref):
    m_ref[...] = jnp.full(m_ref.shape, NEG, F32)
    l_ref[...] = jnp.zeros(l_ref.shape, F32)
    acc_ref[...] = jnp.zeros(acc_ref.shape, F32)


def _flash_step(s, v, m_ref, l_ref, acc_ref):
    tk = s.shape[1]
    m_old = m_ref[...]
    m_new = jnp.maximum(m_old, jnp.max(s, axis=1, keepdims=True))
    p = jnp.exp(s - _lane_tile(m_new, tk))
    alpha = jnp.exp(m_old - m_new)
    psum = p[:, :LANES]
    for j in range(1, tk // LANES):
        psum = psum + p[:, j * LANES:(j + 1) * LANES]
    l_ref[...] = alpha * l_ref[...] + psum
    acc_ref[...] = (_lane_tile(alpha, acc_ref.shape[1]) * acc_ref[...]
                    + jnp.dot(p.astype(BF16), v, preferred_element_type=F32))
    m_ref[...] = m_new


def _flash_pipeline(n, logits_fn, values_fn, m_ref, l_ref, acc_ref):
    def body(t, s_cur):
        s_next = logits_fn(t + 1)
        _flash_step(s_cur, values_fn(t), m_ref, l_ref, acc_ref)
        return s_next
    s_last = lax.fori_loop(0, n - 1, body, logits_fn(0))
    _flash_step(s_last, values_fn(n - 1), m_ref, l_ref, acc_ref)


def _flash_result(l_ref, acc_ref):
    return acc_ref[...] / jnp.sum(l_ref[...], axis=1, keepdims=True)


A_TILE = max(w for w, _ in A_PAIRS)


def _dilated_kernel(*refs):
    group_refs = [refs[5 * g:5 * g + 5] for g in range(A_GROUPS)]
    o_ref, qs_ref, ks_ref, vs_ref, m_sc, l_sc, n_sc = refs[5 * A_GROUPS:]
    j = pl.program_id(1)
    qi = lax.broadcasted_iota(jnp.int32, (Q_BLOCK, 2 * Q_BLOCK), 0)
    kj = lax.broadcasted_iota(jnp.int32, (Q_BLOCK, 2 * Q_BLOCK), 1)
    rel = kj - qi
    band = (rel >= 0) & (rel <= Q_BLOCK)
    bias_band = jnp.where(band, 0.0, NEG)
    bias_first = jnp.where(band & (kj >= Q_BLOCK), 0.0, NEG)
    halves = A_OUT // LANES

    def get(ref, rows):
        return jnp.concatenate([ref[hh, rows, :] for hh in range(halves)], axis=1)

    def put(ref, rows, val):
        for hh in range(halves):
            ref[hh, rows, :] = val[:, hh * LANES:(hh + 1) * LANES]

    for g, ((window, dil), (q_ref, kp_ref, kc_ref, vp_ref, vc_ref)) in enumerate(zip(A_PAIRS, group_refs)):
        assert window == Q_BLOCK * dil
        put(qs_ref, slice(0, A_TILE), q_ref[...].astype(F32))
        put(ks_ref, slice(0, window), kp_ref[...].astype(F32))
        put(ks_ref, slice(window, window + A_TILE), kc_ref[...].astype(F32))
        put(vs_ref, slice(0, window), vp_ref[...].astype(F32))
        put(vs_ref, slice(window, window + A_TILE), vc_ref[...].astype(F32))
        shift = dil.bit_length() - 1

        def problem(p, carry, g=g, window=window, dil=dil, shift=shift):
            sub = p >> shift
            base = sub * window + (p & (dil - 1))
            stride = dil if dil > 1 else None
            q_rows = pl.ds(base, Q_BLOCK, stride=stride)
            k_rows = pl.ds(base, 2 * Q_BLOCK, stride=stride)
            qstack = _stack_heads(get(qs_ref, q_rows).astype(BF16))
            k = get(ks_ref, k_rows).astype(BF16)
            v = get(vs_ref, k_rows).astype(BF16)
            bias = jnp.where((sub > 0) | (j > 0), bias_band, bias_first)
            s = _nt_dot(qstack, k) + jnp.concatenate([bias] * A_HEADS, axis=0)
            m = jnp.max(s, axis=1, keepdims=True)
            e = jnp.exp(s - m)
            l = jnp.sum(e, axis=1, keepdims=True)
            num = jnp.dot(e.astype(BF16), v, preferred_element_type=F32)
            m_n, l_n, num_n = (_unstack_heads(t, Q_BLOCK) for t in (m, l, num))
            if g == 0:
                put(m_sc, q_rows, m_n)
                put(l_sc, q_rows, l_n)
                put(n_sc, q_rows, num_n)
            else:
                m_o = get(m_sc, q_rows)
                m_x = jnp.maximum(m_o, m_n)
                a, b = jnp.exp(m_o - m_x), jnp.exp(m_n - m_x)
                put(m_sc, q_rows, m_x)
                put(l_sc, q_rows, a * get(l_sc, q_rows) + b * l_n)
                put(n_sc, q_rows, a * get(n_sc, q_rows) + b * num_n)
            return carry

        lax.fori_loop(0, A_TILE // Q_BLOCK, problem, 0)

    all_rows = slice(0, A_TILE)
    o_ref[...] = (get(n_sc, all_rows) / get(l_sc, all_rows)).astype(o_ref.dtype)


def _dilated_attention(pa3):
    b, s, _ = pa3.shape
    in_specs = []
    for g, (window, _) in enumerate(A_PAIRS):
        per = A_TILE // window
        cur = lambda bi, j, col: (bi, j, col)
        prev = lambda bi, j, col, per=per: (bi, jnp.maximum(j * per - 1, 0), col)
        in_specs.append(pl.BlockSpec((None, A_TILE, A_OUT), functools.partial(cur, col=3 * g)))
        for col in (3 * g + 1, 3 * g + 2):
            in_specs.append(pl.BlockSpec((None, window, A_OUT), functools.partial(prev, col=col)))
            in_specs.append(pl.BlockSpec((None, A_TILE, A_OUT), functools.partial(cur, col=col)))
    stage = lambda rows: pltpu.VMEM((A_OUT // LANES, rows, LANES), F32)
    return pl.pallas_call(
        _dilated_kernel,
        grid=(b, s // A_TILE),
        in_specs=in_specs,
        out_specs=pl.BlockSpec((None, A_TILE, A_OUT), lambda bi, j: (bi, j, 0)),
        out_shape=jax.ShapeDtypeStruct((b, s, A_OUT), BF16),
        scratch_shapes=[stage(A_TILE), stage(2 * A_TILE), stage(2 * A_TILE),
                        stage(A_TILE), stage(A_TILE), stage(A_TILE)],
        compiler_params=pltpu.CompilerParams(dimension_semantics=("arbitrary", "arbitrary"),
                                             vmem_limit_bytes=VMEM_LIMIT),
        name="dilated_attn",
    )(*([pa3] * (5 * A_GROUPS)))


SEL_CHUNK = 512
ATT_CHUNK = 512


def _sparse_kernel(qi_ref, wi_ref, ki_ref, q_ref, k_ref, v_ref, o_ref, key_ref, m_ref, l_ref, acc_ref, *, topk):
    i = pl.program_id(1)
    qs = i * Q_BLOCK
    nch = qs // SEL_CHUNK + 1
    nat = qs // ATT_CHUNK + 1
    lane = lax.broadcasted_iota(jnp.int32, (1, LANES), 1)
    qpos = qs + lax.broadcasted_iota(jnp.int32, (Q_BLOCK, 1), 0)

    qi = qi_ref[...]
    stacked = []
    for h in range(IDX_HEADS):
        pair = qi[:, (h // 2) * LANES:(h // 2 + 1) * LANES]
        keep = (lane < IDX_DIM) if h % 2 == 0 else (lane >= IDX_DIM)
        stacked.append(jnp.where(keep, pair, jnp.zeros_like(pair)))
    stacked = jnp.concatenate(stacked, axis=0)
    w = wi_ref[...]
    wcol = [w[:, h:h + 1] for h in range(IDX_HEADS)]

    def to_key(score):
        bits = lax.bitcast_convert_type(score, jnp.int32)
        return bits ^ ((bits >> 31) & 0x7FFFFFFF)

    def score_chunk(c, group_max):
        start = pl.multiple_of(c * SEL_CHUNK, SEL_CHUNK)
        kc = ki_ref[pl.ds(start, SEL_CHUNK), :]
        logits = _nt_dot(stacked, kc)
        score = jnp.zeros((Q_BLOCK, SEL_CHUNK), F32)
        for h in range(IDX_HEADS):
            score = score + jnp.maximum(logits[h * Q_BLOCK:(h + 1) * Q_BLOCK], 0.0) * wcol[h]
        kpos = start + lax.broadcasted_iota(jnp.int32, (1, SEL_CHUNK), 1)
        causal = kpos <= qpos
        key_ref[:, pl.ds(start, SEL_CHUNK)] = jnp.where(causal, to_key(score), INT_MIN)
        score = jnp.where(causal, score, -jnp.inf)
        group_max = list(group_max)
        for jj in range(SEL_CHUNK // LANES):
            group_max[jj % 2] = jnp.maximum(group_max[jj % 2], score[:, jj * LANES:(jj + 1) * LANES])
        return tuple(group_max)

    neg_inf = jnp.full((Q_BLOCK, LANES), -jnp.inf, F32)
    gm_even, gm_odd = lax.fori_loop(0, nch, score_chunk, (neg_inf, neg_inf))

    def count(pred):
        def body(c, acc):
            start = pl.multiple_of(c * SEL_CHUNK, SEL_CHUNK)
            hit = jnp.where(pred(key_ref[:, pl.ds(start, SEL_CHUNK)]), 1.0, 0.0)
            for jj in range(SEL_CHUNK // LANES):
                acc = acc + hit[:, jj * LANES:(jj + 1) * LANES]
            return acc
        acc = lax.fori_loop(0, nch, body, jnp.zeros((Q_BLOCK, LANES), F32))
        return jnp.sum(acc, axis=1, keepdims=True)

    kf = float(topk)
    assert topk <= 2 * LANES
    searchable = qs >= topk
    lo_f = jnp.minimum(jnp.min(gm_even, axis=1, keepdims=True), jnp.min(gm_odd, axis=1, keepdims=True))
    hi_f = jnp.maximum(jnp.max(gm_even, axis=1, keepdims=True), jnp.max(gm_odd, axis=1, keepdims=True))
    lo0 = jnp.where(searchable, to_key(lo_f), INT_MIN + 1)
    hi0 = to_key(hi_f) + 1
    cnt_lo0 = jnp.where(searchable, kf + 1.0, kf) + jnp.zeros((Q_BLOCK, 1), F32)

    def midpoint(lo, hi):
        return (lo >> 1) + (hi >> 1) + (lo & hi & 1)

    def unresolved(lo, hi, cnt_lo):
        return (cnt_lo > kf) & (midpoint(lo, hi) > lo)

    def any_row(flag):
        return jnp.max(jnp.where(flag, 1.0, 0.0)) > 0.0

    def bisect(carry):
        lo, hi, cnt_lo, _ = carry
        active = unresolved(lo, hi, cnt_lo)
        mid = midpoint(lo, hi)
        cnt = count(lambda blk: blk >= mid)
        up = active & (cnt >= kf)
        down = active & (cnt < kf)
        lo, hi, cnt_lo = jnp.where(up, mid, lo), jnp.where(down, mid, hi), jnp.where(up, cnt, cnt_lo)
        return lo, hi, cnt_lo, any_row(unresolved(lo, hi, cnt_lo))

    thr, _, cnt_thr, _ = lax.while_loop(lambda carry: carry[3], bisect,
                                        (lo0, hi0, cnt_lo0, any_row(unresolved(lo0, hi0, cnt_lo0))))
    has_ties = jnp.max(cnt_thr) > kf

    qstack = _stack_heads(q_ref[...])

    def logits(c, sel):
        start = pl.multiple_of(c * ATT_CHUNK, ATT_CHUNK)
        bias = jnp.where(sel, 0.0, NEG)
        return _nt_dot(qstack, k_ref[pl.ds(start, ATT_CHUNK), :]) + jnp.concatenate([bias] * B_HEADS, axis=0)

    def values(c):
        return v_ref[pl.ds(pl.multiple_of(c * ATT_CHUNK, ATT_CHUNK), ATT_CHUNK), :]

    def attend(c, sel):
        _flash_step(logits(c, sel), values(c), m_ref, l_ref, acc_ref)

    _flash_init(m_ref, l_ref, acc_ref)

    @pl.when(jnp.logical_not(has_ties))
    def _():
        def selected(c):
            return key_ref[:, pl.ds(pl.multiple_of(c * ATT_CHUNK, ATT_CHUNK), ATT_CHUNK)] >= thr
        _flash_pipeline(nat, lambda c: logits(c, selected(c)), values, m_ref, l_ref, acc_ref)

    @pl.when(has_ties)
    def _():
        need = kf - count(lambda blk: blk > thr)
        r = lax.broadcasted_iota(jnp.int32, (ATT_CHUNK, ATT_CHUNK), 0)
        cc = lax.broadcasted_iota(jnp.int32, (ATT_CHUNK, ATT_CHUNK), 1)
        before = jnp.where(r < cc, 1.0, 0.0).astype(BF16)

        def body(c, seen):
            start = pl.multiple_of(c * ATT_CHUNK, ATT_CHUNK)
            blk = key_ref[:, pl.ds(start, ATT_CHUNK)]
            eq = blk == thr
            eqf = jnp.where(eq, 1.0, 0.0)
            rank = seen + jnp.dot(eqf.astype(BF16), before, preferred_element_type=F32)
            attend(c, (blk > thr) | (eq & (rank < need)))
            return seen + jnp.sum(eqf, axis=1, keepdims=True)
        lax.fori_loop(0, nat, body, jnp.zeros((Q_BLOCK, 1), F32))

    o_ref[...] = _unstack_heads(_flash_result(l_ref, acc_ref), Q_BLOCK).astype(o_ref.dtype)


def _sparse_attention(qi3, wi3, ki3, pb3):
    b, s, _ = pb3.shape
    topk = min(TOPK_MAX, s // 4)
    rows = B_HEADS * Q_BLOCK
    qblk = lambda width, col: pl.BlockSpec((None, Q_BLOCK, width), lambda bi, i: (bi, i, col))
    seq = lambda width, col: pl.BlockSpec((None, s, width), lambda bi, i: (bi, 0, col))
    return pl.pallas_call(
        functools.partial(_sparse_kernel, topk=topk),
        grid=(b, s // Q_BLOCK),
        in_specs=[qblk(IDX_Q, 0), qblk(LANES, 0), seq(LANES, 0), qblk(B_OUT, 0), seq(B_OUT, 1), seq(B_OUT, 2)],
        out_specs=pl.BlockSpec((None, Q_BLOCK, B_OUT), lambda bi, i: (bi, i, 0)),
        out_shape=jax.ShapeDtypeStruct((b, s, B_OUT), BF16),
        scratch_shapes=[pltpu.VMEM((Q_BLOCK, s), jnp.int32),
                        pltpu.VMEM((rows, LANES), F32), pltpu.VMEM((rows, LANES), F32),
                        pltpu.VMEM((rows, B_OUT), F32)],
        compiler_params=pltpu.CompilerParams(dimension_semantics=("arbitrary", "arbitrary"),
                                             vmem_limit_bytes=VMEM_LIMIT),
        name="sparse_attn",
    )(qi3, wi3, ki3, pb3, pb3, pb3)


DIFF_TQ = 256
DIFF_TK = 1024


def _diff_kernel(lam_ref, g_ref, q_ref, k_ref, v_ref, o_ref, m_ref, l_ref, acc_ref, *, out_scale):
    i = pl.program_id(2)
    qstack = _stack_heads(q_ref[...])
    _flash_init(m_ref, l_ref, acc_ref)

    def logits(kb):
        return _nt_dot(qstack, k_ref[pl.ds(pl.multiple_of(kb * DIFF_TK, DIFF_TK), DIFF_TK), :])

    def values(kb):
        return v_ref[pl.ds(pl.multiple_of(kb * DIFF_TK, DIFF_TK), DIFF_TK), :]

    n_full = (i * DIFF_TQ) // DIFF_TK

    def full_block(kb, carry):
        _flash_step(logits(kb), values(kb), m_ref, l_ref, acc_ref)
        return carry

    lax.fori_loop(0, n_full, full_block, 0)
    qpos = i * DIFF_TQ + (lax.broadcasted_iota(jnp.int32, (2 * DIFF_TQ, DIFF_TK), 0) & (DIFF_TQ - 1))
    kpos = n_full * DIFF_TK + lax.broadcasted_iota(jnp.int32, (2 * DIFF_TQ, DIFF_TK), 1)
    _flash_step(logits(n_full) + jnp.where(kpos <= qpos, 0.0, NEG), values(n_full), m_ref, l_ref, acc_ref)

    res = _flash_result(l_ref, acc_ref)
    o = res[:DIFF_TQ] - lam_ref[0] * res[DIFF_TQ:]
    o_ref[...] = (_rms(o, g_ref[...]) * out_scale).astype(o_ref.dtype)


def _diff_attention(pc3, lam, subln_g, lam_init):
    b, s, _ = pc3.shape
    nq = C_QK // LANES
    return pl.pallas_call(
        functools.partial(_diff_kernel, out_scale=1.0 - lam_init),
        grid=(b, C_HEADS, s // DIFF_TQ),
        in_specs=[pl.BlockSpec(memory_space=pltpu.SMEM),
                  pl.BlockSpec((1, C_VDIM), lambda bi, h, i: (0, 0)),
                  pl.BlockSpec((None, DIFF_TQ, LANES), lambda bi, h, i: (bi, i, h)),
                  pl.BlockSpec((None, s, LANES), lambda bi, h, i: (bi, 0, nq + h)),
                  pl.BlockSpec((None, s, C_VDIM), lambda bi, h, i: (bi, 0, 2 * nq + h))],
        out_specs=pl.BlockSpec((None, DIFF_TQ, C_VDIM), lambda bi, h, i: (bi, i, h)),
        out_shape=jax.ShapeDtypeStruct((b, s, C_OUT), BF16),
        scratch_shapes=[pltpu.VMEM((2 * DIFF_TQ, LANES), F32), pltpu.VMEM((2 * DIFF_TQ, LANES), F32),
                        pltpu.VMEM((2 * DIFF_TQ, C_VDIM), F32)],
        compiler_params=pltpu.CompilerParams(dimension_semantics=("arbitrary", "arbitrary", "arbitrary"),
                                             vmem_limit_bytes=VMEM_LIMIT),
        name="diff_attn",
    )(lam, subln_g, pc3, pc3, pc3)


def _merge_kernel(x_ref, gpre_ref, gpost_ref, oa_ref, ob_ref, oc_ref, wg_ref, wa_ref, wb_ref, wc_ref, wo_ref,
                  out_ref, y_ref, h_ref):
    x = x_ref[...]
    xn = _rms(x, gpre_ref[...]).astype(BF16)
    branches = ((oa_ref, wa_ref), (ob_ref, wb_ref), (oc_ref, wc_ref))
    cw = 2 * LANES
    for c in range(0, D_MODEL, cw):
        y = jnp.zeros((x.shape[0], cw), F32)
        for j, (o_ref, w_ref) in enumerate(branches):
            logit = jnp.dot(xn, wg_ref[:, j * D_MODEL + c:j * D_MODEL + c + cw], preferred_element_type=F32)
            gate = 1.0 / (1.0 + jnp.exp(-logit))
            y = y + gate * jnp.dot(o_ref[...], w_ref[:, c:c + cw], preferred_element_type=F32)
        y_ref[:, c:c + cw] = y.astype(BF16)
    ssq = jnp.zeros((x.shape[0], 1), F32)
    for c in range(0, D_MODEL, cw):
        h = jnp.dot(y_ref[...], wo_ref[:, c:c + cw], preferred_element_type=F32)
        h_ref[:, c:c + cw] = h
        ssq = ssq + jnp.sum(h * h, axis=1, keepdims=True)
    inv = lax.rsqrt(ssq * (1.0 / D_MODEL) + NORM_EPS)
    out_ref[...] = x + h_ref[...] * inv * gpost_ref[...]


def _merge(x2, gpre, gpost, oa, ob, oc, w, tm=512):
    n = x2.shape[0]
    row = lambda width: pl.BlockSpec((tm, width), lambda i: (i, 0))
    full = lambda a: pl.BlockSpec(a.shape, lambda i: (0, 0))
    return pl.pallas_call(
        _merge_kernel,
        grid=(n // tm,),
        in_specs=[row(D_MODEL), full(gpre), full(gpost), row(A_OUT), row(B_OUT), row(C_OUT)] + [full(a) for a in w],
        out_specs=row(D_MODEL),
        out_shape=jax.ShapeDtypeStruct((n, D_MODEL), F32),
        scratch_shapes=[pltpu.VMEM((tm, D_MODEL), BF16), pltpu.VMEM((tm, D_MODEL), F32)],
        compiler_params=pltpu.CompilerParams(dimension_semantics=("arbitrary",), vmem_limit_bytes=VMEM_LIMIT),
        name="merge",
    )(x2, gpre, gpost, oa, ob, oc, *w)


FFN_CHUNK = 256
CARRY_ROWS = 8


def _ffn_kernel(x_ref, gpre_ref, gpost_ref, wup_ref, cw_ref, cb_ref, wdn_ref, out_ref, carry_ref, acc_ref,
                *, tiles_per_seq):
    x = x_ref[...]
    tm = x.shape[0]
    xn = _rms(x, gpre_ref[...]).astype(BF16)
    first = pl.program_id(0) % tiles_per_seq == 0
    row = lax.broadcasted_iota(jnp.int32, (tm, 1), 0)

    def conv(col):
        h = jnp.dot(xn, wup_ref[:, col:col + FFN_CHUNK], preferred_element_type=F32)
        prev = jnp.where(first, 0.0, carry_ref[:, col:col + FFN_CHUNK])
        carry_ref[:, col:col + FFN_CHUNK] = h[tm - CARRY_ROWS:, :]
        p1 = prev[CARRY_ROWS - 1:CARRY_ROWS, :]
        p2 = prev[CARRY_ROWS - 2:CARRY_ROWS - 1, :]
        h1 = jnp.where(row == 0, p1, pltpu.roll(h, 1, 0))
        h2 = jnp.where(row == 0, p2, jnp.where(row == 1, p1, pltpu.roll(h, 2, 0)))
        w = cw_ref[:, col:col + FFN_CHUNK]
        return h2 * w[0:1, :] + h1 * w[1:2, :] + h * w[2:3, :] + cb_ref[:, col:col + FFN_CHUNK]

    for c in range(0, D_FF, FFN_CHUNK):
        g = conv(c)
        u = conv(D_FF + c)
        gelu = 0.5 * g * (1.0 + jnp.tanh(math.sqrt(2.0 / math.pi) * (g + 0.044715 * (g * g * g))))
        part = jnp.dot((gelu * u).astype(BF16), wdn_ref[c:c + FFN_CHUNK, :], preferred_element_type=F32)
        if c == 0:
            acc_ref[...] = part
        else:
            acc_ref[...] += part
    out_ref[...] = x + _rms(acc_ref[...], gpost_ref[...])


def _ffn(x2, gpre, gpost, wup, conv_w, conv_b, wdn, seq_len, tm=512):
    n = x2.shape[0]
    row = lambda width: pl.BlockSpec((tm, width), lambda i: (i, 0))
    full = lambda a: pl.BlockSpec(a.shape, lambda i: (0, 0))
    return pl.pallas_call(
        functools.partial(_ffn_kernel, tiles_per_seq=seq_len // tm),
        grid=(n // tm,),
        in_specs=[row(D_MODEL), full(gpre), full(gpost), full(wup), full(conv_w), full(conv_b), full(wdn)],
        out_specs=row(D_MODEL),
        out_shape=jax.ShapeDtypeStruct((n, D_MODEL), F32),
        scratch_shapes=[pltpu.VMEM((CARRY_ROWS, 2 * D_FF), F32), pltpu.VMEM((tm, D_MODEL), F32)],
        compiler_params=pltpu.CompilerParams(dimension_semantics=("arbitrary",), vmem_limit_bytes=VMEM_LIMIT),
        name="conv_ffn",
    )(x2, gpre, gpost, wup, conv_w, conv_b, wdn)


def _rope_lane_tables(positions):
    half = ROPE_DIM // 2
    inv = ROPE_THETA ** (-jnp.arange(0, ROPE_DIM, 2, dtype=F32) / ROPE_DIM)
    ang = positions.astype(F32).reshape(-1, 1) * inv
    cos, sin = jnp.cos(ang), jnp.sin(ang)
    n = ang.shape[0]
    rest = HEAD_DIM - ROPE_DIM
    ones, zeros, zh = jnp.ones((n, rest), F32), jnp.zeros((n, rest), F32), jnp.zeros((n, half), F32)
    ct = jnp.concatenate([cos, cos, ones], axis=1)
    s1 = jnp.concatenate([-sin, zh, zeros], axis=1)
    s2 = jnp.concatenate([zh, sin, zeros], axis=1)
    return tuple(jnp.tile(t, (1, LANES // HEAD_DIM)) for t in (ct, s1, s2))


def _split_w_in(w_in):
    o_b, o_i, o_c, o_g = A_IN, A_IN + B_IN, A_IN + B_IN + IDX_IN, A_IN + B_IN + IDX_IN + C_IN
    wk = w_in[:, o_i + IDX_Q:o_i + IDX_Q + IDX_DIM]
    ww = w_in[:, o_i + IDX_Q + IDX_DIM:o_c]
    proj = (w_in[:, :o_b], w_in[:, o_b:o_i], w_in[:, o_i:o_i + IDX_Q],
            jnp.concatenate([wk, wk], axis=1),
            jnp.pad(ww, ((0, 0), (0, LANES - IDX_HEADS))),
            w_in[:, o_c:o_g])
    return tuple(a.astype(BF16) for a in proj), w_in[:, o_g:].astype(BF16)


def kernel(x, positions, w_in, w_br_a, w_br_b, w_br_c, w_out, lam_q1, lam_k1, lam_q2, lam_k2, subln_g,
           norm_mix_pre, norm_mix_post, norm_ffn_pre, norm_ffn_post, w_ffn_up, conv_w, conv_b, w_ffn_down):
    b, s, d = x.shape
    depth = w_in.shape[0]
    assert d == D_MODEL and s % A_TILE == 0 and s % SEL_CHUNK == 0 and s % DIFF_TK == 0
    tables = _rope_lane_tables(positions)
    x2 = x.reshape(b * s, d)
    vec = lambda a: a.reshape(1, -1)
    for layer in range(depth):
        lam_init = 0.8 - 0.6 * math.exp(-0.3 * layer)
        lam = (jnp.exp(jnp.sum(lam_q1[layer] * lam_k1[layer])) - jnp.exp(jnp.sum(lam_q2[layer] * lam_k2[layer]))
               + lam_init).reshape(1).astype(F32)
        w_proj, w_gate = _split_w_in(w_in[layer])
        pa, pb, qi, ki, wi, pc = _inproj(x2, vec(norm_mix_pre[layer]), tables, w_proj)
        seq = lambda a: a.reshape(b, s, a.shape[-1])
        oa = _dilated_attention(seq(pa))
        ob = _sparse_attention(seq(qi), seq(wi), seq(ki), seq(pb))
        oc = _diff_attention(seq(pc), lam, vec(subln_g[layer]), lam_init)
        flat = lambda a: a.reshape(b * s, a.shape[-1])
        w_merge = (w_gate, w_br_a[layer].astype(BF16), w_br_b[layer].astype(BF16), w_br_c[layer].astype(BF16),
                   w_out[layer].astype(BF16))
        x2 = _merge(x2, vec(norm_mix_pre[layer]), vec(norm_mix_post[layer]), flat(oa), flat(ob), flat(oc), w_merge)
        x2 = _ffn(x2, vec(norm_ffn_pre[layer]), vec(norm_ffn_post[layer]), w_ffn_up[layer].astype(BF16),
                  conv_w[layer], vec(conv_b[layer]), w_ffn_down[layer].astype(BF16), s)
    return x2.reshape(b, s, d)
```

```python
import functools
import math

import jax
import jax.numpy as jnp
from jax import lax
from jax.experimental import pallas as pl
from jax.experimental.pallas import tpu as pltpu

D_MODEL = 1024
HEAD_DIM = 64
ROPE_THETA = 500000.0
ROPE_DIM = HEAD_DIM // 4
NORM_EPS = 1e-6
Q_BLOCK = 128

A_PAIRS = ((128, 1), (512, 4), (2048, 16))
A_GROUPS = len(A_PAIRS)
A_HEADS = 4
A_OUT = A_HEADS * HEAD_DIM
B_HEADS = 4
B_OUT = B_HEADS * HEAD_DIM
IDX_HEADS = 8
IDX_DIM = 64
TOPK_MAX = 256
C_HEADS = 4
C_VDIM = 2 * HEAD_DIM
C_OUT = C_HEADS * C_VDIM
N_BRANCH = 3

A_IN = A_GROUPS * 3 * A_HEADS * HEAD_DIM
B_IN = 3 * B_HEADS * HEAD_DIM
IDX_Q = IDX_HEADS * IDX_DIM
IDX_IN = IDX_Q + IDX_DIM + IDX_HEADS
C_QK = C_HEADS * 2 * HEAD_DIM
C_IN = 2 * C_QK + C_OUT
GATE_IN = N_BRANCH * D_MODEL
D_FF = ((8 * D_MODEL // 3 + 127) // 128) * 128
CONV_WIDTH = 3

LANES = 128
SCALE = HEAD_DIM ** -0.5
NEG = -1e30
INT_MIN = -2 ** 31
VMEM_LIMIT = 56 * 1024 * 1024

F32 = jnp.float32
BF16 = jnp.bfloat16
NT_DIMS = (((1,), (1,)), ((), ()))


def _nt_dot(a, b):
    return lax.dot_general(a, b, NT_DIMS, preferred_element_type=F32)


def _rms(x, g):
    return x * lax.rsqrt(jnp.mean(x * x, axis=-1, keepdims=True) + NORM_EPS) * g


def _inproj_kernel(x_ref, g_ref, ct_ref, s1_ref, s2_ref, wa_ref, wb_ref, wqi_ref, wki_ref, wwi_ref, wc_ref,
                   pa_ref, pb_ref, qi_ref, ki_ref, wi_ref, pc_ref):
    xn = _rms(x_ref[...], g_ref[...]).astype(BF16)
    ct, s1, s2 = ct_ref[...], s1_ref[...], s2_ref[...]

    def rope(y):
        return y * ct + pltpu.roll(y, LANES - ROPE_DIM // 2, 1) * s1 + pltpu.roll(y, ROPE_DIM // 2, 1) * s2

    def project(w_ref, out_ref, n_cols, mode_of_col):
        for c in range(0, n_cols, 2 * LANES):
            y = jnp.dot(xn, w_ref[:, c:c + 2 * LANES], preferred_element_type=F32)
            for h in range(2):
                col = c + h * LANES
                z = y[:, h * LANES:(h + 1) * LANES]
                mode = mode_of_col(col)
                if mode != "v":
                    z = rope(z)
                if mode == "q":
                    z = z * SCALE
                out_ref[:, col:col + LANES] = z.astype(out_ref.dtype)

    qkv_mode = lambda width: (lambda col: ("q", "k", "v")[(col // width) % 3])
    project(wa_ref, pa_ref, A_IN, qkv_mode(A_OUT))
    project(wb_ref, pb_ref, B_IN, qkv_mode(B_OUT))
    project(wqi_ref, qi_ref, IDX_Q, lambda col: "k")
    project(wc_ref, pc_ref, C_IN, qkv_mode(C_QK))
    ki = jnp.dot(xn, wki_ref[...], preferred_element_type=F32)
    ki_ref[...] = rope(ki).astype(BF16)
    wi = jnp.dot(xn, wwi_ref[...], preferred_element_type=F32)
    wi_ref[...] = (wi * (IDX_HEADS ** -0.5)) * (IDX_DIM ** -0.5)


def _inproj(x2, gain, tables, w, tm=512):
    n = x2.shape[0]
    row = lambda width: pl.BlockSpec((tm, width), lambda i: (i, 0))
    full = lambda a: pl.BlockSpec(a.shape, lambda i: (0, 0))
    outs = [(A_IN, BF16), (B_IN, BF16), (IDX_Q, BF16), (LANES, BF16), (LANES, F32), (C_IN, BF16)]
    return pl.pallas_call(
        _inproj_kernel,
        grid=(n // tm,),
        in_specs=[row(D_MODEL), full(gain), row(LANES), row(LANES), row(LANES)] + [full(a) for a in w],
        out_specs=[row(wd) for wd, _ in outs],
        out_shape=[jax.ShapeDtypeStruct((n, wd), dt) for wd, dt in outs],
        compiler_params=pltpu.CompilerParams(dimension_semantics=("arbitrary",), vmem_limit_bytes=VMEM_LIMIT),
        name="inproj",
    )(x2, gain, *tables, *w)


def _head_masks(width):
    lane = lax.broadcasted_iota(jnp.int32, (1, width), 1)
    return [lane // HEAD_DIM == h for h in range(width // HEAD_DIM)]


def _stack_heads(q):
    zero = jnp.zeros_like(q)
    return jnp.concatenate([jnp.where(hm, q, zero) for hm in _head_masks(q.shape[1])], axis=0)


def _unstack_heads(x, rows):
    hmask = _head_masks(HEAD_DIM * (x.shape[0] // rows))
    out = jnp.where(hmask[0], x[0:rows], 0.0)
    for h in range(1, len(hmask)):
        out = jnp.where(hmask[h], x[h * rows:(h + 1) * rows], out)
    return out


def _lane_tile(x, width):
    reps = width // LANES
    return x if reps == 1 else jnp.concatenate([x] * reps, axis=1)


def _flash_init(m_ref, l_ref, acc_ref):
    m_ref[...] = jnp.full(m_ref.shape, NEG, F32)
    l_ref[...] = jnp.zeros(l_ref.shape, F32)
    acc_ref[...] = jnp.zeros(acc_ref.shape, F32)


def _flash_step(s, v, m_ref, l_ref, acc_ref):
    tk = s.shape[1]
    m_old = m_ref[...]
    m_new = jnp.maximum(m_old, jnp.max(s, axis=1, keepdims=True))
    p = jnp.exp(s - _lane_tile(m_new, tk))
    alpha = jnp.exp(m_old - m_new)
    psum = p[:, :LANES]
    for j in range(1, tk // LANES):
        psum = psum + p[:, j * LANES:(j + 1) * LANES]
    l_ref[...] = alpha * l_ref[...] + psum
    acc_ref[...] = (_lane_tile(alpha, acc_ref.shape[1]) * acc_ref[...]
                    + jnp.dot(p.astype(BF16), v, preferred_element_type=F32))
    m_ref[...] = m_new


def _flash_pipeline(n, logits_fn, values_fn, m_ref, l_ref, acc_ref):
    def body(t, s_cur):
        s_next = logits_fn(t + 1)
        _flash_step(s_cur, values_fn(t), m_ref, l_ref, acc_ref)
        return s_next
    s_last = lax.fori_loop(0, n - 1, body, logits_fn(0))
    _flash_step(s_last, values_fn(n - 1), m_ref, l_ref, acc_ref)


def _flash_result(l_ref, acc_ref):
    return acc_ref[...] / jnp.sum(l_ref[...], axis=1, keepdims=True)


A_TILE = max(w for w, _ in A_PAIRS)


def _dilated_kernel(*refs):
    group_refs = [refs[5 * g:5 * g + 5] for g in range(A_GROUPS)]
    o_ref, qs_ref, ks_ref, vs_ref, m_sc, l_sc, n_sc = refs[5 * A_GROUPS:]
    j = pl.program_id(1)
    qi = lax.broadcasted_iota(jnp.int32, (Q_BLOCK, 2 * Q_BLOCK), 0)
    kj = lax.broadcasted_iota(jnp.int32, (Q_BLOCK, 2 * Q_BLOCK), 1)
    rel = kj - qi
    band = (rel >= 0) & (rel <= Q_BLOCK)
    bias_band = jnp.where(band, 0.0, NEG)
    bias_first = jnp.where(band & (kj >= Q_BLOCK), 0.0, NEG)
    halves = A_OUT // LANES

    def get(ref, rows):
        return jnp.concatenate([ref[hh, rows, :] for hh in range(halves)], axis=1)

    def put(ref, rows, val):
        for hh in range(halves):
            ref[hh, rows, :] = val[:, hh * LANES:(hh + 1) * LANES]

    for g, ((window, dil), (q_ref, kp_ref, kc_ref, vp_ref, vc_ref)) in enumerate(zip(A_PAIRS, group_refs)):
        assert window == Q_BLOCK * dil
        put(qs_ref, slice(0, A_TILE), q_ref[...].astype(F32))
        put(ks_ref, slice(0, window), kp_ref[...].astype(F32))
        put(ks_ref, slice(window, window + A_TILE), kc_ref[...].astype(F32))
        put(vs_ref, slice(0, window), vp_ref[...].astype(F32))
        put(vs_ref, slice(window, window + A_TILE), vc_ref[...].astype(F32))
        shift = dil.bit_length() - 1

        def problem(p, carry, g=g, window=window, dil=dil, shift=shift):
            sub = p >> shift
            base = sub * window + (p & (dil - 1))
            stride = dil if dil > 1 else None
            q_rows = pl.ds(base, Q_BLOCK, stride=stride)
            k_rows = pl.ds(base, 2 * Q_BLOCK, stride=stride)
            qstack = _stack_heads(get(qs_ref, q_rows).astype(BF16))
            k = get(ks_ref, k_rows).astype(BF16)
            v = get(vs_ref, k_rows).astype(BF16)
            bias = jnp.where((sub > 0) | (j > 0), bias_band, bias_first)
            s = _nt_dot(qstack, k) + jnp.concatenate([bias] * A_HEADS, axis=0)
            m = jnp.max(s, axis=1, keepdims=True)
            e = jnp.exp(s - m)
            l = jnp.sum(e, axis=1, keepdims=True)
            num = jnp.dot(e.astype(BF16), v, preferred_element_type=F32)
            m_n, l_n, num_n = (_unstack_heads(t, Q_BLOCK) for t in (m, l, num))
            if g == 0:
                put(m_sc, q_rows, m_n)
                put(l_sc, q_rows, l_n)
                put(n_sc, q_rows, num_n)
            else:
                m_o = get(m_sc, q_rows)
                m_x = jnp.maximum(m_o, m_n)
                a, b = jnp.exp(m_o - m_x), jnp.exp(m_n - m_x)
                put(m_sc, q_rows, m_x)
                put(l_sc, q_rows, a * get(l_sc, q_rows) + b * l_n)
                put(n_sc, q_rows, a * get(n_sc, q_rows) + b * num_n)
            return carry

        lax.fori_loop(0, A_TILE // Q_BLOCK, problem, 0)

    all_rows = slice(0, A_TILE)
    o_ref[...] = (get(n_sc, all_rows) / get(l_sc, all_rows)).astype(o_ref.dtype)


def _dilated_attention(pa3):
    b, s, _ = pa3.shape
    in_specs = []
    for g, (window, _) in enumerate(A_PAIRS):
        per = A_TILE // window
        cur = lambda bi, j, col: (bi, j, col)
        prev = lambda bi, j, col, per=per: (bi, jnp.maximum(j * per - 1, 0), col)
        in_specs.append(pl.BlockSpec((None, A_TILE, A_OUT), functools.partial(cur, col=3 * g)))
        for col in (3 * g + 1, 3 * g + 2):
            in_specs.append(pl.BlockSpec((None, window, A_OUT), functools.partial(prev, col=col)))
            in_specs.append(pl.BlockSpec((None, A_TILE, A_OUT), functools.partial(cur, col=col)))
    stage = lambda rows: pltpu.VMEM((A_OUT // LANES, rows, LANES), F32)
    return pl.pallas_call(
        _dilated_kernel,
        grid=(b, s // A_TILE),
        in_specs=in_specs,
        out_specs=pl.BlockSpec((None, A_TILE, A_OUT), lambda bi, j: (bi, j, 0)),
        out_shape=jax.ShapeDtypeStruct((b, s, A_OUT), BF16),
        scratch_shapes=[stage(A_TILE), stage(2 * A_TILE), stage(2 * A_TILE),
                        stage(A_TILE), stage(A_TILE), stage(A_TILE)],
        compiler_params=pltpu.CompilerParams(dimension_semantics=("arbitrary", "arbitrary"),
                                             vmem_limit_bytes=VMEM_LIMIT),
        name="dilated_attn",
    )(*([pa3] * (5 * A_GROUPS)))


SEL_CHUNK = 512
ATT_CHUNK = 512


def _sparse_kernel(qi_ref, wi_ref, ki_ref, q_ref, k_ref, v_ref, o_ref, key_ref, m_ref, l_ref, acc_ref, *, topk):
    i = pl.program_id(1)
    qs = i * Q_BLOCK
    nch = qs // SEL_CHUNK + 1
    nat = qs // ATT_CHUNK + 1
    lane = lax.broadcasted_iota(jnp.int32, (1, LANES), 1)
    qpos = qs + lax.broadcasted_iota(jnp.int32, (Q_BLOCK, 1), 0)

    qi = qi_ref[...]
    stacked = []
    for h in range(IDX_HEADS):
        pair = qi[:, (h // 2) * LANES:(h // 2 + 1) * LANES]
        keep = (lane < IDX_DIM) if h % 2 == 0 else (lane >= IDX_DIM)
        stacked.append(jnp.where(keep, pair, jnp.zeros_like(pair)))
    stacked = jnp.concatenate(stacked, axis=0)
    w = wi_ref[...]
    wcol = [w[:, h:h + 1] for h in range(IDX_HEADS)]

    def to_key(score):
        bits = lax.bitcast_convert_type(score, jnp.int32)
        return bits ^ ((bits >> 31) & 0x7FFFFFFF)

    def chunk_logits(c):
        start = pl.multiple_of(c * SEL_CHUNK, SEL_CHUNK)
        return _nt_dot(stacked, ki_ref[pl.ds(start, SEL_CHUNK), :])

    def chunk_scores(c, logits, group_max):
        start = pl.multiple_of(c * SEL_CHUNK, SEL_CHUNK)
        score = jnp.zeros((Q_BLOCK, SEL_CHUNK), F32)
        for h in range(IDX_HEADS):
            score = score + jnp.maximum(logits[h * Q_BLOCK:(h + 1) * Q_BLOCK], 0.0) * wcol[h]
        kpos = start + lax.broadcasted_iota(jnp.int32, (1, SEL_CHUNK), 1)
        causal = kpos <= qpos
        key_ref[:, pl.ds(start, SEL_CHUNK)] = jnp.where(causal, to_key(score), INT_MIN)
        score = jnp.where(causal, score, -jnp.inf)
        group_max = list(group_max)
        for jj in range(SEL_CHUNK // LANES):
            group_max[jj % 2] = jnp.maximum(group_max[jj % 2], score[:, jj * LANES:(jj + 1) * LANES])
        return tuple(group_max)

    neg_inf = jnp.full((Q_BLOCK, LANES), -jnp.inf, F32)
    gm_even, gm_odd = lax.fori_loop(0, nch, lambda c, gm: chunk_scores(c, chunk_logits(c), gm), (neg_inf, neg_inf))

    def count(pred):
        def body(c, acc):
            start = pl.multiple_of(c * SEL_CHUNK, SEL_CHUNK)
            hit = jnp.where(pred(key_ref[:, pl.ds(start, SEL_CHUNK)]), 1.0, 0.0)
            for jj in range(SEL_CHUNK // LANES):
                acc = acc + hit[:, jj * LANES:(jj + 1) * LANES]
            return acc
        acc = lax.fori_loop(0, nch, body, jnp.zeros((Q_BLOCK, LANES), F32))
        return jnp.sum(acc, axis=1, keepdims=True)

    kf = float(topk)
    assert topk <= 2 * LANES
    searchable = qs >= topk
    lo_f = jnp.minimum(jnp.min(gm_even, axis=1, keepdims=True), jnp.min(gm_odd, axis=1, keepdims=True))
    hi_f = jnp.maximum(jnp.max(gm_even, axis=1, keepdims=True), jnp.max(gm_odd, axis=1, keepdims=True))
    lo0 = jnp.where(searchable, to_key(lo_f), INT_MIN + 1)
    hi0 = to_key(hi_f) + 1
    cnt_lo0 = jnp.where(searchable, kf + 1.0, kf) + jnp.zeros((Q_BLOCK, 1), F32)

    def candidate(lo, hi, cnt_lo):
        mid = (lo >> 1) + (hi >> 1) + (lo & hi & 1)
        open_row = (cnt_lo > kf) & (mid > lo)
        cand = jnp.where((lo < 0) & (hi > 0), 0, jnp.where((lo == 0) & (hi > 1), 1, mid))
        open_row = jnp.where(open_row, 1.0, 0.0)
        return cand, open_row, jnp.max(open_row) > 0.0

    def search_step(carry):
        lo, hi, cnt_lo, cand, open_row, _ = carry
        cnt = count(lambda blk: blk >= cand)
        up = (open_row > 0.0) & (cnt >= kf)
        down = (open_row > 0.0) & (cnt < kf)
        lo, cnt_lo, hi = jnp.where(up, cand, lo), jnp.where(up, cnt, cnt_lo), jnp.where(down, cand, hi)
        return (lo, hi, cnt_lo) + candidate(lo, hi, cnt_lo)

    thr, _, cnt_thr = lax.while_loop(lambda carry: carry[5], search_step,
                                     (lo0, hi0, cnt_lo0) + candidate(lo0, hi0, cnt_lo0))[:3]
    has_ties = jnp.max(cnt_thr) > kf

    @pl.when(has_ties)
    def _():
        need = kf - count(lambda blk: blk > thr)
        r = lax.broadcasted_iota(jnp.int32, (ATT_CHUNK, ATT_CHUNK), 0)
        cc = lax.broadcasted_iota(jnp.int32, (ATT_CHUNK, ATT_CHUNK), 1)
        before = jnp.where(r < cc, 1.0, 0.0).astype(BF16)

        def body(c, seen):
            cols = pl.ds(pl.multiple_of(c * ATT_CHUNK, ATT_CHUNK), ATT_CHUNK)
            blk = key_ref[:, cols]
            eq = blk == thr
            eqf = jnp.where(eq, 1.0, 0.0)
            rank = seen + jnp.dot(eqf.astype(BF16), before, preferred_element_type=F32)
            key_ref[:, cols] = jnp.where(eq & (rank >= need), INT_MIN, blk)
            return seen + jnp.sum(eqf, axis=1, keepdims=True)
        lax.fori_loop(0, nat, body, jnp.zeros((Q_BLOCK, 1), F32))

    qstack = _stack_heads(q_ref[...])

    def logits(c, sel):
        start = pl.multiple_of(c * ATT_CHUNK, ATT_CHUNK)
        bias = jnp.where(sel, 0.0, NEG)
        return _nt_dot(qstack, k_ref[pl.ds(start, ATT_CHUNK), :]) + jnp.concatenate([bias] * B_HEADS, axis=0)

    def values(c):
        return v_ref[pl.ds(pl.multiple_of(c * ATT_CHUNK, ATT_CHUNK), ATT_CHUNK), :]

    def selected(c):
        return key_ref[:, pl.ds(pl.multiple_of(c * ATT_CHUNK, ATT_CHUNK), ATT_CHUNK)] >= thr

    _flash_init(m_ref, l_ref, acc_ref)
    _flash_pipeline(nat, lambda c: logits(c, selected(c)), values, m_ref, l_ref, acc_ref)

    o_ref[...] = _unstack_heads(_flash_result(l_ref, acc_ref), Q_BLOCK).astype(o_ref.dtype)


def _sparse_attention(qi3, wi3, ki3, pb3):
    b, s, _ = pb3.shape
    topk = min(TOPK_MAX, s // 4)
    rows = B_HEADS * Q_BLOCK
    qblk = lambda width, col: pl.BlockSpec((None, Q_BLOCK, width), lambda bi, i: (bi, i, col))
    seq = lambda width, col: pl.BlockSpec((None, s, width), lambda bi, i: (bi, 0, col))
    return pl.pallas_call(
        functools.partial(_sparse_kernel, topk=topk),
        grid=(b, s // Q_BLOCK),
        in_specs=[qblk(IDX_Q, 0), qblk(LANES, 0), seq(LANES, 0), qblk(B_OUT, 0), seq(B_OUT, 1), seq(B_OUT, 2)],
        out_specs=pl.BlockSpec((None, Q_BLOCK, B_OUT), lambda bi, i: (bi, i, 0)),
        out_shape=jax.ShapeDtypeStruct((b, s, B_OUT), BF16),
        scratch_shapes=[pltpu.VMEM((Q_BLOCK, s), jnp.int32),
                        pltpu.VMEM((rows, LANES), F32), pltpu.VMEM((rows, LANES), F32),
                        pltpu.VMEM((rows, B_OUT), F32)],
        compiler_params=pltpu.CompilerParams(dimension_semantics=("arbitrary", "arbitrary"),
                                             vmem_limit_bytes=VMEM_LIMIT),
        name="sparse_attn",
    )(qi3, wi3, ki3, pb3, pb3, pb3)


DIFF_TQ = 512
DIFF_TK = 1024


def _diff_kernel(lam_ref, g_ref, q_ref, k_ref, v_ref, o_ref, m_ref, l_ref, acc_ref, *, out_scale):
    i = pl.program_id(2)
    qstack = _stack_heads(q_ref[...])
    _flash_init(m_ref, l_ref, acc_ref)

    def logits(kb):
        return _nt_dot(qstack, k_ref[pl.ds(pl.multiple_of(kb * DIFF_TK, DIFF_TK), DIFF_TK), :])

    def values(kb):
        return v_ref[pl.ds(pl.multiple_of(kb * DIFF_TK, DIFF_TK), DIFF_TK), :]

    n_full = (i * DIFF_TQ) // DIFF_TK

    def full_block(kb, carry):
        _flash_step(logits(kb), values(kb), m_ref, l_ref, acc_ref)
        return carry

    lax.fori_loop(0, n_full, full_block, 0)
    qpos = i * DIFF_TQ + (lax.broadcasted_iota(jnp.int32, (2 * DIFF_TQ, DIFF_TK), 0) & (DIFF_TQ - 1))
    kpos = n_full * DIFF_TK + lax.broadcasted_iota(jnp.int32, (2 * DIFF_TQ, DIFF_TK), 1)
    _flash_step(logits(n_full) + jnp.where(kpos <= qpos, 0.0, NEG), values(n_full), m_ref, l_ref, acc_ref)

    res = _flash_result(l_ref, acc_ref)
    o = res[:DIFF_TQ] - lam_ref[0] * res[DIFF_TQ:]
    o_ref[...] = (_rms(o, g_ref[...]) * out_scale).astype(o_ref.dtype)


def _diff_attention(pc3, lam, subln_g, lam_init):
    b, s, _ = pc3.shape
    nq = C_QK // LANES
    return pl.pallas_call(
        functools.partial(_diff_kernel, out_scale=1.0 - lam_init),
        grid=(b, C_HEADS, s // DIFF_TQ),
        in_specs=[pl.BlockSpec(memory_space=pltpu.SMEM),
                  pl.BlockSpec((1, C_VDIM), lambda bi, h, i: (0, 0)),
                  pl.BlockSpec((None, DIFF_TQ, LANES), lambda bi, h, i: (bi, i, h)),
                  pl.BlockSpec((None, s, LANES), lambda bi, h, i: (bi, 0, nq + h)),
                  pl.BlockSpec((None, s, C_VDIM), lambda bi, h, i: (bi, 0, 2 * nq + h))],
        out_specs=pl.BlockSpec((None, DIFF_TQ, C_VDIM), lambda bi, h, i: (bi, i, h)),
        out_shape=jax.ShapeDtypeStruct((b, s, C_OUT), BF16),
        scratch_shapes=[pltpu.VMEM((2 * DIFF_TQ, LANES), F32), pltpu.VMEM((2 * DIFF_TQ, LANES), F32),
                        pltpu.VMEM((2 * DIFF_TQ, C_VDIM), F32)],
        compiler_params=pltpu.CompilerParams(dimension_semantics=("arbitrary", "arbitrary", "arbitrary"),
                                             vmem_limit_bytes=VMEM_LIMIT),
        name="diff_attn",
    )(lam, subln_g, pc3, pc3, pc3)


def _merge_kernel(x_ref, gpre_ref, gpost_ref, oa_ref, ob_ref, oc_ref, wg_ref, wa_ref, wb_ref, wc_ref, wo_ref,
                  out_ref, y_ref, h_ref):
    x = x_ref[...]
    xn = _rms(x, gpre_ref[...]).astype(BF16)
    branches = ((oa_ref, wa_ref), (ob_ref, wb_ref), (oc_ref, wc_ref))
    cw = 2 * LANES
    for c in range(0, D_MODEL, cw):
        y = jnp.zeros((x.shape[0], cw), F32)
        for j, (o_ref, w_ref) in enumerate(branches):
            logit = jnp.dot(xn, wg_ref[:, j * D_MODEL + c:j * D_MODEL + c + cw], preferred_element_type=F32)
            gate = 1.0 / (1.0 + jnp.exp(-logit))
            y = y + gate * jnp.dot(o_ref[...], w_ref[:, c:c + cw], preferred_element_type=F32)
        y_ref[:, c:c + cw] = y.astype(BF16)
    ssq = jnp.zeros((x.shape[0], 1), F32)
    for c in range(0, D_MODEL, cw):
        h = jnp.dot(y_ref[...], wo_ref[:, c:c + cw], preferred_element_type=F32)
        h_ref[:, c:c + cw] = h
        ssq = ssq + jnp.sum(h * h, axis=1, keepdims=True)
    inv = lax.rsqrt(ssq * (1.0 / D_MODEL) + NORM_EPS)
    out_ref[...] = x + h_ref[...] * inv * gpost_ref[...]


def _merge(x2, gpre, gpost, oa, ob, oc, w, tm=512):
    n = x2.shape[0]
    row = lambda width: pl.BlockSpec((tm, width), lambda i: (i, 0))
    full = lambda a: pl.BlockSpec(a.shape, lambda i: (0, 0))
    return pl.pallas_call(
        _merge_kernel,
        grid=(n // tm,),
        in_specs=[row(D_MODEL), full(gpre), full(gpost), row(A_OUT), row(B_OUT), row(C_OUT)] + [full(a) for a in w],
        out_specs=row(D_MODEL),
        out_shape=jax.ShapeDtypeStruct((n, D_MODEL), F32),
        scratch_shapes=[pltpu.VMEM((tm, D_MODEL), BF16), pltpu.VMEM((tm, D_MODEL), F32)],
        compiler_params=pltpu.CompilerParams(dimension_semantics=("arbitrary",), vmem_limit_bytes=VMEM_LIMIT),
        name="merge",
    )(x2, gpre, gpost, oa, ob, oc, *w)


FFN_CHUNK = 256
CARRY_ROWS = 8


def _ffn_kernel(x_ref, gpre_ref, gpost_ref, wup_ref, cw_ref, cb_ref, wdn_ref, out_ref, carry_ref, acc_ref,
                *, tiles_per_seq):
    x = x_ref[...]
    tm = x.shape[0]
    xn = _rms(x, gpre_ref[...]).astype(BF16)
    first = pl.program_id(0) % tiles_per_seq == 0
    row = lax.broadcasted_iota(jnp.int32, (tm, 1), 0)

    def conv(col):
        h = jnp.dot(xn, wup_ref[:, col:col + FFN_CHUNK], preferred_element_type=F32)
        prev = jnp.where(first, 0.0, carry_ref[:, col:col + FFN_CHUNK])
        carry_ref[:, col:col + FFN_CHUNK] = h[tm - CARRY_ROWS:, :]
        p1 = prev[CARRY_ROWS - 1:CARRY_ROWS, :]
        p2 = prev[CARRY_ROWS - 2:CARRY_ROWS - 1, :]
        h1 = jnp.where(row == 0, p1, pltpu.roll(h, 1, 0))
        h2 = jnp.where(row == 0, p2, jnp.where(row == 1, p1, pltpu.roll(h, 2, 0)))
        w = cw_ref[:, col:col + FFN_CHUNK]
        return h2 * w[0:1, :] + h1 * w[1:2, :] + h * w[2:3, :] + cb_ref[:, col:col + FFN_CHUNK]

    for c in range(0, D_FF, FFN_CHUNK):
        g = conv(c)
        u = conv(D_FF + c)
        gelu = 0.5 * g * (1.0 + jnp.tanh(math.sqrt(2.0 / math.pi) * (g + 0.044715 * (g * g * g))))
        part = jnp.dot((gelu * u).astype(BF16), wdn_ref[c:c + FFN_CHUNK, :], preferred_element_type=F32)
        if c == 0:
            acc_ref[...] = part
        else:
            acc_ref[...] += part
    out_ref[...] = x + _rms(acc_ref[...], gpost_ref[...])


def _ffn(x2, gpre, gpost, wup, conv_w, conv_b, wdn, seq_len, tm=512):
    n = x2.shape[0]
    row = lambda width: pl.BlockSpec((tm, width), lambda i: (i, 0))
    full = lambda a: pl.BlockSpec(a.shape, lambda i: (0, 0))
    return pl.pallas_call(
        functools.partial(_ffn_kernel, tiles_per_seq=seq_len // tm),
        grid=(n // tm,),
        in_specs=[row(D_MODEL), full(gpre), full(gpost), full(wup), full(conv_w), full(conv_b), full(wdn)],
        out_specs=row(D_MODEL),
        out_shape=jax.ShapeDtypeStruct((n, D_MODEL), F32),
        scratch_shapes=[pltpu.VMEM((CARRY_ROWS, 2 * D_FF), F32), pltpu.VMEM((tm, D_MODEL), F32)],
        compiler_params=pltpu.CompilerParams(dimension_semantics=("arbitrary",), vmem_limit_bytes=VMEM_LIMIT),
        name="conv_ffn",
    )(x2, gpre, gpost, wup, conv_w, conv_b, wdn)


def _rope_lane_tables(positions):
    half = ROPE_DIM // 2
    inv = ROPE_THETA ** (-jnp.arange(0, ROPE_DIM, 2, dtype=F32) / ROPE_DIM)
    ang = positions.astype(F32).reshape(-1, 1) * inv
    cos, sin = jnp.cos(ang), jnp.sin(ang)
    n = ang.shape[0]
    rest = HEAD_DIM - ROPE_DIM
    ones, zeros, zh = jnp.ones((n, rest), F32), jnp.zeros((n, rest), F32), jnp.zeros((n, half), F32)
    ct = jnp.concatenate([cos, cos, ones], axis=1)
    s1 = jnp.concatenate([-sin, zh, zeros], axis=1)
    s2 = jnp.concatenate([zh, sin, zeros], axis=1)
    return tuple(jnp.tile(t, (1, LANES // HEAD_DIM)) for t in (ct, s1, s2))


def _split_w_in(w_in):
    o_b, o_i, o_c, o_g = A_IN, A_IN + B_IN, A_IN + B_IN + IDX_IN, A_IN + B_IN + IDX_IN + C_IN
    wk = w_in[:, o_i + IDX_Q:o_i + IDX_Q + IDX_DIM]
    ww = w_in[:, o_i + IDX_Q + IDX_DIM:o_c]
    proj = (w_in[:, :o_b], w_in[:, o_b:o_i], w_in[:, o_i:o_i + IDX_Q],
            jnp.concatenate([wk, wk], axis=1),
            jnp.pad(ww, ((0, 0), (0, LANES - IDX_HEADS))),
            w_in[:, o_c:o_g])
    return tuple(a.astype(BF16) for a in proj), w_in[:, o_g:].astype(BF16)


def kernel(x, positions, w_in, w_br_a, w_br_b, w_br_c, w_out, lam_q1, lam_k1, lam_q2, lam_k2, subln_g,
           norm_mix_pre, norm_mix_post, norm_ffn_pre, norm_ffn_post, w_ffn_up, conv_w, conv_b, w_ffn_down):
    b, s, d = x.shape
    depth = w_in.shape[0]
    assert d == D_MODEL and s % A_TILE == 0 and s % SEL_CHUNK == 0 and s % DIFF_TK == 0
    tables = _rope_lane_tables(positions)
    x2 = x.reshape(b * s, d)
    vec = lambda a: a.reshape(1, -1)
    for layer in range(depth):
        lam_init = 0.8 - 0.6 * math.exp(-0.3 * layer)
        lam = (jnp.exp(jnp.sum(lam_q1[layer] * lam_k1[layer])) - jnp.exp(jnp.sum(lam_q2[layer] * lam_k2[layer]))
               + lam_init).reshape(1).astype(F32)
        w_proj, w_gate = _split_w_in(w_in[layer])
        pa, pb, qi, ki, wi, pc = _inproj(x2, vec(norm_mix_pre[layer]), tables, w_proj)
        seq = lambda a: a.reshape(b, s, a.shape[-1])
        oa = _dilated_attention(seq(pa))
        ob = _sparse_attention(seq(qi), seq(wi), seq(ki), seq(pb))
        oc = _diff_attention(seq(pc), lam, vec(subln_g[layer]), lam_init)
        flat = lambda a: a.reshape(b * s, a.shape[-1])
        w_merge = (w_gate, w_br_a[layer].astype(BF16), w_br_b[layer].astype(BF16), w_br_c[layer].astype(BF16),
                   w_out[layer].astype(BF16))
        x2 = _merge(x2, vec(norm_mix_pre[layer]), vec(norm_mix_post[layer]), flat(oa), flat(ob), flat(oc), w_merge)
        x2 = _ffn(x2, vec(norm_ffn_pre[layer]), vec(norm_ffn_post[layer]), w_ffn_up[layer].astype(BF16),
                  conv_w[layer], vec(conv_b[layer]), w_ffn_down[layer].astype(BF16), s)
    return x2.reshape(b, s, d)
```

```python
import functools
import math

import jax
import jax.numpy as jnp
from jax import lax
from jax.experimental import pallas as pl
from jax.experimental.pallas import tpu as pltpu

D_MODEL = 1024
HEAD_DIM = 64
ROPE_THETA = 500000.0
ROPE_DIM = HEAD_DIM // 4
NORM_EPS = 1e-6
Q_BLOCK = 128

A_PAIRS = ((128, 1), (512, 4), (2048, 16))
A_GROUPS = len(A_PAIRS)
A_HEADS = 4
A_OUT = A_HEADS * HEAD_DIM
B_HEADS = 4
B_OUT = B_HEADS * HEAD_DIM
IDX_HEADS = 8
IDX_DIM = 64
TOPK_MAX = 256
C_HEADS = 4
C_VDIM = 2 * HEAD_DIM
C_OUT = C_HEADS * C_VDIM
N_BRANCH = 3

A_IN = A_GROUPS * 3 * A_HEADS * HEAD_DIM
B_IN = 3 * B_HEADS * HEAD_DIM
IDX_Q = IDX_HEADS * IDX_DIM
IDX_IN = IDX_Q + IDX_DIM + IDX_HEADS
C_QK = C_HEADS * 2 * HEAD_DIM
C_IN = 2 * C_QK + C_OUT
GATE_IN = N_BRANCH * D_MODEL
D_FF = ((8 * D_MODEL // 3 + 127) // 128) * 128
CONV_WIDTH = 3

LANES = 128
SCALE = HEAD_DIM ** -0.5
NEG = -1e30
INT_MIN = -2 ** 31
VMEM_LIMIT = 56 * 1024 * 1024

F32 = jnp.float32
BF16 = jnp.bfloat16
NT_DIMS = (((1,), (1,)), ((), ()))


def _nt_dot(a, b):
    return lax.dot_general(a, b, NT_DIMS, preferred_element_type=F32)


def _rms(x, g):
    return x * lax.rsqrt(jnp.mean(x * x, axis=-1, keepdims=True) + NORM_EPS) * g


def _inproj_kernel(x_ref, g_ref, ct_ref, s1_ref, s2_ref, wa_ref, wb_ref, wqi_ref, wki_ref, wwi_ref, wc_ref,
                   pa_ref, pb_ref, qi_ref, ki_ref, wi_ref, pc_ref):
    xn = _rms(x_ref[...], g_ref[...]).astype(BF16)
    ct, s1, s2 = ct_ref[...], s1_ref[...], s2_ref[...]

    def rope(y):
        return y * ct + pltpu.roll(y, LANES - ROPE_DIM // 2, 1) * s1 + pltpu.roll(y, ROPE_DIM // 2, 1) * s2

    def project(w_ref, out_ref, n_cols, mode_of_col):
        for c in range(0, n_cols, 2 * LANES):
            y = jnp.dot(xn, w_ref[:, c:c + 2 * LANES], preferred_element_type=F32)
            for h in range(2):
                col = c + h * LANES
                z = y[:, h * LANES:(h + 1) * LANES]
                mode = mode_of_col(col)
                if mode != "v":
                    z = rope(z)
                if mode == "q":
                    z = z * SCALE
                out_ref[:, col:col + LANES] = z.astype(out_ref.dtype)

    qkv_mode = lambda width: (lambda col: ("q", "k", "v")[(col // width) % 3])
    project(wa_ref, pa_ref, A_IN, qkv_mode(A_OUT))
    project(wb_ref, pb_ref, B_IN, qkv_mode(B_OUT))
    project(wqi_ref, qi_ref, IDX_Q, lambda col: "k")
    project(wc_ref, pc_ref, C_IN, qkv_mode(C_QK))
    ki = jnp.dot(xn, wki_ref[...], preferred_element_type=F32)
    ki_ref[...] = rope(ki).astype(BF16)
    wi = jnp.dot(xn, wwi_ref[...], preferred_element_type=F32)
    wi_ref[...] = (wi * (IDX_HEADS ** -0.5)) * (IDX_DIM ** -0.5)


def _inproj(x2, gain, tables, w, tm=512):
    n = x2.shape[0]
    row = lambda width: pl.BlockSpec((tm, width), lambda i: (i, 0))
    full = lambda a: pl.BlockSpec(a.shape, lambda i: (0, 0))
    outs = [(A_IN, BF16), (B_IN, BF16), (IDX_Q, BF16), (LANES, BF16), (LANES, F32), (C_IN, BF16)]
    return pl.pallas_call(
        _inproj_kernel,
        grid=(n // tm,),
        in_specs=[row(D_MODEL), full(gain), row(LANES), row(LANES), row(LANES)] + [full(a) for a in w],
        out_specs=[row(wd) for wd, _ in outs],
        out_shape=[jax.ShapeDtypeStruct((n, wd), dt) for wd, dt in outs],
        compiler_params=pltpu.CompilerParams(dimension_semantics=("arbitrary",), vmem_limit_bytes=VMEM_LIMIT),
        name="inproj",
    )(x2, gain, *tables, *w)


def _head_masks(width):
    lane = lax.broadcasted_iota(jnp.int32, (1, width), 1)
    return [lane // HEAD_DIM == h for h in range(width // HEAD_DIM)]


def _stack_heads(q):
    zero = jnp.zeros_like(q)
    return jnp.concatenate([jnp.where(hm, q, zero) for hm in _head_masks(q.shape[1])], axis=0)


def _unstack_heads(x, rows):
    hmask = _head_masks(HEAD_DIM * (x.shape[0] // rows))
    out = jnp.where(hmask[0], x[0:rows], 0.0)
    for h in range(1, len(hmask)):
        out = jnp.where(hmask[h], x[h * rows:(h + 1) * rows], out)
    return out


def _lane_tile(x, width):
    reps = width // LANES
    return x if reps == 1 else jnp.concatenate([x] * reps, axis=1)


def _flash_init(m_ref, l_ref, acc_ref):
    m_ref[...] = jnp.full(m_ref.shape, NEG, F32)
    l_ref[...] = jnp.zeros(l_ref.shape, F32)
    acc_ref[...] = jnp.zeros(acc_ref.shape, F32)


def _flash_step(s, v, m_ref, l_ref, acc_ref):
    tk = s.shape[1]
    m_old = m_ref[...]
    m_new = jnp.maximum(m_old, jnp.max(s, axis=1, keepdims=True))
    p = jnp.exp(s - _lane_tile(m_new, tk))
    alpha = jnp.exp(m_old - m_new)
    psum = p[:, :LANES]
    for j in range(1, tk // LANES):
        psum = psum + p[:, j * LANES:(j + 1) * LANES]
    l_ref[...] = alpha * l_ref[...] + psum
    acc_ref[...] = (_lane_tile(alpha, acc_ref.shape[1]) * acc_ref[...]
                    + jnp.dot(p.astype(BF16), v, preferred_element_type=F32))
    m_ref[...] = m_new


def _flash_pipeline(n, logits_fn, values_fn, m_ref, l_ref, acc_ref):
    def body(t, s_cur):
        s_next = logits_fn(t + 1)
        _flash_step(s_cur, values_fn(t), m_ref, l_ref, acc_ref)
        return s_next
    s_last = lax.fori_loop(0, n - 1, body, logits_fn(0))
    _flash_step(s_last, values_fn(n - 1), m_ref, l_ref, acc_ref)


def _flash_result(l_ref, acc_ref):
    return acc_ref[...] / jnp.sum(l_ref[...], axis=1, keepdims=True)


A_TILE = max(w for w, _ in A_PAIRS)


def _dilated_kernel(*refs):
    group_refs = [refs[5 * g:5 * g + 5] for g in range(A_GROUPS)]
    o_ref, qs_ref, ks_ref, vs_ref, m_sc, l_sc, n_sc = refs[5 * A_GROUPS:]
    j = pl.program_id(1)
    qi = lax.broadcasted_iota(jnp.int32, (Q_BLOCK, 2 * Q_BLOCK), 0)
    kj = lax.broadcasted_iota(jnp.int32, (Q_BLOCK, 2 * Q_BLOCK), 1)
    rel = kj - qi
    band = (rel >= 0) & (rel <= Q_BLOCK)
    bias_band = jnp.where(band, 0.0, NEG)
    bias_first = jnp.where(band & (kj >= Q_BLOCK), 0.0, NEG)
    halves = A_OUT // LANES

    def get(ref, rows):
        return jnp.concatenate([ref[hh, rows, :] for hh in range(halves)], axis=1)

    def put(ref, rows, val):
        for hh in range(halves):
            ref[hh, rows, :] = val[:, hh * LANES:(hh + 1) * LANES]

    for g, ((window, dil), (q_ref, kp_ref, kc_ref, vp_ref, vc_ref)) in enumerate(zip(A_PAIRS, group_refs)):
        assert window == Q_BLOCK * dil
        put(qs_ref, slice(0, A_TILE), q_ref[...].astype(F32))
        put(ks_ref, slice(0, window), kp_ref[...].astype(F32))
        put(ks_ref, slice(window, window + A_TILE), kc_ref[...].astype(F32))
        put(vs_ref, slice(0, window), vp_ref[...].astype(F32))
        put(vs_ref, slice(window, window + A_TILE), vc_ref[...].astype(F32))
        shift = dil.bit_length() - 1

        def problem(p, carry, g=g, window=window, dil=dil, shift=shift):
            sub = p >> shift
            base = sub * window + (p & (dil - 1))
            stride = dil if dil > 1 else None
            q_rows = pl.ds(base, Q_BLOCK, stride=stride)
            k_rows = pl.ds(base, 2 * Q_BLOCK, stride=stride)
            qstack = _stack_heads(get(qs_ref, q_rows).astype(BF16))
            k = get(ks_ref, k_rows).astype(BF16)
            v = get(vs_ref, k_rows).astype(BF16)
            bias = jnp.where((sub > 0) | (j > 0), bias_band, bias_first)
            s = _nt_dot(qstack, k) + jnp.concatenate([bias] * A_HEADS, axis=0)
            m = jnp.max(s, axis=1, keepdims=True)
            e = jnp.exp(s - m)
            l = jnp.sum(e, axis=1, keepdims=True)
            num = jnp.dot(e.astype(BF16), v, preferred_element_type=F32)
            m_n, l_n, num_n = (_unstack_heads(t, Q_BLOCK) for t in (m, l, num))
            if g == 0:
                put(m_sc, q_rows, m_n)
                put(l_sc, q_rows, l_n)
                put(n_sc, q_rows, num_n)
            else:
                m_o = get(m_sc, q_rows)
                m_x = jnp.maximum(m_o, m_n)
                a, b = jnp.exp(m_o - m_x), jnp.exp(m_n - m_x)
                put(m_sc, q_rows, m_x)
                put(l_sc, q_rows, a * get(l_sc, q_rows) + b * l_n)
                put(n_sc, q_rows, a * get(n_sc, q_rows) + b * num_n)
            return carry

        lax.fori_loop(0, A_TILE // Q_BLOCK, problem, 0, unroll=4)

    all_rows = slice(0, A_TILE)
    o_ref[...] = (get(n_sc, all_rows) / get(l_sc, all_rows)).astype(o_ref.dtype)


def _dilated_attention(pa3):
    b, s, _ = pa3.shape
    in_specs = []
    for g, (window, _) in enumerate(A_PAIRS):
        per = A_TILE // window
        cur = lambda bi, j, col: (bi, j, col)
        prev = lambda bi, j, col, per=per: (bi, jnp.maximum(j * per - 1, 0), col)
        in_specs.append(pl.BlockSpec((None, A_TILE, A_OUT), functools.partial(cur, col=3 * g)))
        for col in (3 * g + 1, 3 * g + 2):
            in_specs.append(pl.BlockSpec((None, window, A_OUT), functools.partial(prev, col=col)))
            in_specs.append(pl.BlockSpec((None, A_TILE, A_OUT), functools.partial(cur, col=col)))
    stage = lambda rows: pltpu.VMEM((A_OUT // LANES, rows, LANES), F32)
    return pl.pallas_call(
        _dilated_kernel,
        grid=(b, s // A_TILE),
        in_specs=in_specs,
        out_specs=pl.BlockSpec((None, A_TILE, A_OUT), lambda bi, j: (bi, j, 0)),
        out_shape=jax.ShapeDtypeStruct((b, s, A_OUT), BF16),
        scratch_shapes=[stage(A_TILE), stage(2 * A_TILE), stage(2 * A_TILE),
                        stage(A_TILE), stage(A_TILE), stage(A_TILE)],
        compiler_params=pltpu.CompilerParams(dimension_semantics=("arbitrary", "arbitrary"),
                                             vmem_limit_bytes=VMEM_LIMIT),
        name="dilated_attn",
    )(*([pa3] * (5 * A_GROUPS)))


SEL_TQ = 128
SEL_CHUNK = 512
ATT_CHUNK = 512


def _sparse_kernel(qi_ref, wi_ref, ki_ref, q_ref, k_ref, v_ref, o_ref, key_ref, m_ref, l_ref, acc_ref, *, topk):
    i = pl.program_id(1)
    qs = i * SEL_TQ
    nch = (qs + SEL_TQ - 1) // SEL_CHUNK + 1
    nat = (qs + SEL_TQ - 1) // ATT_CHUNK + 1
    lane = lax.broadcasted_iota(jnp.int32, (1, LANES), 1)
    qpos = qs + lax.broadcasted_iota(jnp.int32, (SEL_TQ, 1), 0)

    qi = qi_ref[...]
    stacked = []
    for h in range(IDX_HEADS):
        pair = qi[:, (h // 2) * LANES:(h // 2 + 1) * LANES]
        keep = (lane < IDX_DIM) if h % 2 == 0 else (lane >= IDX_DIM)
        stacked.append(jnp.where(keep, pair, jnp.zeros_like(pair)))
    stacked = jnp.concatenate(stacked, axis=0)
    w = wi_ref[...]
    wcol = [w[:, h:h + 1] for h in range(IDX_HEADS)]

    def to_key(score):
        bits = lax.bitcast_convert_type(score, jnp.int32)
        return bits ^ ((bits >> 31) & 0x7FFFFFFF)

    def chunk_logits(c):
        start = pl.multiple_of(c * SEL_CHUNK, SEL_CHUNK)
        return _nt_dot(stacked, ki_ref[pl.ds(start, SEL_CHUNK), :])

    def chunk_scores(c, logits, group_max):
        start = pl.multiple_of(c * SEL_CHUNK, SEL_CHUNK)
        score = jnp.zeros((SEL_TQ, SEL_CHUNK), F32)
        for h in range(IDX_HEADS):
            score = score + jnp.maximum(logits[h * SEL_TQ:(h + 1) * SEL_TQ], 0.0) * wcol[h]
        kpos = start + lax.broadcasted_iota(jnp.int32, (1, SEL_CHUNK), 1)
        causal = kpos <= qpos
        key_ref[:, pl.ds(start, SEL_CHUNK)] = jnp.where(causal, to_key(score), INT_MIN)
        score = jnp.where(causal, score, -jnp.inf)
        group_max = list(group_max)
        for jj in range(SEL_CHUNK // LANES):
            group_max[jj % 2] = jnp.maximum(group_max[jj % 2], score[:, jj * LANES:(jj + 1) * LANES])
        return tuple(group_max)

    neg_inf = jnp.full((SEL_TQ, LANES), -jnp.inf, F32)
    gm_even, gm_odd = lax.fori_loop(0, nch, lambda c, gm: chunk_scores(c, chunk_logits(c), gm), (neg_inf, neg_inf))

    def count(compare, level):
        blocks = [slice(r, r + Q_BLOCK) for r in range(0, SEL_TQ, Q_BLOCK)]

        def body(c, accs):
            cols = pl.ds(pl.multiple_of(c * SEL_CHUNK, SEL_CHUNK), SEL_CHUNK)
            out = []
            for rows, acc in zip(blocks, accs):
                hit = jnp.where(compare(key_ref[rows, cols], level[rows]), 1.0, 0.0)
                for jj in range(SEL_CHUNK // LANES):
                    acc = acc + hit[:, jj * LANES:(jj + 1) * LANES]
                out.append(acc)
            return tuple(out)
        accs = lax.fori_loop(0, nch, body, tuple(jnp.zeros((Q_BLOCK, LANES), F32) for _ in blocks))
        return jnp.concatenate([jnp.sum(acc, axis=1, keepdims=True) for acc in accs], axis=0)

    kf = float(topk)
    assert topk <= 2 * LANES
    searchable = qs >= topk
    lo_f = jnp.minimum(jnp.min(gm_even, axis=1, keepdims=True), jnp.min(gm_odd, axis=1, keepdims=True))
    hi_f = jnp.maximum(jnp.max(gm_even, axis=1, keepdims=True), jnp.max(gm_odd, axis=1, keepdims=True))
    lo0 = jnp.where(searchable, to_key(lo_f), INT_MIN + 1)
    hi0 = to_key(hi_f) + 1
    cnt_lo0 = jnp.where(searchable, kf + 1.0, kf) + jnp.zeros((SEL_TQ, 1), F32)

    def candidate(lo, hi, cnt_lo):
        mid = (lo >> 1) + (hi >> 1) + (lo & hi & 1)
        open_row = (cnt_lo > kf) & (mid > lo)
        cand = jnp.where((lo < 0) & (hi > 0), 0, jnp.where((lo == 0) & (hi > 1), 1, mid))
        open_row = jnp.where(open_row, 1.0, 0.0)
        return cand, open_row, jnp.max(open_row) > 0.0

    def search_step(carry):
        lo, hi, cnt_lo, cand, open_row, _ = carry
        cnt = count(lambda blk, level: blk >= level, cand)
        up = (open_row > 0.0) & (cnt >= kf)
        down = (open_row > 0.0) & (cnt < kf)
        lo, cnt_lo, hi = jnp.where(up, cand, lo), jnp.where(up, cnt, cnt_lo), jnp.where(down, cand, hi)
        return (lo, hi, cnt_lo) + candidate(lo, hi, cnt_lo)

    thr, _, cnt_thr = lax.while_loop(lambda carry: carry[5], lambda carry: search_step(search_step(carry)),
                                     (lo0, hi0, cnt_lo0) + candidate(lo0, hi0, cnt_lo0))[:3]
    has_ties = jnp.max(cnt_thr) > kf

    @pl.when(has_ties)
    def _():
        need = kf - count(lambda blk, level: blk > level, thr)
        r = lax.broadcasted_iota(jnp.int32, (ATT_CHUNK, ATT_CHUNK), 0)
        cc = lax.broadcasted_iota(jnp.int32, (ATT_CHUNK, ATT_CHUNK), 1)
        before = jnp.where(r < cc, 1.0, 0.0).astype(BF16)

        def body(c, seen):
            cols = pl.ds(pl.multiple_of(c * ATT_CHUNK, ATT_CHUNK), ATT_CHUNK)
            blk = key_ref[:, cols]
            eq = blk == thr
            eqf = jnp.where(eq, 1.0, 0.0)
            rank = seen + jnp.dot(eqf.astype(BF16), before, preferred_element_type=F32)
            key_ref[:, cols] = jnp.where(eq & (rank >= need), INT_MIN, blk)
            return seen + jnp.sum(eqf, axis=1, keepdims=True)
        lax.fori_loop(0, nat, body, jnp.zeros((SEL_TQ, 1), F32))

    qstack = _stack_heads(q_ref[...])

    def logits(c, sel):
        start = pl.multiple_of(c * ATT_CHUNK, ATT_CHUNK)
        bias = jnp.where(sel, 0.0, NEG)
        return _nt_dot(qstack, k_ref[pl.ds(start, ATT_CHUNK), :]) + jnp.concatenate([bias] * B_HEADS, axis=0)

    def values(c):
        return v_ref[pl.ds(pl.multiple_of(c * ATT_CHUNK, ATT_CHUNK), ATT_CHUNK), :]

    def selected(c):
        return key_ref[:, pl.ds(pl.multiple_of(c * ATT_CHUNK, ATT_CHUNK), ATT_CHUNK)] >= thr

    _flash_init(m_ref, l_ref, acc_ref)
    _flash_pipeline(nat, lambda c: logits(c, selected(c)), values, m_ref, l_ref, acc_ref)

    o_ref[...] = _unstack_heads(_flash_result(l_ref, acc_ref), SEL_TQ).astype(o_ref.dtype)


def _sparse_attention(qi3, wi3, ki3, pb3):
    b, s, _ = pb3.shape
    topk = min(TOPK_MAX, s // 4)
    rows = B_HEADS * SEL_TQ
    qblk = lambda width, col: pl.BlockSpec((None, SEL_TQ, width), lambda bi, i: (bi, i, col))
    seq = lambda width, col: pl.BlockSpec((None, s, width), lambda bi, i: (bi, 0, col))
    return pl.pallas_call(
        functools.partial(_sparse_kernel, topk=topk),
        grid=(b, s // SEL_TQ),
        in_specs=[qblk(IDX_Q, 0), qblk(LANES, 0), seq(LANES, 0), qblk(B_OUT, 0), seq(B_OUT, 1), seq(B_OUT, 2)],
        out_specs=pl.BlockSpec((None, SEL_TQ, B_OUT), lambda bi, i: (bi, i, 0)),
        out_shape=jax.ShapeDtypeStruct((b, s, B_OUT), BF16),
        scratch_shapes=[pltpu.VMEM((SEL_TQ, s), jnp.int32),
                        pltpu.VMEM((rows, LANES), F32), pltpu.VMEM((rows, LANES), F32),
                        pltpu.VMEM((rows, B_OUT), F32)],
        compiler_params=pltpu.CompilerParams(dimension_semantics=("arbitrary", "arbitrary"),
                                             vmem_limit_bytes=VMEM_LIMIT),
        name="sparse_attn",
    )(qi3, wi3, ki3, pb3, pb3, pb3)


DIFF_TQ = 512
DIFF_TK = 1024


def _diff_kernel(lam_ref, g_ref, q_ref, k_ref, v_ref, o_ref, m_ref, l_ref, acc_ref, *, out_scale):
    i = pl.program_id(2)
    qstack = _stack_heads(q_ref[...])
    _flash_init(m_ref, l_ref, acc_ref)

    def logits(kb):
        return _nt_dot(qstack, k_ref[pl.ds(pl.multiple_of(kb * DIFF_TK, DIFF_TK), DIFF_TK), :])

    def values(kb):
        return v_ref[pl.ds(pl.multiple_of(kb * DIFF_TK, DIFF_TK), DIFF_TK), :]

    n_full = (i * DIFF_TQ) // DIFF_TK

    def full_block(kb, carry):
        _flash_step(logits(kb), values(kb), m_ref, l_ref, acc_ref)
        return carry

    lax.fori_loop(0, n_full, full_block, 0)
    qpos = i * DIFF_TQ + (lax.broadcasted_iota(jnp.int32, (2 * DIFF_TQ, DIFF_TK), 0) & (DIFF_TQ - 1))
    kpos = n_full * DIFF_TK + lax.broadcasted_iota(jnp.int32, (2 * DIFF_TQ, DIFF_TK), 1)
    _flash_step(logits(n_full) + jnp.where(kpos <= qpos, 0.0, NEG), values(n_full), m_ref, l_ref, acc_ref)

    res = _flash_result(l_ref, acc_ref)
    o = res[:DIFF_TQ] - lam_ref[0] * res[DIFF_TQ:]
    o_ref[...] = (_rms(o, g_ref[...]) * out_scale).astype(o_ref.dtype)


def _diff_attention(pc3, lam, subln_g, lam_init):
    b, s, _ = pc3.shape
    nq = C_QK // LANES
    return pl.pallas_call(
        functools.partial(_diff_kernel, out_scale=1.0 - lam_init),
        grid=(b, C_HEADS, s // DIFF_TQ),
        in_specs=[pl.BlockSpec(memory_space=pltpu.SMEM),
                  pl.BlockSpec((1, C_VDIM), lambda bi, h, i: (0, 0)),
                  pl.BlockSpec((None, DIFF_TQ, LANES), lambda bi, h, i: (bi, i, h)),
                  pl.BlockSpec((None, s, LANES), lambda bi, h, i: (bi, 0, nq + h)),
                  pl.BlockSpec((None, s, C_VDIM), lambda bi, h, i: (bi, 0, 2 * nq + h))],
        out_specs=pl.BlockSpec((None, DIFF_TQ, C_VDIM), lambda bi, h, i: (bi, i, h)),
        out_shape=jax.ShapeDtypeStruct((b, s, C_OUT), BF16),
        scratch_shapes=[pltpu.VMEM((2 * DIFF_TQ, LANES), F32), pltpu.VMEM((2 * DIFF_TQ, LANES), F32),
                        pltpu.VMEM((2 * DIFF_TQ, C_VDIM), F32)],
        compiler_params=pltpu.CompilerParams(dimension_semantics=("arbitrary", "arbitrary", "arbitrary"),
                                             vmem_limit_bytes=VMEM_LIMIT),
        name="diff_attn",
    )(lam, subln_g, pc3, pc3, pc3)


def _merge_kernel(x_ref, gpre_ref, gpost_ref, oa_ref, ob_ref, oc_ref, wg_ref, wa_ref, wb_ref, wc_ref, wo_ref,
                  out_ref, y_ref, h_ref):
    x = x_ref[...]
    xn = _rms(x, gpre_ref[...]).astype(BF16)
    branches = ((oa_ref, wa_ref), (ob_ref, wb_ref), (oc_ref, wc_ref))
    cw = 2 * LANES
    for c in range(0, D_MODEL, cw):
        y = jnp.zeros((x.shape[0], cw), F32)
        for j, (o_ref, w_ref) in enumerate(branches):
            logit = jnp.dot(xn, wg_ref[:, j * D_MODEL + c:j * D_MODEL + c + cw], preferred_element_type=F32)
            gate = 1.0 / (1.0 + jnp.exp(-logit))
            y = y + gate * jnp.dot(o_ref[...], w_ref[:, c:c + cw], preferred_element_type=F32)
        y_ref[:, c:c + cw] = y.astype(BF16)
    ssq = jnp.zeros((x.shape[0], 1), F32)
    for c in range(0, D_MODEL, cw):
        h = jnp.dot(y_ref[...], wo_ref[:, c:c + cw], preferred_element_type=F32)
        h_ref[:, c:c + cw] = h
        ssq = ssq + jnp.sum(h * h, axis=1, keepdims=True)
    inv = lax.rsqrt(ssq * (1.0 / D_MODEL) + NORM_EPS)
    out_ref[...] = x + h_ref[...] * inv * gpost_ref[...]


def _merge(x2, gpre, gpost, oa, ob, oc, w, tm=512):
    n = x2.shape[0]
    row = lambda width: pl.BlockSpec((tm, width), lambda i: (i, 0))
    full = lambda a: pl.BlockSpec(a.shape, lambda i: (0, 0))
    return pl.pallas_call(
        _merge_kernel,
        grid=(n // tm,),
        in_specs=[row(D_MODEL), full(gpre), full(gpost), row(A_OUT), row(B_OUT), row(C_OUT)] + [full(a) for a in w],
        out_specs=row(D_MODEL),
        out_shape=jax.ShapeDtypeStruct((n, D_MODEL), F32),
        scratch_shapes=[pltpu.VMEM((tm, D_MODEL), BF16), pltpu.VMEM((tm, D_MODEL), F32)],
        compiler_params=pltpu.CompilerParams(dimension_semantics=("arbitrary",), vmem_limit_bytes=VMEM_LIMIT),
        name="merge",
    )(x2, gpre, gpost, oa, ob, oc, *w)


FFN_CHUNK = 256
CARRY_ROWS = 8


def _ffn_kernel(x_ref, gpre_ref, gpost_ref, wup_ref, cw_ref, cb_ref, wdn_ref, out_ref, carry_ref, act_ref,
                *, tiles_per_seq):
    x = x_ref[...]
    tm = x.shape[0]
    xn = _rms(x, gpre_ref[...]).astype(BF16)
    first = pl.program_id(0) % tiles_per_seq == 0
    row = lax.broadcasted_iota(jnp.int32, (tm, 1), 0)

    def conv(col):
        h = jnp.dot(xn, wup_ref[:, col:col + FFN_CHUNK], preferred_element_type=F32)
        prev = jnp.where(first, 0.0, carry_ref[:, col:col + FFN_CHUNK])
        carry_ref[:, col:col + FFN_CHUNK] = h[tm - CARRY_ROWS:, :]
        p1 = prev[CARRY_ROWS - 1:CARRY_ROWS, :]
        p2 = prev[CARRY_ROWS - 2:CARRY_ROWS - 1, :]
        h1 = jnp.where(row == 0, p1, pltpu.roll(h, 1, 0))
        h2 = jnp.where(row == 0, p2, jnp.where(row == 1, p1, pltpu.roll(h, 2, 0)))
        w = cw_ref[:, col:col + FFN_CHUNK]
        return h2 * w[0:1, :] + h1 * w[1:2, :] + h * w[2:3, :] + cb_ref[:, col:col + FFN_CHUNK]

    for c in range(0, D_FF, FFN_CHUNK):
        g = conv(c)
        u = conv(D_FF + c)
        gelu = 0.5 * g * (1.0 + jnp.tanh(math.sqrt(2.0 / math.pi) * (g + 0.044715 * (g * g * g))))
        act_ref[:, c:c + FFN_CHUNK] = (gelu * u).astype(BF16)
    h = jnp.dot(act_ref[...], wdn_ref[...], preferred_element_type=F32)
    out_ref[...] = x + _rms(h, gpost_ref[...])


def _ffn(x2, gpre, gpost, wup, conv_w, conv_b, wdn, seq_len, tm=512):
    n = x2.shape[0]
    row = lambda width: pl.BlockSpec((tm, width), lambda i: (i, 0))
    full = lambda a: pl.BlockSpec(a.shape, lambda i: (0, 0))
    return pl.pallas_call(
        functools.partial(_ffn_kernel, tiles_per_seq=seq_len // tm),
        grid=(n // tm,),
        in_specs=[row(D_MODEL), full(gpre), full(gpost), full(wup), full(conv_w), full(conv_b), full(wdn)],
        out_specs=row(D_MODEL),
        out_shape=jax.ShapeDtypeStruct((n, D_MODEL), F32),
        scratch_shapes=[pltpu.VMEM((CARRY_ROWS, 2 * D_FF), F32), pltpu.VMEM((tm, D_FF), BF16)],
        compiler_params=pltpu.CompilerParams(dimension_semantics=("arbitrary",), vmem_limit_bytes=VMEM_LIMIT),
        name="conv_ffn",
    )(x2, gpre, gpost, wup, conv_w, conv_b, wdn)


def _rope_lane_tables(positions):
    half = ROPE_DIM // 2
    inv = ROPE_THETA ** (-jnp.arange(0, ROPE_DIM, 2, dtype=F32) / ROPE_DIM)
    ang = positions.astype(F32).reshape(-1, 1) * inv
    cos, sin = jnp.cos(ang), jnp.sin(ang)
    n = ang.shape[0]
    rest = HEAD_DIM - ROPE_DIM
    ones, zeros, zh = jnp.ones((n, rest), F32), jnp.zeros((n, rest), F32), jnp.zeros((n, half), F32)
    ct = jnp.concatenate([cos, cos, ones], axis=1)
    s1 = jnp.concatenate([-sin, zh, zeros], axis=1)
    s2 = jnp.concatenate([zh, sin, zeros], axis=1)
    return tuple(jnp.tile(t, (1, LANES // HEAD_DIM)) for t in (ct, s1, s2))


def _split_w_in(w_in):
    o_b, o_i, o_c, o_g = A_IN, A_IN + B_IN, A_IN + B_IN + IDX_IN, A_IN + B_IN + IDX_IN + C_IN
    wk = w_in[:, o_i + IDX_Q:o_i + IDX_Q + IDX_DIM]
    ww = w_in[:, o_i + IDX_Q + IDX_DIM:o_c]
    proj = (w_in[:, :o_b], w_in[:, o_b:o_i], w_in[:, o_i:o_i + IDX_Q],
            jnp.concatenate([wk, wk], axis=1),
            jnp.pad(ww, ((0, 0), (0, LANES - IDX_HEADS))),
            w_in[:, o_c:o_g])
    return tuple(a.astype(BF16) for a in proj), w_in[:, o_g:].astype(BF16)


def kernel(x, positions, w_in, w_br_a, w_br_b, w_br_c, w_out, lam_q1, lam_k1, lam_q2, lam_k2, subln_g,
           norm_mix_pre, norm_mix_post, norm_ffn_pre, norm_ffn_post, w_ffn_up, conv_w, conv_b, w_ffn_down):
    b, s, d = x.shape
    depth = w_in.shape[0]
    assert d == D_MODEL and s % A_TILE == 0 and s % SEL_CHUNK == 0 and s % DIFF_TK == 0
    tables = _rope_lane_tables(positions)
    x2 = x.reshape(b * s, d)
    vec = lambda a: a.reshape(1, -1)
    for layer in range(depth):
        lam_init = 0.8 - 0.6 * math.exp(-0.3 * layer)
        lam = (jnp.exp(jnp.sum(lam_q1[layer] * lam_k1[layer])) - jnp.exp(jnp.sum(lam_q2[layer] * lam_k2[layer]))
               + lam_init).reshape(1).astype(F32)
        w_proj, w_gate = _split_w_in(w_in[layer])
        pa, pb, qi, ki, wi, pc = _inproj(x2, vec(norm_mix_pre[layer]), tables, w_proj)
        seq = lambda a: a.reshape(b, s, a.shape[-1])
        oa = _dilated_attention(seq(pa))
        ob = _sparse_attention(seq(qi), seq(wi), seq(ki), seq(pb))
        oc = _diff_attention(seq(pc), lam, vec(subln_g[layer]), lam_init)
        flat = lambda a: a.reshape(b * s, a.shape[-1])
        w_merge = (w_gate, w_br_a[layer].astype(BF16), w_br_b[layer].astype(BF16), w_br_c[layer].astype(BF16),
                   w_out[layer].astype(BF16))
        x2 = _merge(x2, vec(norm_mix_pre[layer]), vec(norm_mix_post[layer]), flat(oa), flat(ob), flat(oc), w_merge)
        x2 = _ffn(x2, vec(norm_ffn_pre[layer]), vec(norm_ffn_post[layer]), w_ffn_up[layer].astype(BF16),
                  conv_w[layer], vec(conv_b[layer]), w_ffn_down[layer].astype(BF16), s)
    return x2.reshape(b, s, d)
```

```python
import functools
import math

import jax
import jax.numpy as jnp
from jax import lax
from jax.experimental import pallas as pl
from jax.experimental.pallas import tpu as pltpu

D_MODEL = 1024
HEAD_DIM = 64
ROPE_THETA = 500000.0
ROPE_DIM = HEAD_DIM // 4
NORM_EPS = 1e-6
Q_BLOCK = 128

A_PAIRS = ((128, 1), (512, 4), (2048, 16))
A_GROUPS = len(A_PAIRS)
A_HEADS = 4
A_OUT = A_HEADS * HEAD_DIM
B_HEADS = 4
B_OUT = B_HEADS * HEAD_DIM
IDX_HEADS = 8
IDX_DIM = 64
TOPK_MAX = 256
C_HEADS = 4
C_VDIM = 2 * HEAD_DIM
C_OUT = C_HEADS * C_VDIM
N_BRANCH = 3

A_IN = A_GROUPS * 3 * A_HEADS * HEAD_DIM
B_IN = 3 * B_HEADS * HEAD_DIM
IDX_Q = IDX_HEADS * IDX_DIM
IDX_IN = IDX_Q + IDX_DIM + IDX_HEADS
C_QK = C_HEADS * 2 * HEAD_DIM
C_IN = 2 * C_QK + C_OUT
GATE_IN = N_BRANCH * D_MODEL
D_FF = ((8 * D_MODEL // 3 + 127) // 128) * 128
CONV_WIDTH = 3

LANES = 128
SUBLANES = 8
SCALE = HEAD_DIM ** -0.5
NEG = -1e30
INT_MIN = -2 ** 31
VMEM_LIMIT = 56 * 1024 * 1024

F32 = jnp.float32
BF16 = jnp.bfloat16
NT_DIMS = (((1,), (1,)), ((), ()))


def _nt_dot(a, b):
    return lax.dot_general(a, b, NT_DIMS, preferred_element_type=F32)


def _rms(x, g):
    return x * lax.rsqrt(jnp.mean(x * x, axis=-1, keepdims=True) + NORM_EPS) * g


def _inproj_kernel(x_ref, g_ref, ct_ref, s1_ref, s2_ref, wa_ref, wb_ref, wqi_ref, wki_ref, wwi_ref, wc_ref,
                   pa_ref, pb_ref, qi_ref, ki_ref, wi_ref, pc_ref):
    xn = _rms(x_ref[...], g_ref[...]).astype(BF16)
    ct, s1, s2 = ct_ref[...], s1_ref[...], s2_ref[...]

    def rope(y):
        return y * ct + pltpu.roll(y, LANES - ROPE_DIM // 2, 1) * s1 + pltpu.roll(y, ROPE_DIM // 2, 1) * s2

    def project(w_ref, out_ref, n_cols, mode_of_col):
        for c in range(0, n_cols, 2 * LANES):
            y = jnp.dot(xn, w_ref[:, c:c + 2 * LANES], preferred_element_type=F32)
            for h in range(2):
                col = c + h * LANES
                z = y[:, h * LANES:(h + 1) * LANES]
                mode = mode_of_col(col)
                if mode != "v":
                    z = rope(z)
                if mode == "q":
                    z = z * SCALE
                out_ref[:, col:col + LANES] = z.astype(out_ref.dtype)

    qkv_mode = lambda width: (lambda col: ("q", "k", "v")[(col // width) % 3])
    project(wa_ref, pa_ref, A_IN, qkv_mode(A_OUT))
    project(wb_ref, pb_ref, B_IN, qkv_mode(B_OUT))
    project(wqi_ref, qi_ref, IDX_Q, lambda col: "k")
    project(wc_ref, pc_ref, C_IN, qkv_mode(C_QK))
    ki = jnp.dot(xn, wki_ref[...], preferred_element_type=F32)
    ki_ref[...] = rope(ki).astype(BF16)
    wi = jnp.dot(xn, wwi_ref[...], preferred_element_type=F32)
    wi_ref[...] = (wi * (IDX_HEADS ** -0.5)) * (IDX_DIM ** -0.5)


def _inproj(x2, gain, tables, w, tm=512):
    n = x2.shape[0]
    row = lambda width: pl.BlockSpec((tm, width), lambda i: (i, 0))
    full = lambda a: pl.BlockSpec(a.shape, lambda i: (0, 0))
    outs = [(A_IN, BF16), (B_IN, BF16), (IDX_Q, BF16), (LANES, BF16), (LANES, F32), (C_IN, BF16)]
    return pl.pallas_call(
        _inproj_kernel,
        grid=(n // tm,),
        in_specs=[row(D_MODEL), full(gain), row(LANES), row(LANES), row(LANES)] + [full(a) for a in w],
        out_specs=[row(wd) for wd, _ in outs],
        out_shape=[jax.ShapeDtypeStruct((n, wd), dt) for wd, dt in outs],
        compiler_params=pltpu.CompilerParams(dimension_semantics=("arbitrary",), vmem_limit_bytes=VMEM_LIMIT),
        name="inproj",
    )(x2, gain, *tables, *w)


def _head_masks(width):
    lane = lax.broadcasted_iota(jnp.int32, (1, width), 1)
    return [lane // HEAD_DIM == h for h in range(width // HEAD_DIM)]


def _stack_heads(q):
    zero = jnp.zeros_like(q)
    return jnp.concatenate([jnp.where(hm, q, zero) for hm in _head_masks(q.shape[1])], axis=0)


def _unstack_heads(x, rows):
    hmask = _head_masks(HEAD_DIM * (x.shape[0] // rows))
    out = jnp.where(hmask[0], x[0:rows], 0.0)
    for h in range(1, len(hmask)):
        out = jnp.where(hmask[h], x[h * rows:(h + 1) * rows], out)
    return out


def _lane_tile(x, width):
    reps = width // LANES
    return x if reps == 1 else jnp.concatenate([x] * reps, axis=1)


def _flash_init(m_ref, l_ref, acc_ref):
    m_ref[...] = jnp.full(m_ref.shape, NEG, F32)
    l_ref[...] = jnp.zeros(l_ref.shape, F32)
    acc_ref[...] = jnp.zeros(acc_ref.shape, F32)


def _flash_step(s, v, m_ref, l_ref, acc_ref):
    tk = s.shape[1]
    m_old = m_ref[...]
    m_new = jnp.maximum(m_old, jnp.max(s, axis=1, keepdims=True))
    p = jnp.exp(s - _lane_tile(m_new, tk))
    alpha = jnp.exp(m_old - m_new)
    psum = p[:, :LANES]
    for j in range(1, tk // LANES):
        psum = psum + p[:, j * LANES:(j + 1) * LANES]
    l_ref[...] = alpha * l_ref[...] + psum
    acc_ref[...] = (_lane_tile(alpha, acc_ref.shape[1]) * acc_ref[...]
                    + jnp.dot(p.astype(BF16), v, preferred_element_type=F32))
    m_ref[...] = m_new


def _flash_pipeline(n, logits_fn, values_fn, m_ref, l_ref, acc_ref):
    def body(t, s_cur):
        s_next = logits_fn(t + 1)
        _flash_step(s_cur, values_fn(t), m_ref, l_ref, acc_ref)
        return s_next
    s_last = lax.fori_loop(0, n - 1, body, logits_fn(0))
    _flash_step(s_last, values_fn(n - 1), m_ref, l_ref, acc_ref)


def _flash_result(l_ref, acc_ref):
    return acc_ref[...] / jnp.sum(l_ref[...], axis=1, keepdims=True)


A_TILE = max(w for w, _ in A_PAIRS)


def _dilated_kernel(*refs):
    group_refs = [refs[5 * g:5 * g + 5] for g in range(A_GROUPS)]
    o_ref, qs_ref, ks_ref, vs_ref, m_sc, l_sc, n_sc = refs[5 * A_GROUPS:]
    j = pl.program_id(1)
    qi = lax.broadcasted_iota(jnp.int32, (Q_BLOCK, 2 * Q_BLOCK), 0)
    kj = lax.broadcasted_iota(jnp.int32, (Q_BLOCK, 2 * Q_BLOCK), 1)
    rel = kj - qi
    band = (rel >= 0) & (rel <= Q_BLOCK)
    bias_band = jnp.where(band, 0.0, NEG)
    bias_first = jnp.where(band & (kj >= Q_BLOCK), 0.0, NEG)
    halves = A_OUT // LANES

    def get(ref, rows):
        return jnp.concatenate([ref[hh, rows, :] for hh in range(halves)], axis=1)

    def put(ref, rows, val):
        for hh in range(halves):
            ref[hh, rows, :] = val[:, hh * LANES:(hh + 1) * LANES]

    for g, ((window, dil), (q_ref, kp_ref, kc_ref, vp_ref, vc_ref)) in enumerate(zip(A_PAIRS, group_refs)):
        assert window == Q_BLOCK * dil
        put(qs_ref, slice(0, A_TILE), q_ref[...].astype(F32))
        put(ks_ref, slice(0, window), kp_ref[...].astype(F32))
        put(ks_ref, slice(window, window + A_TILE), kc_ref[...].astype(F32))
        put(vs_ref, slice(0, window), vp_ref[...].astype(F32))
        put(vs_ref, slice(window, window + A_TILE), vc_ref[...].astype(F32))
        shift = dil.bit_length() - 1

        def problem(p, carry, g=g, window=window, dil=dil, shift=shift):
            sub = p >> shift
            base = sub * window + (p & (dil - 1))
            stride = dil if dil > 1 else None
            q_rows = pl.ds(base, Q_BLOCK, stride=stride)
            k_rows = pl.ds(base, 2 * Q_BLOCK, stride=stride)
            qstack = _stack_heads(get(qs_ref, q_rows).astype(BF16))
            k = get(ks_ref, k_rows).astype(BF16)
            v = get(vs_ref, k_rows).astype(BF16)
            bias = jnp.where((sub > 0) | (j > 0), bias_band, bias_first)
            s = _nt_dot(qstack, k) + jnp.concatenate([bias] * A_HEADS, axis=0)
            m = jnp.max(s, axis=1, keepdims=True)
            e = jnp.exp(s - m)
            l = jnp.sum(e, axis=1, keepdims=True)
            num = jnp.dot(e.astype(BF16), v, preferred_element_type=F32)
            m_n, l_n, num_n = (_unstack_heads(t, Q_BLOCK) for t in (m, l, num))
            if g == 0:
                put(m_sc, q_rows, m_n)
                put(l_sc, q_rows, l_n)
                put(n_sc, q_rows, num_n)
            else:
                m_o = get(m_sc, q_rows)
                m_x = jnp.maximum(m_o, m_n)
                a, b = jnp.exp(m_o - m_x), jnp.exp(m_n - m_x)
                put(m_sc, q_rows, m_x)
                put(l_sc, q_rows, a * get(l_sc, q_rows) + b * l_n)
                put(n_sc, q_rows, a * get(n_sc, q_rows) + b * num_n)
            return carry

        lax.fori_loop(0, A_TILE // Q_BLOCK, problem, 0, unroll=4)

    all_rows = slice(0, A_TILE)
    o_ref[...] = (get(n_sc, all_rows) / get(l_sc, all_rows)).astype(o_ref.dtype)


def _dilated_attention(pa3):
    b, s, _ = pa3.shape
    in_specs = []
    for g, (window, _) in enumerate(A_PAIRS):
        per = A_TILE // window
        cur = lambda bi, j, col: (bi, j, col)
        prev = lambda bi, j, col, per=per: (bi, jnp.maximum(j * per - 1, 0), col)
        in_specs.append(pl.BlockSpec((None, A_TILE, A_OUT), functools.partial(cur, col=3 * g)))
        for col in (3 * g + 1, 3 * g + 2):
            in_specs.append(pl.BlockSpec((None, window, A_OUT), functools.partial(prev, col=col)))
            in_specs.append(pl.BlockSpec((None, A_TILE, A_OUT), functools.partial(cur, col=col)))
    stage = lambda rows: pltpu.VMEM((A_OUT // LANES, rows, LANES), F32)
    return pl.pallas_call(
        _dilated_kernel,
        grid=(b, s // A_TILE),
        in_specs=in_specs,
        out_specs=pl.BlockSpec((None, A_TILE, A_OUT), lambda bi, j: (bi, j, 0)),
        out_shape=jax.ShapeDtypeStruct((b, s, A_OUT), BF16),
        scratch_shapes=[stage(A_TILE), stage(2 * A_TILE), stage(2 * A_TILE),
                        stage(A_TILE), stage(A_TILE), stage(A_TILE)],
        compiler_params=pltpu.CompilerParams(dimension_semantics=("arbitrary", "arbitrary"),
                                             vmem_limit_bytes=VMEM_LIMIT),
        name="dilated_attn",
    )(*([pa3] * (5 * A_GROUPS)))


SEL_TQ = 128
SEL_CHUNK = 512
ATT_CHUNK = 512


def _sparse_kernel(qi_ref, wi_ref, ki_ref, q_ref, k_ref, v_ref, o_ref, key_ref, keyt_ref, m_ref, l_ref, acc_ref,
                   *, topk):
    i = pl.program_id(1)
    qs = i * SEL_TQ
    nch = (qs + SEL_TQ - 1) // SEL_CHUNK + 1
    nat = (qs + SEL_TQ - 1) // ATT_CHUNK + 1
    lane = lax.broadcasted_iota(jnp.int32, (1, LANES), 1)
    qpos = qs + lax.broadcasted_iota(jnp.int32, (SEL_TQ, 1), 0)

    qi = qi_ref[...]
    stacked = []
    for h in range(IDX_HEADS):
        pair = qi[:, (h // 2) * LANES:(h // 2 + 1) * LANES]
        keep = (lane < IDX_DIM) if h % 2 == 0 else (lane >= IDX_DIM)
        stacked.append(jnp.where(keep, pair, jnp.zeros_like(pair)))
    stacked = jnp.concatenate(stacked, axis=0)
    w = wi_ref[...]
    wcol = [w[:, h:h + 1] for h in range(IDX_HEADS)]

    def to_key(score):
        bits = lax.bitcast_convert_type(score, jnp.int32)
        return bits ^ ((bits >> 31) & 0x7FFFFFFF)

    def chunk_logits(c):
        start = pl.multiple_of(c * SEL_CHUNK, SEL_CHUNK)
        return _nt_dot(stacked, ki_ref[pl.ds(start, SEL_CHUNK), :])

    def chunk_scores(c, logits, group_max):
        start = pl.multiple_of(c * SEL_CHUNK, SEL_CHUNK)
        score = jnp.zeros((SEL_TQ, SEL_CHUNK), F32)
        for h in range(IDX_HEADS):
            score = score + jnp.maximum(logits[h * SEL_TQ:(h + 1) * SEL_TQ], 0.0) * wcol[h]
        kpos = start + lax.broadcasted_iota(jnp.int32, (1, SEL_CHUNK), 1)
        causal = kpos <= qpos
        keys = jnp.where(causal, to_key(score), INT_MIN)
        key_ref[:, pl.ds(start, SEL_CHUNK)] = keys
        keyt_ref[pl.ds(start, SEL_CHUNK), :] = keys.T
        score = jnp.where(causal, score, -jnp.inf)
        group_max = list(group_max)
        for jj in range(SEL_CHUNK // LANES):
            group_max[jj % 2] = jnp.maximum(group_max[jj % 2], score[:, jj * LANES:(jj + 1) * LANES])
        return tuple(group_max)

    neg_inf = jnp.full((SEL_TQ, LANES), -jnp.inf, F32)
    gm_even, gm_odd = lax.fori_loop(0, nch, lambda c, gm: chunk_scores(c, chunk_logits(c), gm), (neg_inf, neg_inf))

    def to_lanes(col):
        return jnp.broadcast_to(col, (SEL_TQ, LANES)).T[0:1, :]

    def to_rows(lanes):
        return jnp.broadcast_to(lanes, (LANES, SEL_TQ)).T[:, 0:1]

    def count(compare, level):
        lanes_acc = 8 * SUBLANES
        def body(c, acc):
            blk = keyt_ref[pl.ds(pl.multiple_of(c * SEL_CHUNK, SEL_CHUNK), SEL_CHUNK), :]
            hit = jnp.where(compare(blk, level), 1.0, 0.0)
            return acc + jnp.sum(hit.reshape(SEL_CHUNK // lanes_acc, lanes_acc, LANES), axis=0)
        acc = lax.fori_loop(0, nch, body, jnp.zeros((lanes_acc, LANES), F32))
        return jnp.sum(acc, axis=0, keepdims=True)

    kf = float(topk)
    assert topk <= 2 * LANES and SEL_TQ == LANES
    searchable = qs >= topk
    lo_f = to_lanes(jnp.minimum(jnp.min(gm_even, axis=1, keepdims=True), jnp.min(gm_odd, axis=1, keepdims=True)))
    hi_f = to_lanes(jnp.maximum(jnp.max(gm_even, axis=1, keepdims=True), jnp.max(gm_odd, axis=1, keepdims=True)))
    lo0 = jnp.where(searchable, to_key(lo_f), INT_MIN + 1)
    hi0 = to_key(hi_f) + 1
    cnt_lo0 = jnp.where(searchable, kf + 1.0, kf) + jnp.zeros((1, LANES), F32)

    def candidate(lo, hi, cnt_lo):
        mid = (lo >> 1) + (hi >> 1) + (lo & hi & 1)
        open_q = (cnt_lo > kf) & (mid > lo)
        cand = jnp.where((lo < 0) & (hi > 0), 0, jnp.where((lo == 0) & (hi > 1), 1, mid))
        open_q = jnp.where(open_q, 1.0, 0.0)
        return cand, open_q, jnp.max(open_q) > 0.0

    def search_step(carry):
        lo, hi, cnt_lo, cand, open_q, _ = carry
        cnt = count(lambda blk, level: blk >= level, cand)
        up = (open_q > 0.0) & (cnt >= kf)
        down = (open_q > 0.0) & (cnt < kf)
        lo, cnt_lo, hi = jnp.where(up, cand, lo), jnp.where(up, cnt, cnt_lo), jnp.where(down, cand, hi)
        return (lo, hi, cnt_lo) + candidate(lo, hi, cnt_lo)

    thr_q, _, cnt_thr = lax.while_loop(lambda carry: carry[5], lambda carry: search_step(search_step(carry)),
                                       (lo0, hi0, cnt_lo0) + candidate(lo0, hi0, cnt_lo0))[:3]
    has_ties = jnp.max(cnt_thr) > kf
    thr = to_rows(thr_q)

    @pl.when(has_ties)
    def _():
        need = to_rows(kf - count(lambda blk, level: blk > level, thr_q))
        r = lax.broadcasted_iota(jnp.int32, (ATT_CHUNK, ATT_CHUNK), 0)
        cc = lax.broadcasted_iota(jnp.int32, (ATT_CHUNK, ATT_CHUNK), 1)
        before = jnp.where(r < cc, 1.0, 0.0).astype(BF16)

        def body(c, seen):
            cols = pl.ds(pl.multiple_of(c * ATT_CHUNK, ATT_CHUNK), ATT_CHUNK)
            blk = key_ref[:, cols]
            eq = blk == thr
            eqf = jnp.where(eq, 1.0, 0.0)
            rank = seen + jnp.dot(eqf.astype(BF16), before, preferred_element_type=F32)
            key_ref[:, cols] = jnp.where(eq & (rank >= need), INT_MIN, blk)
            return seen + jnp.sum(eqf, axis=1, keepdims=True)
        lax.fori_loop(0, nat, body, jnp.zeros((SEL_TQ, 1), F32))

    qstack = _stack_heads(q_ref[...])

    def logits(c, sel):
        start = pl.multiple_of(c * ATT_CHUNK, ATT_CHUNK)
        bias = jnp.where(sel, 0.0, NEG)
        return _nt_dot(qstack, k_ref[pl.ds(start, ATT_CHUNK), :]) + jnp.concatenate([bias] * B_HEADS, axis=0)

    def values(c):
        return v_ref[pl.ds(pl.multiple_of(c * ATT_CHUNK, ATT_CHUNK), ATT_CHUNK), :]

    def selected(c):
        return key_ref[:, pl.ds(pl.multiple_of(c * ATT_CHUNK, ATT_CHUNK), ATT_CHUNK)] >= thr

    _flash_init(m_ref, l_ref, acc_ref)
    _flash_pipeline(nat, lambda c: logits(c, selected(c)), values, m_ref, l_ref, acc_ref)

    o_ref[...] = _unstack_heads(_flash_result(l_ref, acc_ref), SEL_TQ).astype(o_ref.dtype)


def _sparse_attention(qi3, wi3, ki3, pb3):
    b, s, _ = pb3.shape
    topk = min(TOPK_MAX, s // 4)
    rows = B_HEADS * SEL_TQ
    qblk = lambda width, col: pl.BlockSpec((None, SEL_TQ, width), lambda bi, i: (bi, i, col))
    seq = lambda width, col: pl.BlockSpec((None, s, width), lambda bi, i: (bi, 0, col))
    return pl.pallas_call(
        functools.partial(_sparse_kernel, topk=topk),
        grid=(b, s // SEL_TQ),
        in_specs=[qblk(IDX_Q, 0), qblk(LANES, 0), seq(LANES, 0), qblk(B_OUT, 0), seq(B_OUT, 1), seq(B_OUT, 2)],
        out_specs=pl.BlockSpec((None, SEL_TQ, B_OUT), lambda bi, i: (bi, i, 0)),
        out_shape=jax.ShapeDtypeStruct((b, s, B_OUT), BF16),
        scratch_shapes=[pltpu.VMEM((SEL_TQ, s), jnp.int32), pltpu.VMEM((s, SEL_TQ), jnp.int32),
                        pltpu.VMEM((rows, LANES), F32), pltpu.VMEM((rows, LANES), F32),
                        pltpu.VMEM((rows, B_OUT), F32)],
        compiler_params=pltpu.CompilerParams(dimension_semantics=("arbitrary", "arbitrary"),
                                             vmem_limit_bytes=VMEM_LIMIT),
        name="sparse_attn",
    )(qi3, wi3, ki3, pb3, pb3, pb3)


DIFF_TQ = 512
DIFF_TK = 1024


def _diff_kernel(lam_ref, g_ref, q_ref, k_ref, v_ref, o_ref, m_ref, l_ref, acc_ref, *, out_scale):
    i = pl.program_id(2)
    qstack = _stack_heads(q_ref[...])
    _flash_init(m_ref, l_ref, acc_ref)

    def logits(kb):
        return _nt_dot(qstack, k_ref[pl.ds(pl.multiple_of(kb * DIFF_TK, DIFF_TK), DIFF_TK), :])

    def values(kb):
        return v_ref[pl.ds(pl.multiple_of(kb * DIFF_TK, DIFF_TK), DIFF_TK), :]

    n_full = (i * DIFF_TQ) // DIFF_TK

    def full_block(kb, carry):
        _flash_step(logits(kb), values(kb), m_ref, l_ref, acc_ref)
        return carry

    lax.fori_loop(0, n_full, full_block, 0)
    qpos = i * DIFF_TQ + (lax.broadcasted_iota(jnp.int32, (2 * DIFF_TQ, DIFF_TK), 0) & (DIFF_TQ - 1))
    kpos = n_full * DIFF_TK + lax.broadcasted_iota(jnp.int32, (2 * DIFF_TQ, DIFF_TK), 1)
    _flash_step(logits(n_full) + jnp.where(kpos <= qpos, 0.0, NEG), values(n_full), m_ref, l_ref, acc_ref)

    res = _flash_result(l_ref, acc_ref)
    o = res[:DIFF_TQ] - lam_ref[0] * res[DIFF_TQ:]
    o_ref[...] = (_rms(o, g_ref[...]) * out_scale).astype(o_ref.dtype)


def _diff_attention(pc3, lam, subln_g, lam_init):
    b, s, _ = pc3.shape
    nq = C_QK // LANES
    return pl.pallas_call(
        functools.partial(_diff_kernel, out_scale=1.0 - lam_init),
        grid=(b, C_HEADS, s // DIFF_TQ),
        in_specs=[pl.BlockSpec(memory_space=pltpu.SMEM),
                  pl.BlockSpec((1, C_VDIM), lambda bi, h, i: (0, 0)),
                  pl.BlockSpec((None, DIFF_TQ, LANES), lambda bi, h, i: (bi, i, h)),
                  pl.BlockSpec((None, s, LANES), lambda bi, h, i: (bi, 0, nq + h)),
                  pl.BlockSpec((None, s, C_VDIM), lambda bi, h, i: (bi, 0, 2 * nq + h))],
        out_specs=pl.BlockSpec((None, DIFF_TQ, C_VDIM), lambda bi, h, i: (bi, i, h)),
        out_shape=jax.ShapeDtypeStruct((b, s, C_OUT), BF16),
        scratch_shapes=[pltpu.VMEM((2 * DIFF_TQ, LANES), F32), pltpu.VMEM((2 * DIFF_TQ, LANES), F32),
                        pltpu.VMEM((2 * DIFF_TQ, C_VDIM), F32)],
        compiler_params=pltpu.CompilerParams(dimension_semantics=("arbitrary", "arbitrary", "arbitrary"),
                                             vmem_limit_bytes=VMEM_LIMIT),
        name="diff_attn",
    )(lam, subln_g, pc3, pc3, pc3)


def _merge_kernel(x_ref, gpre_ref, gpost_ref, oa_ref, ob_ref, oc_ref, wg_ref, wa_ref, wb_ref, wc_ref, wo_ref,
                  out_ref, y_ref, h_ref):
    x = x_ref[...]
    xn = _rms(x, gpre_ref[...]).astype(BF16)
    branches = ((oa_ref, wa_ref), (ob_ref, wb_ref), (oc_ref, wc_ref))
    cw = 2 * LANES
    for c in range(0, D_MODEL, cw):
        y = jnp.zeros((x.shape[0], cw), F32)
        for j, (o_ref, w_ref) in enumerate(branches):
            logit = jnp.dot(xn, wg_ref[:, j * D_MODEL + c:j * D_MODEL + c + cw], preferred_element_type=F32)
            gate = 1.0 / (1.0 + jnp.exp(-logit))
            y = y + gate * jnp.dot(o_ref[...], w_ref[:, c:c + cw], preferred_element_type=F32)
        y_ref[:, c:c + cw] = y.astype(BF16)
    ssq = jnp.zeros((x.shape[0], 1), F32)
    for c in range(0, D_MODEL, cw):
        h = jnp.dot(y_ref[...], wo_ref[:, c:c + cw], preferred_element_type=F32)
        h_ref[:, c:c + cw] = h
        ssq = ssq + jnp.sum(h * h, axis=1, keepdims=True)
    inv = lax.rsqrt(ssq * (1.0 / D_MODEL) + NORM_EPS)
    out_ref[...] = x + h_ref[...] * inv * gpost_ref[...]


def _merge(x2, gpre, gpost, oa, ob, oc, w, tm=512):
    n = x2.shape[0]
    row = lambda width: pl.BlockSpec((tm, width), lambda i: (i, 0))
    full = lambda a: pl.BlockSpec(a.shape, lambda i: (0, 0))
    return pl.pallas_call(
        _merge_kernel,
        grid=(n // tm,),
        in_specs=[row(D_MODEL), full(gpre), full(gpost), row(A_OUT), row(B_OUT), row(C_OUT)] + [full(a) for a in w],
        out_specs=row(D_MODEL),
        out_shape=jax.ShapeDtypeStruct((n, D_MODEL), F32),
        scratch_shapes=[pltpu.VMEM((tm, D_MODEL), BF16), pltpu.VMEM((tm, D_MODEL), F32)],
        compiler_params=pltpu.CompilerParams(dimension_semantics=("arbitrary",), vmem_limit_bytes=VMEM_LIMIT),
        name="merge",
    )(x2, gpre, gpost, oa, ob, oc, *w)


FFN_CHUNK = 256
CARRY_ROWS = 8


def _ffn_kernel(x_ref, gpre_ref, gpost_ref, wup_ref, cw_ref, cb_ref, wdn_ref, out_ref, carry_ref, act_ref,
                *, tiles_per_seq):
    x = x_ref[...]
    tm = x.shape[0]
    xn = _rms(x, gpre_ref[...]).astype(BF16)
    first = pl.program_id(0) % tiles_per_seq == 0
    row = lax.broadcasted_iota(jnp.int32, (tm, 1), 0)

    def conv(col):
        h = jnp.dot(xn, wup_ref[:, col:col + FFN_CHUNK], preferred_element_type=F32)
        prev = jnp.where(first, 0.0, carry_ref[:, col:col + FFN_CHUNK])
        carry_ref[:, col:col + FFN_CHUNK] = h[tm - CARRY_ROWS:, :]
        p1 = prev[CARRY_ROWS - 1:CARRY_ROWS, :]
        p2 = prev[CARRY_ROWS - 2:CARRY_ROWS - 1, :]
        h1 = jnp.where(row == 0, p1, pltpu.roll(h, 1, 0))
        h2 = jnp.where(row == 0, p2, jnp.where(row == 1, p1, pltpu.roll(h, 2, 0)))
        w = cw_ref[:, col:col + FFN_CHUNK]
        return h2 * w[0:1, :] + h1 * w[1:2, :] + h * w[2:3, :] + cb_ref[:, col:col + FFN_CHUNK]

    for c in range(0, D_FF, FFN_CHUNK):
        g = conv(c)
        u = conv(D_FF + c)
        gelu = 0.5 * g * (1.0 + jnp.tanh(math.sqrt(2.0 / math.pi) * (g + 0.044715 * (g * g * g))))
        act_ref[:, c:c + FFN_CHUNK] = (gelu * u).astype(BF16)
    h = jnp.dot(act_ref[...], wdn_ref[...], preferred_element_type=F32)
    out_ref[...] = x + _rms(h, gpost_ref[...])


def _ffn(x2, gpre, gpost, wup, conv_w, conv_b, wdn, seq_len, tm=512):
    n = x2.shape[0]
    row = lambda width: pl.BlockSpec((tm, width), lambda i: (i, 0))
    full = lambda a: pl.BlockSpec(a.shape, lambda i: (0, 0))
    return pl.pallas_call(
        functools.partial(_ffn_kernel, tiles_per_seq=seq_len // tm),
        grid=(n // tm,),
        in_specs=[row(D_MODEL), full(gpre), full(gpost), full(wup), full(conv_w), full(conv_b), full(wdn)],
        out_specs=row(D_MODEL),
        out_shape=jax.ShapeDtypeStruct((n, D_MODEL), F32),
        scratch_shapes=[pltpu.VMEM((CARRY_ROWS, 2 * D_FF), F32), pltpu.VMEM((tm, D_FF), BF16)],
        compiler_params=pltpu.CompilerParams(dimension_semantics=("arbitrary",), vmem_limit_bytes=VMEM_LIMIT),
        name="conv_ffn",
    )(x2, gpre, gpost, wup, conv_w, conv_b, wdn)


def _rope_lane_tables(positions):
    half = ROPE_DIM // 2
    inv = ROPE_THETA ** (-jnp.arange(0, ROPE_DIM, 2, dtype=F32) / ROPE_DIM)
    ang = positions.astype(F32).reshape(-1, 1) * inv
    cos, sin = jnp.cos(ang), jnp.sin(ang)
    n = ang.shape[0]
    rest = HEAD_DIM - ROPE_DIM
    ones, zeros, zh = jnp.ones((n, rest), F32), jnp.zeros((n, rest), F32), jnp.zeros((n, half), F32)
    ct = jnp.concatenate([cos, cos, ones], axis=1)
    s1 = jnp.concatenate([-sin, zh, zeros], axis=1)
    s2 = jnp.concatenate([zh, sin, zeros], axis=1)
    return tuple(jnp.tile(t, (1, LANES // HEAD_DIM)) for t in (ct, s1, s2))


def _split_w_in(w_in):
    o_b, o_i, o_c, o_g = A_IN, A_IN + B_IN, A_IN + B_IN + IDX_IN, A_IN + B_IN + IDX_IN + C_IN
    wk = w_in[:, o_i + IDX_Q:o_i + IDX_Q + IDX_DIM]
    ww = w_in[:, o_i + IDX_Q + IDX_DIM:o_c]
    proj = (w_in[:, :o_b], w_in[:, o_b:o_i], w_in[:, o_i:o_i + IDX_Q],
            jnp.concatenate([wk, wk], axis=1),
            jnp.pad(ww, ((0, 0), (0, LANES - IDX_HEADS))),
            w_in[:, o_c:o_g])
    return tuple(a.astype(BF16) for a in proj), w_in[:, o_g:].astype(BF16)


def kernel(x, positions, w_in, w_br_a, w_br_b, w_br_c, w_out, lam_q1, lam_k1, lam_q2, lam_k2, subln_g,
           norm_mix_pre, norm_mix_post, norm_ffn_pre, norm_ffn_post, w_ffn_up, conv_w, conv_b, w_ffn_down):
    b, s, d = x.shape
    depth = w_in.shape[0]
    assert d == D_MODEL and s % A_TILE == 0 and s % SEL_CHUNK == 0 and s % DIFF_TK == 0
    tables = _rope_lane_tables(positions)
    x2 = x.reshape(b * s, d)
    vec = lambda a: a.reshape(1, -1)
    for layer in range(depth):
        lam_init = 0.8 - 0.6 * math.exp(-0.3 * layer)
        lam = (jnp.exp(jnp.sum(lam_q1[layer] * lam_k1[layer])) - jnp.exp(jnp.sum(lam_q2[layer] * lam_k2[layer]))
               + lam_init).reshape(1).astype(F32)
        w_proj, w_gate = _split_w_in(w_in[layer])
        pa, pb, qi, ki, wi, pc = _inproj(x2, vec(norm_mix_pre[layer]), tables, w_proj)
        seq = lambda a: a.reshape(b, s, a.shape[-1])
        oa = _dilated_attention(seq(pa))
        ob = _sparse_attention(seq(qi), seq(wi), seq(ki), seq(pb))
        oc = _diff_attention(seq(pc), lam, vec(subln_g[layer]), lam_init)
        flat = lambda a: a.reshape(b * s, a.shape[-1])
        w_merge = (w_gate, w_br_a[layer].astype(BF16), w_br_b[layer].astype(BF16), w_br_c[layer].astype(BF16),
                   w_out[layer].astype(BF16))
        x2 = _merge(x2, vec(norm_mix_pre[layer]), vec(norm_mix_post[layer]), flat(oa), flat(ob), flat(oc), w_merge)
        x2 = _ffn(x2, vec(norm_ffn_pre[layer]), vec(norm_ffn_post[layer]), w_ffn_up[layer].astype(BF16),
                  conv_w[layer], vec(conv_b[layer]), w_ffn_down[layer].astype(BF16), s)
    return x2.reshape(b, s, d)
```

```python
import functools
import math

import jax
import jax.numpy as jnp
from jax import lax
from jax.experimental import pallas as pl
from jax.experimental.pallas import tpu as pltpu

D_MODEL = 1024
HEAD_DIM = 64
ROPE_THETA = 500000.0
ROPE_DIM = HEAD_DIM // 4
NORM_EPS = 1e-6
Q_BLOCK = 128

A_PAIRS = ((128, 1), (512, 4), (2048, 16))
A_GROUPS = len(A_PAIRS)
A_HEADS = 4
A_OUT = A_HEADS * HEAD_DIM
B_HEADS = 4
B_OUT = B_HEADS * HEAD_DIM
IDX_HEADS = 8
IDX_DIM = 64
TOPK_MAX = 256
C_HEADS = 4
C_VDIM = 2 * HEAD_DIM
C_OUT = C_HEADS * C_VDIM
N_BRANCH = 3

A_IN = A_GROUPS * 3 * A_HEADS * HEAD_DIM
B_IN = 3 * B_HEADS * HEAD_DIM
IDX_Q = IDX_HEADS * IDX_DIM
IDX_IN = IDX_Q + IDX_DIM + IDX_HEADS
C_QK = C_HEADS * 2 * HEAD_DIM
C_IN = 2 * C_QK + C_OUT
GATE_IN = N_BRANCH * D_MODEL
D_FF = ((8 * D_MODEL // 3 + 127) // 128) * 128
CONV_WIDTH = 3

LANES = 128
SUBLANES = 8
SCALE = HEAD_DIM ** -0.5
NEG = -1e30
INT_MIN = -2 ** 31
VMEM_LIMIT = 56 * 1024 * 1024

F32 = jnp.float32
BF16 = jnp.bfloat16
NT_DIMS = (((1,), (1,)), ((), ()))


def _nt_dot(a, b):
    return lax.dot_general(a, b, NT_DIMS, preferred_element_type=F32)


def _rms(x, g):
    return x * lax.rsqrt(jnp.mean(x * x, axis=-1, keepdims=True) + NORM_EPS) * g


def _inproj_kernel(x_ref, g_ref, ct_ref, s1_ref, s2_ref, wa_ref, wb_ref, wqi_ref, wki_ref, wwi_ref, wc_ref,
                   pa_ref, pb_ref, qi_ref, ki_ref, wi_ref, pc_ref):
    xn = _rms(x_ref[...], g_ref[...]).astype(BF16)
    ct, s1, s2 = ct_ref[...], s1_ref[...], s2_ref[...]

    def rope(y):
        return y * ct + pltpu.roll(y, LANES - ROPE_DIM // 2, 1) * s1 + pltpu.roll(y, ROPE_DIM // 2, 1) * s2

    def project(w_ref, out_ref, n_cols, mode_of_col):
        for c in range(0, n_cols, 2 * LANES):
            y = jnp.dot(xn, w_ref[:, c:c + 2 * LANES], preferred_element_type=F32)
            for h in range(2):
                col = c + h * LANES
                z = y[:, h * LANES:(h + 1) * LANES]
                mode = mode_of_col(col)
                if mode != "v":
                    z = rope(z)
                if mode == "q":
                    z = z * SCALE
                out_ref[:, col:col + LANES] = z.astype(out_ref.dtype)

    qkv_mode = lambda width: (lambda col: ("q", "k", "v")[(col // width) % 3])
    project(wa_ref, pa_ref, A_IN, qkv_mode(A_OUT))
    project(wb_ref, pb_ref, B_IN, qkv_mode(B_OUT))
    project(wqi_ref, qi_ref, IDX_Q, lambda col: "k")
    project(wc_ref, pc_ref, C_IN, qkv_mode(C_QK))
    ki = jnp.dot(xn, wki_ref[...], preferred_element_type=F32)
    ki_ref[...] = rope(ki).astype(BF16)
    wi = jnp.dot(xn, wwi_ref[...], preferred_element_type=F32)
    wi_ref[...] = (wi * (IDX_HEADS ** -0.5)) * (IDX_DIM ** -0.5)


def _inproj(x2, gain, tables, w, tm=512):
    n = x2.shape[0]
    row = lambda width: pl.BlockSpec((tm, width), lambda i: (i, 0))
    full = lambda a: pl.BlockSpec(a.shape, lambda i: (0, 0))
    outs = [(A_IN, BF16), (B_IN, BF16), (IDX_Q, BF16), (LANES, BF16), (LANES, F32), (C_IN, BF16)]
    return pl.pallas_call(
        _inproj_kernel,
        grid=(n // tm,),
        in_specs=[row(D_MODEL), full(gain), row(LANES), row(LANES), row(LANES)] + [full(a) for a in w],
        out_specs=[row(wd) for wd, _ in outs],
        out_shape=[jax.ShapeDtypeStruct((n, wd), dt) for wd, dt in outs],
        compiler_params=pltpu.CompilerParams(dimension_semantics=("arbitrary",), vmem_limit_bytes=VMEM_LIMIT),
        name="inproj",
    )(x2, gain, *tables, *w)


def _head_masks(width):
    lane = lax.broadcasted_iota(jnp.int32, (1, width), 1)
    return [lane // HEAD_DIM == h for h in range(width // HEAD_DIM)]


def _stack_heads(q):
    zero = jnp.zeros_like(q)
    return jnp.concatenate([jnp.where(hm, q, zero) for hm in _head_masks(q.shape[1])], axis=0)


def _unstack_heads(x, rows):
    hmask = _head_masks(HEAD_DIM * (x.shape[0] // rows))
    out = jnp.where(hmask[0], x[0:rows], 0.0)
    for h in range(1, len(hmask)):
        out = jnp.where(hmask[h], x[h * rows:(h + 1) * rows], out)
    return out


def _lane_tile(x, width):
    reps = width // LANES
    return x if reps == 1 else jnp.concatenate([x] * reps, axis=1)


def _flash_init(m_ref, l_ref, acc_ref):
    m_ref[...] = jnp.full(m_ref.shape, NEG, F32)
    l_ref[...] = jnp.zeros(l_ref.shape, F32)
    acc_ref[...] = jnp.zeros(acc_ref.shape, F32)


def _flash_step(s, v, m_ref, l_ref, acc_ref):
    tk = s.shape[1]
    m_old = m_ref[...]
    m_new = jnp.maximum(m_old, jnp.max(s, axis=1, keepdims=True))
    p = jnp.exp(s - _lane_tile(m_new, tk))
    alpha = jnp.exp(m_old - m_new)
    psum = p[:, :LANES]
    for j in range(1, tk // LANES):
        psum = psum + p[:, j * LANES:(j + 1) * LANES]
    l_ref[...] = alpha * l_ref[...] + psum
    acc_ref[...] = (_lane_tile(alpha, acc_ref.shape[1]) * acc_ref[...]
                    + jnp.dot(p.astype(BF16), v, preferred_element_type=F32))
    m_ref[...] = m_new


def _flash_pipeline(n, logits_fn, values_fn, m_ref, l_ref, acc_ref):
    def body(t, s_cur):
        s_next = logits_fn(t + 1)
        _flash_step(s_cur, values_fn(t), m_ref, l_ref, acc_ref)
        return s_next
    s_last = lax.fori_loop(0, n - 1, body, logits_fn(0))
    _flash_step(s_last, values_fn(n - 1), m_ref, l_ref, acc_ref)


def _flash_result(l_ref, acc_ref):
    return acc_ref[...] / jnp.sum(l_ref[...], axis=1, keepdims=True)


A_TILE = max(w for w, _ in A_PAIRS)


def _dilated_kernel(*refs):
    group_refs = [refs[5 * g:5 * g + 5] for g in range(A_GROUPS)]
    o_ref, qs_ref, ks_ref, vs_ref, m_sc, l_sc, n_sc = refs[5 * A_GROUPS:]
    j = pl.program_id(1)
    qi = lax.broadcasted_iota(jnp.int32, (Q_BLOCK, 2 * Q_BLOCK), 0)
    kj = lax.broadcasted_iota(jnp.int32, (Q_BLOCK, 2 * Q_BLOCK), 1)
    rel = kj - qi
    band = (rel >= 0) & (rel <= Q_BLOCK)
    bias_band = jnp.where(band, 0.0, NEG)
    bias_first = jnp.where(band & (kj >= Q_BLOCK), 0.0, NEG)
    halves = A_OUT // LANES

    def get(ref, rows):
        return jnp.concatenate([ref[hh, rows, :] for hh in range(halves)], axis=1)

    def put(ref, rows, val):
        for hh in range(halves):
            ref[hh, rows, :] = val[:, hh * LANES:(hh + 1) * LANES]

    for g, ((window, dil), (q_ref, kp_ref, kc_ref, vp_ref, vc_ref)) in enumerate(zip(A_PAIRS, group_refs)):
        assert window == Q_BLOCK * dil
        put(qs_ref, slice(0, A_TILE), q_ref[...].astype(F32))
        put(ks_ref, slice(0, window), kp_ref[...].astype(F32))
        put(ks_ref, slice(window, window + A_TILE), kc_ref[...].astype(F32))
        put(vs_ref, slice(0, window), vp_ref[...].astype(F32))
        put(vs_ref, slice(window, window + A_TILE), vc_ref[...].astype(F32))
        shift = dil.bit_length() - 1

        def problem(p, carry, g=g, window=window, dil=dil, shift=shift):
            sub = p >> shift
            base = sub * window + (p & (dil - 1))
            stride = dil if dil > 1 else None
            q_rows = pl.ds(base, Q_BLOCK, stride=stride)
            k_rows = pl.ds(base, 2 * Q_BLOCK, stride=stride)
            qstack = _stack_heads(get(qs_ref, q_rows).astype(BF16))
            k = get(ks_ref, k_rows).astype(BF16)
            v = get(vs_ref, k_rows).astype(BF16)
            bias = jnp.where((sub > 0) | (j > 0), bias_band, bias_first)
            s = _nt_dot(qstack, k) + jnp.concatenate([bias] * A_HEADS, axis=0)
            m = jnp.max(s, axis=1, keepdims=True)
            e = jnp.exp(s - m)
            l = jnp.sum(e, axis=1, keepdims=True)
            num = jnp.dot(e.astype(BF16), v, preferred_element_type=F32)
            m_n, l_n, num_n = (_unstack_heads(t, Q_BLOCK) for t in (m, l, num))
            if g == 0:
                put(m_sc, q_rows, m_n)
                put(l_sc, q_rows, l_n)
                put(n_sc, q_rows, num_n)
            else:
                m_o = get(m_sc, q_rows)
                m_x = jnp.maximum(m_o, m_n)
                a, b = jnp.exp(m_o - m_x), jnp.exp(m_n - m_x)
                put(m_sc, q_rows, m_x)
                put(l_sc, q_rows, a * get(l_sc, q_rows) + b * l_n)
                put(n_sc, q_rows, a * get(n_sc, q_rows) + b * num_n)
            return carry

        lax.fori_loop(0, A_TILE // Q_BLOCK, problem, 0, unroll=4)

    all_rows = slice(0, A_TILE)
    o_ref[...] = (get(n_sc, all_rows) / get(l_sc, all_rows)).astype(o_ref.dtype)


def _dilated_attention(pa3):
    b, s, _ = pa3.shape
    in_specs = []
    for g, (window, _) in enumerate(A_PAIRS):
        per = A_TILE // window
        cur = lambda bi, j, col: (bi, j, col)
        prev = lambda bi, j, col, per=per: (bi, jnp.maximum(j * per - 1, 0), col)
        in_specs.append(pl.BlockSpec((None, A_TILE, A_OUT), functools.partial(cur, col=3 * g)))
        for col in (3 * g + 1, 3 * g + 2):
            in_specs.append(pl.BlockSpec((None, window, A_OUT), functools.partial(prev, col=col)))
            in_specs.append(pl.BlockSpec((None, A_TILE, A_OUT), functools.partial(cur, col=col)))
    stage = lambda rows: pltpu.VMEM((A_OUT // LANES, rows, LANES), F32)
    return pl.pallas_call(
        _dilated_kernel,
        grid=(b, s // A_TILE),
        in_specs=in_specs,
        out_specs=pl.BlockSpec((None, A_TILE, A_OUT), lambda bi, j: (bi, j, 0)),
        out_shape=jax.ShapeDtypeStruct((b, s, A_OUT), BF16),
        scratch_shapes=[stage(A_TILE), stage(2 * A_TILE), stage(2 * A_TILE),
                        stage(A_TILE), stage(A_TILE), stage(A_TILE)],
        compiler_params=pltpu.CompilerParams(dimension_semantics=("arbitrary", "arbitrary"),
                                             vmem_limit_bytes=VMEM_LIMIT),
        name="dilated_attn",
    )(*([pa3] * (5 * A_GROUPS)))


SEL_TQ = 128
SEL_CHUNK = 512
SCORE_GROUP = 16
ATT_CHUNK = 512


def _sparse_kernel(qi_ref, wi_ref, ki_ref, q_ref, k_ref, v_ref, o_ref, key_ref, keyt_ref, m_ref, l_ref, acc_ref,
                   *, topk):
    i = pl.program_id(1)
    qs = i * SEL_TQ
    nch = (qs + SEL_TQ - 1) // SEL_CHUNK + 1
    nat = (qs + SEL_TQ - 1) // ATT_CHUNK + 1
    lane = lax.broadcasted_iota(jnp.int32, (1, LANES), 1)
    qpos = qs + lax.broadcasted_iota(jnp.int32, (SEL_TQ, 1), 0)

    qi = qi_ref[...]
    per_head = []
    for h in range(IDX_HEADS):
        pair = qi[:, (h // 2) * LANES:(h // 2 + 1) * LANES]
        keep = (lane < IDX_DIM) if h % 2 == 0 else (lane >= IDX_DIM)
        per_head.append(jnp.where(keep, pair, jnp.zeros_like(pair)))
    groups = range(0, SEL_TQ, SCORE_GROUP)
    stacked = jnp.concatenate([ph[g:g + SCORE_GROUP] for g in groups for ph in per_head], axis=0)
    w = wi_ref[...]
    wcol = [w[:, h:h + 1] for h in range(IDX_HEADS)]

    def to_key(score):
        bits = lax.bitcast_convert_type(score, jnp.int32)
        return bits ^ ((bits >> 31) & 0x7FFFFFFF)

    part = SEL_CHUNK

    def score_chunk(c, group_max):
        start = pl.multiple_of(c * SEL_CHUNK, SEL_CHUNK)
        logits = _nt_dot(stacked, ki_ref[pl.ds(start, part), :])
        pieces = []
        for g in groups:
            base = g * IDX_HEADS
            sg = jnp.zeros((SCORE_GROUP, part), F32)
            for h in range(IDX_HEADS):
                rows = slice(base + h * SCORE_GROUP, base + (h + 1) * SCORE_GROUP)
                sg = sg + jnp.maximum(logits[rows], 0.0) * wcol[h][g:g + SCORE_GROUP]
            pieces.append(sg)
        score = jnp.concatenate(pieces, axis=0)
        kpos = start + lax.broadcasted_iota(jnp.int32, (1, part), 1)
        causal = kpos <= qpos
        keys = jnp.where(causal, to_key(score), INT_MIN)
        key_ref[:, pl.ds(start, part)] = keys
        score = jnp.where(causal, score, -jnp.inf)
        group_max = list(group_max)
        for jj in range(part // LANES):
            group_max[jj % 2] = jnp.maximum(group_max[jj % 2], score[:, jj * LANES:(jj + 1) * LANES])
        return tuple(group_max)

    def transpose_chunk(c):
        start = pl.multiple_of(c * SEL_CHUNK, SEL_CHUNK)
        keyt_ref[pl.ds(start, SEL_CHUNK), :] = key_ref[:, pl.ds(start, SEL_CHUNK)].T

    def score_step(c, group_max):
        transpose_chunk(c - 1)
        return score_chunk(c, group_max)

    neg_inf = jnp.full((SEL_TQ, LANES), -jnp.inf, F32)
    gm_even, gm_odd = lax.fori_loop(1, nch, score_step, score_chunk(0, (neg_inf, neg_inf)))
    transpose_chunk(nch - 1)

    def to_lanes(col):
        return jnp.broadcast_to(col, (SEL_TQ, LANES)).T[0:1, :]

    def to_rows(lanes):
        return jnp.broadcast_to(lanes, (LANES, SEL_TQ)).T[:, 0:1]

    def count(compare, level):
        lanes_acc = 8 * SUBLANES
        def body(c, acc):
            blk = keyt_ref[pl.ds(pl.multiple_of(c * SEL_CHUNK, SEL_CHUNK), SEL_CHUNK), :]
            hit = jnp.where(compare(blk, level), 1.0, 0.0)
            return acc + jnp.sum(hit.reshape(SEL_CHUNK // lanes_acc, lanes_acc, LANES), axis=0)
        acc = lax.fori_loop(0, nch, body, jnp.zeros((lanes_acc, LANES), F32))
        return jnp.sum(acc, axis=0, keepdims=True)

    kf = float(topk)
    assert topk <= 2 * LANES and SEL_TQ == LANES
    searchable = qs >= topk
    lo_f = to_lanes(jnp.minimum(jnp.min(gm_even, axis=1, keepdims=True), jnp.min(gm_odd, axis=1, keepdims=True)))
    hi_f = to_lanes(jnp.maximum(jnp.max(gm_even, axis=1, keepdims=True), jnp.max(gm_odd, axis=1, keepdims=True)))
    lo0 = jnp.where(searchable, to_key(lo_f), INT_MIN + 1)
    hi0 = to_key(hi_f) + 1
    cnt_lo0 = jnp.where(searchable, kf + 1.0, kf) + jnp.zeros((1, LANES), F32)

    def candidate(lo, hi, cnt_lo):
        mid = (lo >> 1) + (hi >> 1) + (lo & hi & 1)
        open_q = (cnt_lo > kf) & (mid > lo)
        cand = jnp.where((lo < 0) & (hi > 0), 0, jnp.where((lo == 0) & (hi > 1), 1, mid))
        open_q = jnp.where(open_q, 1.0, 0.0)
        return cand, open_q, jnp.max(open_q) > 0.0

    def search_step(carry):
        lo, hi, cnt_lo, cand, open_q, _ = carry
        cnt = count(lambda blk, level: blk >= level, cand)
        up = (open_q > 0.0) & (cnt >= kf)
        down = (open_q > 0.0) & (cnt < kf)
        lo, cnt_lo, hi = jnp.where(up, cand, lo), jnp.where(up, cnt, cnt_lo), jnp.where(down, cand, hi)
        return (lo, hi, cnt_lo) + candidate(lo, hi, cnt_lo)

    thr_q, _, cnt_thr = lax.while_loop(lambda carry: carry[5], lambda carry: search_step(search_step(carry)),
                                       (lo0, hi0, cnt_lo0) + candidate(lo0, hi0, cnt_lo0))[:3]
    has_ties = jnp.max(cnt_thr) > kf
    thr = to_rows(thr_q)

    @pl.when(has_ties)
    def _():
        need = to_rows(kf - count(lambda blk, level: blk > level, thr_q))
        r = lax.broadcasted_iota(jnp.int32, (ATT_CHUNK, ATT_CHUNK), 0)
        cc = lax.broadcasted_iota(jnp.int32, (ATT_CHUNK, ATT_CHUNK), 1)
        before = jnp.where(r < cc, 1.0, 0.0).astype(BF16)

        def body(c, seen):
            cols = pl.ds(pl.multiple_of(c * ATT_CHUNK, ATT_CHUNK), ATT_CHUNK)
            blk = key_ref[:, cols]
            eq = blk == thr
            eqf = jnp.where(eq, 1.0, 0.0)
            rank = seen + jnp.dot(eqf.astype(BF16), before, preferred_element_type=F32)
            key_ref[:, cols] = jnp.where(eq & (rank >= need), INT_MIN, blk)
            return seen + jnp.sum(eqf, axis=1, keepdims=True)
        lax.fori_loop(0, nat, body, jnp.zeros((SEL_TQ, 1), F32))

    qstack = _stack_heads(q_ref[...])

    def logits(c, sel):
        start = pl.multiple_of(c * ATT_CHUNK, ATT_CHUNK)
        bias = jnp.where(sel, 0.0, NEG)
        return _nt_dot(qstack, k_ref[pl.ds(start, ATT_CHUNK), :]) + jnp.concatenate([bias] * B_HEADS, axis=0)

    def values(c):
        return v_ref[pl.ds(pl.multiple_of(c * ATT_CHUNK, ATT_CHUNK), ATT_CHUNK), :]

    def selected(c):
        return key_ref[:, pl.ds(pl.multiple_of(c * ATT_CHUNK, ATT_CHUNK), ATT_CHUNK)] >= thr

    _flash_init(m_ref, l_ref, acc_ref)
    _flash_pipeline(nat, lambda c: logits(c, selected(c)), values, m_ref, l_ref, acc_ref)

    o_ref[...] = _unstack_heads(_flash_result(l_ref, acc_ref), SEL_TQ).astype(o_ref.dtype)


def _sparse_attention(qi3, wi3, ki3, pb3):
    b, s, _ = pb3.shape
    topk = min(TOPK_MAX, s // 4)
    rows = B_HEADS * SEL_TQ
    qblk = lambda width, col: pl.BlockSpec((None, SEL_TQ, width), lambda bi, i: (bi, i, col))
    seq = lambda width, col: pl.BlockSpec((None, s, width), lambda bi, i: (bi, 0, col))
    return pl.pallas_call(
        functools.partial(_sparse_kernel, topk=topk),
        grid=(b, s // SEL_TQ),
        in_specs=[qblk(IDX_Q, 0), qblk(LANES, 0), seq(LANES, 0), qblk(B_OUT, 0), seq(B_OUT, 1), seq(B_OUT, 2)],
        out_specs=pl.BlockSpec((None, SEL_TQ, B_OUT), lambda bi, i: (bi, i, 0)),
        out_shape=jax.ShapeDtypeStruct((b, s, B_OUT), BF16),
        scratch_shapes=[pltpu.VMEM((SEL_TQ, s), jnp.int32), pltpu.VMEM((s, SEL_TQ), jnp.int32),
                        pltpu.VMEM((rows, LANES), F32), pltpu.VMEM((rows, LANES), F32),
                        pltpu.VMEM((rows, B_OUT), F32)],
        compiler_params=pltpu.CompilerParams(dimension_semantics=("arbitrary", "arbitrary"),
                                             vmem_limit_bytes=VMEM_LIMIT),
        name="sparse_attn",
    )(qi3, wi3, ki3, pb3, pb3, pb3)


DIFF_TQ = 512
DIFF_TK = 1024


def _diff_kernel(lam_ref, g_ref, q_ref, k_ref, v_ref, o_ref, m_ref, l_ref, acc_ref, *, out_scale):
    i = pl.program_id(2)
    qstack = _stack_heads(q_ref[...])
    _flash_init(m_ref, l_ref, acc_ref)

    def logits(kb):
        return _nt_dot(qstack, k_ref[pl.ds(pl.multiple_of(kb * DIFF_TK, DIFF_TK), DIFF_TK), :])

    def values(kb):
        return v_ref[pl.ds(pl.multiple_of(kb * DIFF_TK, DIFF_TK), DIFF_TK), :]

    n_full = (i * DIFF_TQ) // DIFF_TK

    def full_block(kb, carry):
        _flash_step(logits(kb), values(kb), m_ref, l_ref, acc_ref)
        return carry

    lax.fori_loop(0, n_full, full_block, 0)

    def diagonal(width):
        start = pl.multiple_of(n_full * DIFF_TK, DIFF_TK)
        qpos = i * DIFF_TQ + (lax.broadcasted_iota(jnp.int32, (2 * DIFF_TQ, width), 0) & (DIFF_TQ - 1))
        kpos = start + lax.broadcasted_iota(jnp.int32, (2 * DIFF_TQ, width), 1)
        s = _nt_dot(qstack, k_ref[pl.ds(start, width), :]) + jnp.where(kpos <= qpos, 0.0, NEG)
        _flash_step(s, v_ref[pl.ds(start, width), :], m_ref, l_ref, acc_ref)

    assert DIFF_TK == 2 * DIFF_TQ
    first_half = (i * DIFF_TQ) % DIFF_TK == 0
    pl.when(first_half)(lambda: diagonal(DIFF_TQ))
    pl.when(jnp.logical_not(first_half))(lambda: diagonal(DIFF_TK))

    res = _flash_result(l_ref, acc_ref)
    o = res[:DIFF_TQ] - lam_ref[0] * res[DIFF_TQ:]
    o_ref[...] = (_rms(o, g_ref[...]) * out_scale).astype(o_ref.dtype)


def _diff_attention(pc3, lam, subln_g, lam_init):
    b, s, _ = pc3.shape
    nq = C_QK // LANES
    return pl.pallas_call(
        functools.partial(_diff_kernel, out_scale=1.0 - lam_init),
        grid=(b, C_HEADS, s // DIFF_TQ),
        in_specs=[pl.BlockSpec(memory_space=pltpu.SMEM),
                  pl.BlockSpec((1, C_VDIM), lambda bi, h, i: (0, 0)),
                  pl.BlockSpec((None, DIFF_TQ, LANES), lambda bi, h, i: (bi, i, h)),
                  pl.BlockSpec((None, s, LANES), lambda bi, h, i: (bi, 0, nq + h)),
                  pl.BlockSpec((None, s, C_VDIM), lambda bi, h, i: (bi, 0, 2 * nq + h))],
        out_specs=pl.BlockSpec((None, DIFF_TQ, C_VDIM), lambda bi, h, i: (bi, i, h)),
        out_shape=jax.ShapeDtypeStruct((b, s, C_OUT), BF16),
        scratch_shapes=[pltpu.VMEM((2 * DIFF_TQ, LANES), F32), pltpu.VMEM((2 * DIFF_TQ, LANES), F32),
                        pltpu.VMEM((2 * DIFF_TQ, C_VDIM), F32)],
        compiler_params=pltpu.CompilerParams(dimension_semantics=("arbitrary", "arbitrary", "arbitrary"),
                                             vmem_limit_bytes=VMEM_LIMIT),
        name="diff_attn",
    )(lam, subln_g, pc3, pc3, pc3)


def _merge_kernel(x_ref, gpre_ref, gpost_ref, oa_ref, ob_ref, oc_ref, wg_ref, wa_ref, wb_ref, wc_ref, wo_ref,
                  out_ref, y_ref, h_ref):
    x = x_ref[...]
    xn = _rms(x, gpre_ref[...]).astype(BF16)
    branches = ((oa_ref, wa_ref), (ob_ref, wb_ref), (oc_ref, wc_ref))
    cw = 2 * LANES
    for c in range(0, D_MODEL, cw):
        y = jnp.zeros((x.shape[0], cw), F32)
        for j, (o_ref, w_ref) in enumerate(branches):
            logit = jnp.dot(xn, wg_ref[:, j * D_MODEL + c:j * D_MODEL + c + cw], preferred_element_type=F32)
            gate = 1.0 / (1.0 + jnp.exp(-logit))
            y = y + gate * jnp.dot(o_ref[...], w_ref[:, c:c + cw], preferred_element_type=F32)
        y_ref[:, c:c + cw] = y.astype(BF16)
    ssq = jnp.zeros((x.shape[0], 1), F32)
    for c in range(0, D_MODEL, cw):
        h = jnp.dot(y_ref[...], wo_ref[:, c:c + cw], preferred_element_type=F32)
        h_ref[:, c:c + cw] = h
        ssq = ssq + jnp.sum(h * h, axis=1, keepdims=True)
    inv = lax.rsqrt(ssq * (1.0 / D_MODEL) + NORM_EPS)
    out_ref[...] = x + h_ref[...] * inv * gpost_ref[...]


def _merge(x2, gpre, gpost, oa, ob, oc, w, tm=512):
    n = x2.shape[0]
    row = lambda width: pl.BlockSpec((tm, width), lambda i: (i, 0))
    full = lambda a: pl.BlockSpec(a.shape, lambda i: (0, 0))
    return pl.pallas_call(
        _merge_kernel,
        grid=(n // tm,),
        in_specs=[row(D_MODEL), full(gpre), full(gpost), row(A_OUT), row(B_OUT), row(C_OUT)] + [full(a) for a in w],
        out_specs=row(D_MODEL),
        out_shape=jax.ShapeDtypeStruct((n, D_MODEL), F32),
        scratch_shapes=[pltpu.VMEM((tm, D_MODEL), BF16), pltpu.VMEM((tm, D_MODEL), F32)],
        compiler_params=pltpu.CompilerParams(dimension_semantics=("arbitrary",), vmem_limit_bytes=VMEM_LIMIT),
        name="merge",
    )(x2, gpre, gpost, oa, ob, oc, *w)


FFN_CHUNK = 256
CARRY_ROWS = 8


def _ffn_kernel(x_ref, gpre_ref, gpost_ref, wup_ref, cw_ref, cb_ref, wdn_ref, out_ref, carry_ref, act_ref,
                *, tiles_per_seq):
    x = x_ref[...]
    tm = x.shape[0]
    xn = _rms(x, gpre_ref[...]).astype(BF16)
    first = pl.program_id(0) % tiles_per_seq == 0
    row = lax.broadcasted_iota(jnp.int32, (tm, 1), 0)

    def conv(col):
        h = jnp.dot(xn, wup_ref[:, col:col + FFN_CHUNK], preferred_element_type=F32)
        prev = jnp.where(first, 0.0, carry_ref[:, col:col + FFN_CHUNK])
        carry_ref[:, col:col + FFN_CHUNK] = h[tm - CARRY_ROWS:, :]
        p1 = prev[CARRY_ROWS - 1:CARRY_ROWS, :]
        p2 = prev[CARRY_ROWS - 2:CARRY_ROWS - 1, :]
        h1 = jnp.where(row == 0, p1, pltpu.roll(h, 1, 0))
        h2 = jnp.where(row == 0, p2, jnp.where(row == 1, p1, pltpu.roll(h, 2, 0)))
        w = cw_ref[:, col:col + FFN_CHUNK]
        return h2 * w[0:1, :] + h1 * w[1:2, :] + h * w[2:3, :] + cb_ref[:, col:col + FFN_CHUNK]

    for c in range(0, D_FF, FFN_CHUNK):
        g = conv(c)
        u = conv(D_FF + c)
        gelu = 0.5 * g * (1.0 + jnp.tanh(math.sqrt(2.0 / math.pi) * (g + 0.044715 * (g * g * g))))
        act_ref[:, c:c + FFN_CHUNK] = (gelu * u).astype(BF16)
    h = jnp.dot(act_ref[...], wdn_ref[...], preferred_element_type=F32)
    out_ref[...] = x + _rms(h, gpost_ref[...])


def _ffn(x2, gpre, gpost, wup, conv_w, conv_b, wdn, seq_len, tm=512):
    n = x2.shape[0]
    row = lambda width: pl.BlockSpec((tm, width), lambda i: (i, 0))
    full = lambda a: pl.BlockSpec(a.shape, lambda i: (0, 0))
    return pl.pallas_call(
        functools.partial(_ffn_kernel, tiles_per_seq=seq_len // tm),
        grid=(n // tm,),
        in_specs=[row(D_MODEL), full(gpre), full(gpost), full(wup), full(conv_w), full(conv_b), full(wdn)],
        out_specs=row(D_MODEL),
        out_shape=jax.ShapeDtypeStruct((n, D_MODEL), F32),
        scratch_shapes=[pltpu.VMEM((CARRY_ROWS, 2 * D_FF), F32), pltpu.VMEM((tm, D_FF), BF16)],
        compiler_params=pltpu.CompilerParams(dimension_semantics=("arbitrary",), vmem_limit_bytes=VMEM_LIMIT),
        name="conv_ffn",
    )(x2, gpre, gpost, wup, conv_w, conv_b, wdn)


def _rope_lane_tables(positions):
    half = ROPE_DIM // 2
    inv = ROPE_THETA ** (-jnp.arange(0, ROPE_DIM, 2, dtype=F32) / ROPE_DIM)
    ang = positions.astype(F32).reshape(-1, 1) * inv
    cos, sin = jnp.cos(ang), jnp.sin(ang)
    n = ang.shape[0]
    rest = HEAD_DIM - ROPE_DIM
    ones, zeros, zh = jnp.ones((n, rest), F32), jnp.zeros((n, rest), F32), jnp.zeros((n, half), F32)
    ct = jnp.concatenate([cos, cos, ones], axis=1)
    s1 = jnp.concatenate([-sin, zh, zeros], axis=1)
    s2 = jnp.concatenate([zh, sin, zeros], axis=1)
    return tuple(jnp.tile(t, (1, LANES // HEAD_DIM)) for t in (ct, s1, s2))


def _split_w_in(w_in):
    o_b, o_i, o_c, o_g = A_IN, A_IN + B_IN, A_IN + B_IN + IDX_IN, A_IN + B_IN + IDX_IN + C_IN
    wk = w_in[:, o_i + IDX_Q:o_i + IDX_Q + IDX_DIM]
    ww = w_in[:, o_i + IDX_Q + IDX_DIM:o_c]
    proj = (w_in[:, :o_b], w_in[:, o_b:o_i], w_in[:, o_i:o_i + IDX_Q],
            jnp.concatenate([wk, wk], axis=1),
            jnp.pad(ww, ((0, 0), (0, LANES - IDX_HEADS))),
            w_in[:, o_c:o_g])
    return tuple(a.astype(BF16) for a in proj), w_in[:, o_g:].astype(BF16)


def kernel(x, positions, w_in, w_br_a, w_br_b, w_br_c, w_out, lam_q1, lam_k1, lam_q2, lam_k2, subln_g,
           norm_mix_pre, norm_mix_post, norm_ffn_pre, norm_ffn_post, w_ffn_up, conv_w, conv_b, w_ffn_down):
    b, s, d = x.shape
    depth = w_in.shape[0]
    assert d == D_MODEL and s % A_TILE == 0 and s % SEL_CHUNK == 0 and s % DIFF_TK == 0
    tables = _rope_lane_tables(positions)
    x2 = x.reshape(b * s, d)
    vec = lambda a: a.reshape(1, -1)
    for layer in range(depth):
        lam_init = 0.8 - 0.6 * math.exp(-0.3 * layer)
        lam = (jnp.exp(jnp.sum(lam_q1[layer] * lam_k1[layer])) - jnp.exp(jnp.sum(lam_q2[layer] * lam_k2[layer]))
               + lam_init).reshape(1).astype(F32)
        w_proj, w_gate = _split_w_in(w_in[layer])
        pa, pb, qi, ki, wi, pc = _inproj(x2, vec(norm_mix_pre[layer]), tables, w_proj)
        seq = lambda a: a.reshape(b, s, a.shape[-1])
        oa = _dilated_attention(seq(pa))
        ob = _sparse_attention(seq(qi), seq(wi), seq(ki), seq(pb))
        oc = _diff_attention(seq(pc), lam, vec(subln_g[layer]), lam_init)
        flat = lambda a: a.reshape(b * s, a.shape[-1])
        w_merge = (w_gate, w_br_a[layer].astype(BF16), w_br_b[layer].astype(BF16), w_br_c[layer].astype(BF16),
                   w_out[layer].astype(BF16))
        x2 = _merge(x2, vec(norm_mix_pre[layer]), vec(norm_mix_post[layer]), flat(oa), flat(ob), flat(oc), w_merge)
        x2 = _ffn(x2, vec(norm_ffn_pre[layer]), vec(norm_ffn_post[layer]), w_ffn_up[layer].astype(BF16),
                  conv_w[layer], vec(conv_b[layer]), w_ffn_down[layer].astype(BF16), s)
    return x2.reshape(b, s, d)
```

```python
import functools
import math

import jax
import jax.numpy as jnp
from jax import lax
from jax.experimental import pallas as pl
from jax.experimental.pallas import tpu as pltpu

D_MODEL = 1024
HEAD_DIM = 64
ROPE_THETA = 500000.0
ROPE_DIM = HEAD_DIM // 4
NORM_EPS = 1e-6
Q_BLOCK = 128

A_PAIRS = ((128, 1), (512, 4), (2048, 16))
A_GROUPS = len(A_PAIRS)
A_HEADS = 4
A_OUT = A_HEADS * HEAD_DIM
B_HEADS = 4
B_OUT = B_HEADS * HEAD_DIM
IDX_HEADS = 8
IDX_DIM = 64
TOPK_MAX = 256
C_HEADS = 4
C_VDIM = 2 * HEAD_DIM
C_OUT = C_HEADS * C_VDIM
N_BRANCH = 3

A_IN = A_GROUPS * 3 * A_HEADS * HEAD_DIM
B_IN = 3 * B_HEADS * HEAD_DIM
IDX_Q = IDX_HEADS * IDX_DIM
IDX_IN = IDX_Q + IDX_DIM + IDX_HEADS
C_QK = C_HEADS * 2 * HEAD_DIM
C_IN = 2 * C_QK + C_OUT
D_FF = ((8 * D_MODEL // 3 + 127) // 128) * 128
CONV_WIDTH = 3

LANES = 128
SUBLANES = 8
SCALE = HEAD_DIM ** -0.5
NEG = -1e30
INT_MIN = -2 ** 31
VMEM_LIMIT = 56 * 1024 * 1024

F32 = jnp.float32
BF16 = jnp.bfloat16
NT_DIMS = (((1,), (1,)), ((), ()))


def _nt_dot(a, b):
    return lax.dot_general(a, b, NT_DIMS, preferred_element_type=F32)


def _rms(x, g):
    return x * lax.rsqrt(jnp.mean(x * x, axis=-1, keepdims=True) + NORM_EPS) * g


def _inproj_kernel(x_ref, g_ref, ct_ref, s1_ref, s2_ref, wa_ref, wb_ref, wqi_ref, wki_ref, wwi_ref, wc_ref,
                   pa_ref, pb_ref, qi_ref, ki_ref, wi_ref, pc_ref):
    xn = _rms(x_ref[...], g_ref[...]).astype(BF16)
    ct, s1, s2 = ct_ref[...], s1_ref[...], s2_ref[...]

    def rope(y):
        return y * ct + pltpu.roll(y, LANES - ROPE_DIM // 2, 1) * s1 + pltpu.roll(y, ROPE_DIM // 2, 1) * s2

    def project(w_ref, out_ref, n_cols, mode_of_col):
        for c in range(0, n_cols, 2 * LANES):
            y = jnp.dot(xn, w_ref[:, c:c + 2 * LANES], preferred_element_type=F32)
            for h in range(2):
                col = c + h * LANES
                z = y[:, h * LANES:(h + 1) * LANES]
                mode = mode_of_col(col)
                if mode != "v":
                    z = rope(z)
                if mode == "q":
                    z = z * SCALE
                out_ref[:, col:col + LANES] = z.astype(out_ref.dtype)

    qkv_mode = lambda width: (lambda col: ("q", "k", "v")[(col // width) % 3])
    project(wa_ref, pa_ref, A_IN, qkv_mode(A_OUT))
    project(wb_ref, pb_ref, B_IN, qkv_mode(B_OUT))
    project(wqi_ref, qi_ref, IDX_Q, lambda col: "k")
    project(wc_ref, pc_ref, C_IN, qkv_mode(C_QK))
    ki = jnp.dot(xn, wki_ref[...], preferred_element_type=F32)
    ki_ref[...] = rope(ki).astype(BF16)
    wi = jnp.dot(xn, wwi_ref[...], preferred_element_type=F32)
    wi_ref[...] = (wi * (IDX_HEADS ** -0.5)) * (IDX_DIM ** -0.5)


def _inproj(x2, gain, tables, w, tm=512):
    n = x2.shape[0]
    row = lambda width: pl.BlockSpec((tm, width), lambda i: (i, 0))
    full = lambda a: pl.BlockSpec(a.shape, lambda i: (0, 0))
    outs = [(A_IN, BF16), (B_IN, BF16), (IDX_Q, BF16), (LANES, BF16), (LANES, F32), (C_IN, BF16)]
    return pl.pallas_call(
        _inproj_kernel,
        grid=(n // tm,),
        in_specs=[row(D_MODEL), full(gain), row(LANES), row(LANES), row(LANES)] + [full(a) for a in w],
        out_specs=[row(wd) for wd, _ in outs],
        out_shape=[jax.ShapeDtypeStruct((n, wd), dt) for wd, dt in outs],
        compiler_params=pltpu.CompilerParams(dimension_semantics=("arbitrary",), vmem_limit_bytes=VMEM_LIMIT),
        name="inproj",
    )(x2, gain, *tables, *w)


def _head_masks(width):
    lane = lax.broadcasted_iota(jnp.int32, (1, width), 1)
    return [lane // HEAD_DIM == h for h in range(width // HEAD_DIM)]


def _stack_heads(q):
    zero = jnp.zeros_like(q)
    return jnp.concatenate([jnp.where(hm, q, zero) for hm in _head_masks(q.shape[1])], axis=0)


def _unstack_heads(x, rows):
    hmask = _head_masks(HEAD_DIM * (x.shape[0] // rows))
    out = jnp.where(hmask[0], x[0:rows], 0.0)
    for h in range(1, len(hmask)):
        out = jnp.where(hmask[h], x[h * rows:(h + 1) * rows], out)
    return out


def _lane_tile(x, width):
    reps = width // LANES
    return x if reps == 1 else jnp.concatenate([x] * reps, axis=1)


def _flash_init(m_ref, l_ref, acc_ref):
    m_ref[...] = jnp.full(m_ref.shape, NEG, F32)
    l_ref[...] = jnp.zeros(l_ref.shape, F32)
    acc_ref[...] = jnp.zeros(acc_ref.shape, F32)


def _flash_step(s, v, m_ref, l_ref, acc_ref):
    tk = s.shape[1]
    m_old = m_ref[...]
    m_new = jnp.maximum(m_old, jnp.max(s, axis=1, keepdims=True))
    p = jnp.exp(s - _lane_tile(m_new, tk))
    alpha = jnp.exp(m_old - m_new)
    psum = p[:, :LANES]
    for j in range(1, tk // LANES):
        psum = psum + p[:, j * LANES:(j + 1) * LANES]
    l_ref[...] = alpha * l_ref[...] + psum
    acc_ref[...] = (_lane_tile(alpha, acc_ref.shape[1]) * acc_ref[...]
                    + jnp.dot(p.astype(BF16), v, preferred_element_type=F32))
    m_ref[...] = m_new


def _flash_pipeline(n, logits_fn, values_fn, m_ref, l_ref, acc_ref):
    def body(t, s_cur):
        s_next = logits_fn(t + 1)
        _flash_step(s_cur, values_fn(t), m_ref, l_ref, acc_ref)
        return s_next
    s_last = lax.fori_loop(0, n - 1, body, logits_fn(0))
    _flash_step(s_last, values_fn(n - 1), m_ref, l_ref, acc_ref)


def _flash_result(l_ref, acc_ref):
    return acc_ref[...] / jnp.sum(l_ref[...], axis=1, keepdims=True)


A_TILE = max(w for w, _ in A_PAIRS)


def _dilated_kernel(*refs):
    group_refs = [refs[5 * g:5 * g + 5] for g in range(A_GROUPS)]
    o_ref, qs_ref, ks_ref, vs_ref, m_sc, l_sc, n_sc = refs[5 * A_GROUPS:]
    j = pl.program_id(1)
    qi = lax.broadcasted_iota(jnp.int32, (Q_BLOCK, 2 * Q_BLOCK), 0)
    kj = lax.broadcasted_iota(jnp.int32, (Q_BLOCK, 2 * Q_BLOCK), 1)
    rel = kj - qi
    band = (rel >= 0) & (rel <= Q_BLOCK)
    bias_band = jnp.where(band, 0.0, NEG)
    bias_first = jnp.where(band & (kj >= Q_BLOCK), 0.0, NEG)
    halves = A_OUT // LANES

    def get(ref, rows):
        return jnp.concatenate([ref[hh, rows, :] for hh in range(halves)], axis=1)

    def put(ref, rows, val):
        for hh in range(halves):
            ref[hh, rows, :] = val[:, hh * LANES:(hh + 1) * LANES]

    for g, ((window, dil), (q_ref, kp_ref, kc_ref, vp_ref, vc_ref)) in enumerate(zip(A_PAIRS, group_refs)):
        assert window == Q_BLOCK * dil
        put(qs_ref, slice(0, A_TILE), q_ref[...].astype(F32))
        put(ks_ref, slice(0, window), kp_ref[...].astype(F32))
        put(ks_ref, slice(window, window + A_TILE), kc_ref[...].astype(F32))
        put(vs_ref, slice(0, window), vp_ref[...].astype(F32))
        put(vs_ref, slice(window, window + A_TILE), vc_ref[...].astype(F32))
        shift = dil.bit_length() - 1

        def problem(p, carry, g=g, window=window, dil=dil, shift=shift):
            sub = p >> shift
            base = sub * window + (p & (dil - 1))
            stride = dil if dil > 1 else None
            q_rows = pl.ds(base, Q_BLOCK, stride=stride)
            k_rows = pl.ds(base, 2 * Q_BLOCK, stride=stride)
            qstack = _stack_heads(get(qs_ref, q_rows).astype(BF16))
            k = get(ks_ref, k_rows).astype(BF16)
            v = get(vs_ref, k_rows).astype(BF16)
            bias = jnp.where((sub > 0) | (j > 0), bias_band, bias_first)
            s = _nt_dot(qstack, k) + jnp.concatenate([bias] * A_HEADS, axis=0)
            m = jnp.max(s, axis=1, keepdims=True)
            e = jnp.exp(s - m)
            l = jnp.sum(e, axis=1, keepdims=True)
            num = jnp.dot(e.astype(BF16), v, preferred_element_type=F32)
            m_n, l_n, num_n = (_unstack_heads(t, Q_BLOCK) for t in (m, l, num))
            if g == 0:
                put(m_sc, q_rows, m_n)
                put(l_sc, q_rows, l_n)
                put(n_sc, q_rows, num_n)
            else:
                m_o = get(m_sc, q_rows)
                m_x = jnp.maximum(m_o, m_n)
                a, b = jnp.exp(m_o - m_x), jnp.exp(m_n - m_x)
                put(m_sc, q_rows, m_x)
                put(l_sc, q_rows, a * get(l_sc, q_rows) + b * l_n)
                put(n_sc, q_rows, a * get(n_sc, q_rows) + b * num_n)
            return carry

        lax.fori_loop(0, A_TILE // Q_BLOCK, problem, 0, unroll=4)

    all_rows = slice(0, A_TILE)
    o_ref[...] = (get(n_sc, all_rows) / get(l_sc, all_rows)).astype(o_ref.dtype)


def _dilated_attention(pa3):
    b, s, _ = pa3.shape
    in_specs = []
    for g, (window, _) in enumerate(A_PAIRS):
        per = A_TILE // window
        cur = lambda bi, j, col: (bi, j, col)
        prev = lambda bi, j, col, per=per: (bi, jnp.maximum(j * per - 1, 0), col)
        in_specs.append(pl.BlockSpec((None, A_TILE, A_OUT), functools.partial(cur, col=3 * g)))
        for col in (3 * g + 1, 3 * g + 2):
            in_specs.append(pl.BlockSpec((None, window, A_OUT), functools.partial(prev, col=col)))
            in_specs.append(pl.BlockSpec((None, A_TILE, A_OUT), functools.partial(cur, col=col)))
    stage = lambda rows: pltpu.VMEM((A_OUT // LANES, rows, LANES), F32)
    return pl.pallas_call(
        _dilated_kernel,
        grid=(b, s // A_TILE),
        in_specs=in_specs,
        out_specs=pl.BlockSpec((None, A_TILE, A_OUT), lambda bi, j: (bi, j, 0)),
        out_shape=jax.ShapeDtypeStruct((b, s, A_OUT), BF16),
        scratch_shapes=[stage(A_TILE), stage(2 * A_TILE), stage(2 * A_TILE),
                        stage(A_TILE), stage(A_TILE), stage(A_TILE)],
        compiler_params=pltpu.CompilerParams(dimension_semantics=("arbitrary", "arbitrary"),
                                             vmem_limit_bytes=VMEM_LIMIT),
        name="dilated_attn",
    )(*([pa3] * (5 * A_GROUPS)))


SEL_TQ = 128
SEL_CHUNK = 512
SCORE_GROUP = 16
ATT_CHUNK = 512


def _sparse_kernel(qi_ref, wi_ref, ki_ref, q_ref, k_ref, v_ref, o_ref, key_ref, keyt_ref, m_ref, l_ref, acc_ref,
                   *, topk):
    i = pl.program_id(1)
    qs = i * SEL_TQ
    nch = (qs + SEL_TQ - 1) // SEL_CHUNK + 1
    nat = (qs + SEL_TQ - 1) // ATT_CHUNK + 1
    lane = lax.broadcasted_iota(jnp.int32, (1, LANES), 1)
    qpos = qs + lax.broadcasted_iota(jnp.int32, (SEL_TQ, 1), 0)

    qi = qi_ref[...]
    per_head = []
    for h in range(IDX_HEADS):
        pair = qi[:, (h // 2) * LANES:(h // 2 + 1) * LANES]
        keep = (lane < IDX_DIM) if h % 2 == 0 else (lane >= IDX_DIM)
        per_head.append(jnp.where(keep, pair, jnp.zeros_like(pair)))
    groups = range(0, SEL_TQ, SCORE_GROUP)
    stacked = jnp.concatenate([ph[g:g + SCORE_GROUP] for g in groups for ph in per_head], axis=0)
    w = wi_ref[...]
    wcol = [w[:, h:h + 1] for h in range(IDX_HEADS)]

    def to_key(score):
        bits = lax.bitcast_convert_type(score, jnp.int32)
        return bits ^ ((bits >> 31) & 0x7FFFFFFF)

    def score_chunk(c, group_max):
        start = pl.multiple_of(c * SEL_CHUNK, SEL_CHUNK)
        logits = _nt_dot(stacked, ki_ref[pl.ds(start, SEL_CHUNK), :])
        pieces = []
        for g in groups:
            base = g * IDX_HEADS
            sg = jnp.zeros((SCORE_GROUP, SEL_CHUNK), F32)
            for h in range(IDX_HEADS):
                rows = slice(base + h * SCORE_GROUP, base + (h + 1) * SCORE_GROUP)
                sg = sg + jnp.maximum(logits[rows], 0.0) * wcol[h][g:g + SCORE_GROUP]
            pieces.append(sg)
        score = jnp.concatenate(pieces, axis=0)
        kpos = start + lax.broadcasted_iota(jnp.int32, (1, SEL_CHUNK), 1)
        causal = kpos <= qpos
        key_ref[:, pl.ds(start, SEL_CHUNK)] = jnp.where(causal, to_key(score), INT_MIN)
        score = jnp.where(causal, score, -jnp.inf)
        group_max = list(group_max)
        for jj in range(SEL_CHUNK // LANES):
            group_max[jj % 2] = jnp.maximum(group_max[jj % 2], score[:, jj * LANES:(jj + 1) * LANES])
        return tuple(group_max)

    def transpose_chunk(c):
        start = pl.multiple_of(c * SEL_CHUNK, SEL_CHUNK)
        keyt_ref[pl.ds(start, SEL_CHUNK), :] = key_ref[:, pl.ds(start, SEL_CHUNK)].T

    def score_step(c, group_max):
        transpose_chunk(c - 1)
        return score_chunk(c, group_max)

    neg_inf = jnp.full((SEL_TQ, LANES), -jnp.inf, F32)
    gm_even, gm_odd = lax.fori_loop(1, nch, score_step, score_chunk(0, (neg_inf, neg_inf)))
    transpose_chunk(nch - 1)

    def to_lanes(col):
        return jnp.broadcast_to(col, (SEL_TQ, LANES)).T[0:1, :]

    def to_rows(lanes):
        return jnp.broadcast_to(lanes, (LANES, SEL_TQ)).T[:, 0:1]

    def count(compare, level):
        lanes_acc = 8 * SUBLANES
        def body(c, acc):
            blk = keyt_ref[pl.ds(pl.multiple_of(c * SEL_CHUNK, SEL_CHUNK), SEL_CHUNK), :]
            hit = jnp.where(compare(blk, level), 1.0, 0.0)
            return acc + jnp.sum(hit.reshape(SEL_CHUNK // lanes_acc, lanes_acc, LANES), axis=0)
        acc = lax.fori_loop(0, nch, body, jnp.zeros((lanes_acc, LANES), F32))
        return jnp.sum(acc, axis=0, keepdims=True)

    kf = float(topk)
    assert topk <= 2 * LANES and SEL_TQ == LANES
    searchable = qs >= topk
    lo_f = to_lanes(jnp.minimum(jnp.min(gm_even, axis=1, keepdims=True), jnp.min(gm_odd, axis=1, keepdims=True)))
    hi_f = to_lanes(jnp.maximum(jnp.max(gm_even, axis=1, keepdims=True), jnp.max(gm_odd, axis=1, keepdims=True)))
    lo0 = jnp.where(searchable, to_key(lo_f), INT_MIN + 1)
    hi0 = to_key(hi_f) + 1
    cnt_lo0 = jnp.where(searchable, kf + 1.0, kf) + jnp.zeros((1, LANES), F32)

    def candidate(lo, hi, cnt_lo):
        mid = (lo >> 1) + (hi >> 1) + (lo & hi & 1)
        open_q = (cnt_lo > kf) & (mid > lo)
        cand = jnp.where((lo < 0) & (hi > 0), 0, jnp.where((lo == 0) & (hi > 1), 1, mid))
        open_q = jnp.where(open_q, 1.0, 0.0)
        return cand, open_q, jnp.max(open_q) > 0.0

    def search_step(carry):
        lo, hi, cnt_lo, cnt_hi, cand, open_q, _ = carry
        cnt = count(lambda blk, level: blk >= level, cand)
        up = (open_q > 0.0) & (cnt >= kf)
        down = (open_q > 0.0) & (cnt < kf)
        lo, cnt_lo = jnp.where(up, cand, lo), jnp.where(up, cnt, cnt_lo)
        hi, cnt_hi = jnp.where(down, cand, hi), jnp.where(down, cnt, cnt_hi)
        return (lo, hi, cnt_lo, cnt_hi) + candidate(lo, hi, cnt_lo)

    thr_q, _, cnt_thr, cnt_hi = lax.while_loop(
        lambda carry: carry[6], lambda carry: search_step(search_step(carry)),
        (lo0, hi0, cnt_lo0, jnp.zeros((1, LANES), F32)) + candidate(lo0, hi0, cnt_lo0))[:4]
    has_ties = jnp.max(cnt_thr) > kf
    thr = to_rows(thr_q)

    @pl.when(has_ties)
    def _():
        need = to_rows(jnp.where(cnt_thr > kf, kf - cnt_hi, kf))
        r = lax.broadcasted_iota(jnp.int32, (ATT_CHUNK, ATT_CHUNK), 0)
        cc = lax.broadcasted_iota(jnp.int32, (ATT_CHUNK, ATT_CHUNK), 1)
        before = jnp.where(r < cc, 1.0, 0.0).astype(BF16)

        def body(c, seen):
            cols = pl.ds(pl.multiple_of(c * ATT_CHUNK, ATT_CHUNK), ATT_CHUNK)
            blk = key_ref[:, cols]
            eq = blk == thr
            eqf = jnp.where(eq, 1.0, 0.0)
            rank = seen + jnp.dot(eqf.astype(BF16), before, preferred_element_type=F32)
            key_ref[:, cols] = jnp.where(eq & (rank >= need), INT_MIN, blk)
            return seen + jnp.sum(eqf, axis=1, keepdims=True)
        lax.fori_loop(0, nat, body, jnp.zeros((SEL_TQ, 1), F32))

    qstack = _stack_heads(q_ref[...])

    def logits(c, sel):
        start = pl.multiple_of(c * ATT_CHUNK, ATT_CHUNK)
        bias = jnp.where(sel, 0.0, NEG)
        return _nt_dot(qstack, k_ref[pl.ds(start, ATT_CHUNK), :]) + jnp.concatenate([bias] * B_HEADS, axis=0)

    def values(c):
        return v_ref[pl.ds(pl.multiple_of(c * ATT_CHUNK, ATT_CHUNK), ATT_CHUNK), :]

    def selected(c):
        return key_ref[:, pl.ds(pl.multiple_of(c * ATT_CHUNK, ATT_CHUNK), ATT_CHUNK)] >= thr

    _flash_init(m_ref, l_ref, acc_ref)
    _flash_pipeline(nat, lambda c: logits(c, selected(c)), values, m_ref, l_ref, acc_ref)

    o_ref[...] = _unstack_heads(_flash_result(l_ref, acc_ref), SEL_TQ).astype(o_ref.dtype)


def _sparse_attention(qi3, wi3, ki3, pb3):
    b, s, _ = pb3.shape
    topk = min(TOPK_MAX, s // 4)
    rows = B_HEADS * SEL_TQ
    qblk = lambda width, col: pl.BlockSpec((None, SEL_TQ, width), lambda bi, i: (bi, i, col))
    seq = lambda width, col: pl.BlockSpec((None, s, width), lambda bi, i: (bi, 0, col))
    return pl.pallas_call(
        functools.partial(_sparse_kernel, topk=topk),
        grid=(b, s // SEL_TQ),
        in_specs=[qblk(IDX_Q, 0), qblk(LANES, 0), seq(LANES, 0), qblk(B_OUT, 0), seq(B_OUT, 1), seq(B_OUT, 2)],
        out_specs=pl.BlockSpec((None, SEL_TQ, B_OUT), lambda bi, i: (bi, i, 0)),
        out_shape=jax.ShapeDtypeStruct((b, s, B_OUT), BF16),
        scratch_shapes=[pltpu.VMEM((SEL_TQ, s), jnp.int32), pltpu.VMEM((s, SEL_TQ), jnp.int32),
                        pltpu.VMEM((rows, LANES), F32), pltpu.VMEM((rows, LANES), F32),
                        pltpu.VMEM((rows, B_OUT), F32)],
        compiler_params=pltpu.CompilerParams(dimension_semantics=("arbitrary", "arbitrary"),
                                             vmem_limit_bytes=VMEM_LIMIT),
        name="sparse_attn",
    )(qi3, wi3, ki3, pb3, pb3, pb3)


DIFF_TQ = 512
DIFF_TK = 1024


def _diff_kernel(lam_ref, g_ref, q_ref, k_ref, v_ref, o_ref, m_ref, l_ref, acc_ref, *, out_scale):
    i = pl.program_id(2)
    qstack = _stack_heads(q_ref[...])
    _flash_init(m_ref, l_ref, acc_ref)

    def logits(kb):
        return _nt_dot(qstack, k_ref[pl.ds(pl.multiple_of(kb * DIFF_TK, DIFF_TK), DIFF_TK), :])

    def values(kb):
        return v_ref[pl.ds(pl.multiple_of(kb * DIFF_TK, DIFF_TK), DIFF_TK), :]

    n_full = (i * DIFF_TQ) // DIFF_TK

    def full_block(kb, carry):
        _flash_step(logits(kb), values(kb), m_ref, l_ref, acc_ref)
        return carry

    lax.fori_loop(0, n_full, full_block, 0)

    def diagonal(width):
        start = pl.multiple_of(n_full * DIFF_TK, DIFF_TK)
        qpos = i * DIFF_TQ + (lax.broadcasted_iota(jnp.int32, (2 * DIFF_TQ, width), 0) & (DIFF_TQ - 1))
        kpos = start + lax.broadcasted_iota(jnp.int32, (2 * DIFF_TQ, width), 1)
        s = _nt_dot(qstack, k_ref[pl.ds(start, width), :]) + jnp.where(kpos <= qpos, 0.0, NEG)
        _flash_step(s, v_ref[pl.ds(start, width), :], m_ref, l_ref, acc_ref)

    assert DIFF_TK == 2 * DIFF_TQ
    first_half = (i * DIFF_TQ) % DIFF_TK == 0
    pl.when(first_half)(lambda: diagonal(DIFF_TQ))
    pl.when(jnp.logical_not(first_half))(lambda: diagonal(DIFF_TK))

    res = _flash_result(l_ref, acc_ref)
    o = res[:DIFF_TQ] - lam_ref[0] * res[DIFF_TQ:]
    o_ref[...] = (_rms(o, g_ref[...]) * out_scale).astype(o_ref.dtype)


def _diff_attention(pc3, lam, subln_g, lam_init):
    b, s, _ = pc3.shape
    nq = C_QK // LANES
    return pl.pallas_call(
        functools.partial(_diff_kernel, out_scale=1.0 - lam_init),
        grid=(b, C_HEADS, s // DIFF_TQ),
        in_specs=[pl.BlockSpec(memory_space=pltpu.SMEM),
                  pl.BlockSpec((1, C_VDIM), lambda bi, h, i: (0, 0)),
                  pl.BlockSpec((None, DIFF_TQ, LANES), lambda bi, h, i: (bi, i, h)),
                  pl.BlockSpec((None, s, LANES), lambda bi, h, i: (bi, 0, nq + h)),
                  pl.BlockSpec((None, s, C_VDIM), lambda bi, h, i: (bi, 0, 2 * nq + h))],
        out_specs=pl.BlockSpec((None, DIFF_TQ, C_VDIM), lambda bi, h, i: (bi, i, h)),
        out_shape=jax.ShapeDtypeStruct((b, s, C_OUT), BF16),
        scratch_shapes=[pltpu.VMEM((2 * DIFF_TQ, LANES), F32), pltpu.VMEM((2 * DIFF_TQ, LANES), F32),
                        pltpu.VMEM((2 * DIFF_TQ, C_VDIM), F32)],
        compiler_params=pltpu.CompilerParams(dimension_semantics=("arbitrary", "arbitrary", "arbitrary"),
                                             vmem_limit_bytes=VMEM_LIMIT),
        name="diff_attn",
    )(lam, subln_g, pc3, pc3, pc3)


def _merge_kernel(x_ref, gpre_ref, gpost_ref, oa_ref, ob_ref, oc_ref, wg_ref, wa_ref, wb_ref, wc_ref, wo_ref,
                  out_ref, y_ref, h_ref):
    x = x_ref[...]
    xn = _rms(x, gpre_ref[...]).astype(BF16)
    branches = ((oa_ref, wa_ref), (ob_ref, wb_ref), (oc_ref, wc_ref))
    cw = 2 * LANES
    for c in range(0, D_MODEL, cw):
        y = jnp.zeros((x.shape[0], cw), F32)
        for j, (o_ref, w_ref) in enumerate(branches):
            logit = jnp.dot(xn, wg_ref[:, j * D_MODEL + c:j * D_MODEL + c + cw], preferred_element_type=F32)
            gate = 1.0 / (1.0 + jnp.exp(-logit))
            y = y + gate * jnp.dot(o_ref[...], w_ref[:, c:c + cw], preferred_element_type=F32)
        y_ref[:, c:c + cw] = y.astype(BF16)
    ssq = jnp.zeros((x.shape[0], 1), F32)
    for c in range(0, D_MODEL, cw):
        h = jnp.dot(y_ref[...], wo_ref[:, c:c + cw], preferred_element_type=F32)
        h_ref[:, c:c + cw] = h
        ssq = ssq + jnp.sum(h * h, axis=1, keepdims=True)
    inv = lax.rsqrt(ssq * (1.0 / D_MODEL) + NORM_EPS)
    out_ref[...] = x + h_ref[...] * inv * gpost_ref[...]


def _merge(x2, gpre, gpost, oa, ob, oc, w, tm=512):
    n = x2.shape[0]
    row = lambda width: pl.BlockSpec((tm, width), lambda i: (i, 0))
    full = lambda a: pl.BlockSpec(a.shape, lambda i: (0, 0))
    return pl.pallas_call(
        _merge_kernel,
        grid=(n // tm,),
        in_specs=[row(D_MODEL), full(gpre), full(gpost), row(A_OUT), row(B_OUT), row(C_OUT)] + [full(a) for a in w],
        out_specs=row(D_MODEL),
        out_shape=jax.ShapeDtypeStruct((n, D_MODEL), F32),
        scratch_shapes=[pltpu.VMEM((tm, D_MODEL), BF16), pltpu.VMEM((tm, D_MODEL), F32)],
        compiler_params=pltpu.CompilerParams(dimension_semantics=("arbitrary",), vmem_limit_bytes=VMEM_LIMIT),
        name="merge",
    )(x2, gpre, gpost, oa, ob, oc, *w)


FFN_CHUNK = 256
CARRY_ROWS = 8


def _ffn_kernel(x_ref, gpre_ref, gpost_ref, wup_ref, cw_ref, cb_ref, wdn_ref, out_ref, carry_ref, act_ref,
                *, tiles_per_seq):
    x = x_ref[...]
    tm = x.shape[0]
    xn = _rms(x, gpre_ref[...]).astype(BF16)
    first = pl.program_id(0) % tiles_per_seq == 0
    row = lax.broadcasted_iota(jnp.int32, (tm, 1), 0)
    assert CONV_WIDTH == 3 and CONV_WIDTH - 1 <= CARRY_ROWS

    def conv(col):
        h = jnp.dot(xn, wup_ref[:, col:col + FFN_CHUNK], preferred_element_type=F32)
        prev = jnp.where(first, 0.0, carry_ref[:, col:col + FFN_CHUNK])
        carry_ref[:, col:col + FFN_CHUNK] = h[tm - CARRY_ROWS:, :]
        p1 = prev[CARRY_ROWS - 1:CARRY_ROWS, :]
        p2 = prev[CARRY_ROWS - 2:CARRY_ROWS - 1, :]
        h1 = jnp.where(row == 0, p1, pltpu.roll(h, 1, 0))
        h2 = jnp.where(row == 0, p2, jnp.where(row == 1, p1, pltpu.roll(h, 2, 0)))
        w = cw_ref[:, col:col + FFN_CHUNK]
        return h2 * w[0:1, :] + h1 * w[1:2, :] + h * w[2:3, :] + cb_ref[:, col:col + FFN_CHUNK]

    for c in range(0, D_FF, FFN_CHUNK):
        g = conv(c)
        u = conv(D_FF + c)
        gelu = 0.5 * g * (1.0 + jnp.tanh(math.sqrt(2.0 / math.pi) * (g + 0.044715 * (g * g * g))))
        act_ref[:, c:c + FFN_CHUNK] = (gelu * u).astype(BF16)
    h = jnp.dot(act_ref[...], wdn_ref[...], preferred_element_type=F32)
    out_ref[...] = x + _rms(h, gpost_ref[...])


def _ffn(x2, gpre, gpost, wup, conv_w, conv_b, wdn, seq_len, tm=512):
    n = x2.shape[0]
    row = lambda width: pl.BlockSpec((tm, width), lambda i: (i, 0))
    full = lambda a: pl.BlockSpec(a.shape, lambda i: (0, 0))
    return pl.pallas_call(
        functools.partial(_ffn_kernel, tiles_per_seq=seq_len // tm),
        grid=(n // tm,),
        in_specs=[row(D_MODEL), full(gpre), full(gpost), full(wup), full(conv_w), full(conv_b), full(wdn)],
        out_specs=row(D_MODEL),
        out_shape=jax.ShapeDtypeStruct((n, D_MODEL), F32),
        scratch_shapes=[pltpu.VMEM((CARRY_ROWS, 2 * D_FF), F32), pltpu.VMEM((tm, D_FF), BF16)],
        compiler_params=pltpu.CompilerParams(dimension_semantics=("arbitrary",), vmem_limit_bytes=VMEM_LIMIT),
        name="conv_ffn",
    )(x2, gpre, gpost, wup, conv_w, conv_b, wdn)


def _rope_lane_tables(positions):
    half = ROPE_DIM // 2
    inv = ROPE_THETA ** (-jnp.arange(0, ROPE_DIM, 2, dtype=F32) / ROPE_DIM)
    ang = positions.astype(F32).reshape(-1, 1) * inv
    cos, sin = jnp.cos(ang), jnp.sin(ang)
    n = ang.shape[0]
    rest = HEAD_DIM - ROPE_DIM
    ones, zeros, zh = jnp.ones((n, rest), F32), jnp.zeros((n, rest), F32), jnp.zeros((n, half), F32)
    ct = jnp.concatenate([cos, cos, ones], axis=1)
    s1 = jnp.concatenate([-sin, zh, zeros], axis=1)
    s2 = jnp.concatenate([zh, sin, zeros], axis=1)
    return tuple(jnp.tile(t, (1, LANES // HEAD_DIM)) for t in (ct, s1, s2))


def _split_w_in(w_in):
    o_b, o_i, o_c, o_g = A_IN, A_IN + B_IN, A_IN + B_IN + IDX_IN, A_IN + B_IN + IDX_IN + C_IN
    wk = w_in[:, o_i + IDX_Q:o_i + IDX_Q + IDX_DIM]
    ww = w_in[:, o_i + IDX_Q + IDX_DIM:o_c]
    proj = (w_in[:, :o_b], w_in[:, o_b:o_i], w_in[:, o_i:o_i + IDX_Q],
            jnp.concatenate([wk, wk], axis=1),
            jnp.pad(ww, ((0, 0), (0, LANES - IDX_HEADS))),
            w_in[:, o_c:o_g])
    return tuple(a.astype(BF16) for a in proj), w_in[:, o_g:].astype(BF16)


def kernel(x, positions, w_in, w_br_a, w_br_b, w_br_c, w_out, lam_q1, lam_k1, lam_q2, lam_k2, subln_g,
           norm_mix_pre, norm_mix_post, norm_ffn_pre, norm_ffn_post, w_ffn_up, conv_w, conv_b, w_ffn_down):
    b, s, d = x.shape
    depth = w_in.shape[0]
    assert d == D_MODEL and s % A_TILE == 0 and s % SEL_CHUNK == 0 and s % DIFF_TK == 0
    tables = _rope_lane_tables(positions)
    x2 = x.reshape(b * s, d)
    vec = lambda a: a.reshape(1, -1)
    for layer in range(depth):
        lam_init = 0.8 - 0.6 * math.exp(-0.3 * layer)
        lam = (jnp.exp(jnp.sum(lam_q1[layer] * lam_k1[layer])) - jnp.exp(jnp.sum(lam_q2[layer] * lam_k2[layer]))
               + lam_init).reshape(1).astype(F32)
        w_proj, w_gate = _split_w_in(w_in[layer])
        pa, pb, qi, ki, wi, pc = _inproj(x2, vec(norm_mix_pre[layer]), tables, w_proj)
        seq = lambda a: a.reshape(b, s, a.shape[-1])
        oa = _dilated_attention(seq(pa))
        ob = _sparse_attention(seq(qi), seq(wi), seq(ki), seq(pb))
        oc = _diff_attention(seq(pc), lam, vec(subln_g[layer]), lam_init)
        flat = lambda a: a.reshape(b * s, a.shape[-1])
        w_merge = (w_gate, w_br_a[layer].astype(BF16), w_br_b[layer].astype(BF16), w_br_c[layer].astype(BF16),
                   w_out[layer].astype(BF16))
        x2 = _merge(x2, vec(norm_mix_pre[layer]), vec(norm_mix_post[layer]), flat(oa), flat(ob), flat(oc), w_merge)
        x2 = _ffn(x2, vec(norm_ffn_pre[layer]), vec(norm_ffn_post[layer]), w_ffn_up[layer].astype(BF16),
                  conv_w[layer], vec(conv_b[layer]), w_ffn_down[layer].astype(BF16), s)
    return x2.reshape(b, s, d)
```

```python
import functools
import math

import jax
import jax.numpy as jnp
from jax import lax
from jax.experimental import pallas as pl
from jax.experimental.pallas import tpu as pltpu

D_MODEL = 1024
HEAD_DIM = 64
ROPE_THETA = 500000.0
ROPE_DIM = HEAD_DIM // 4
NORM_EPS = 1e-6
Q_BLOCK = 128

A_PAIRS = ((128, 1), (512, 4), (2048, 16))
A_GROUPS = len(A_PAIRS)
A_HEADS = 4
A_OUT = A_HEADS * HEAD_DIM
B_HEADS = 4
B_OUT = B_HEADS * HEAD_DIM
IDX_HEADS = 8
IDX_DIM = 64
TOPK_MAX = 256
C_HEADS = 4
C_VDIM = 2 * HEAD_DIM
C_OUT = C_HEADS * C_VDIM
N_BRANCH = 3

A_IN = A_GROUPS * 3 * A_HEADS * HEAD_DIM
B_IN = 3 * B_HEADS * HEAD_DIM
IDX_Q = IDX_HEADS * IDX_DIM
IDX_IN = IDX_Q + IDX_DIM + IDX_HEADS
C_QK = C_HEADS * 2 * HEAD_DIM
C_IN = 2 * C_QK + C_OUT
D_FF = ((8 * D_MODEL // 3 + 127) // 128) * 128
CONV_WIDTH = 3

LANES = 128
SUBLANES = 8
SCALE = HEAD_DIM ** -0.5
NEG = -1e30
INT_MIN = -2 ** 31
VMEM_LIMIT = 56 * 1024 * 1024

F32 = jnp.float32
BF16 = jnp.bfloat16
NT_DIMS = (((1,), (1,)), ((), ()))


def _nt_dot(a, b):
    return lax.dot_general(a, b, NT_DIMS, preferred_element_type=F32)


def _rms(x, g):
    return x * lax.rsqrt(jnp.mean(x * x, axis=-1, keepdims=True) + NORM_EPS) * g


def _inproj_kernel(x_ref, g_ref, ct_ref, s1_ref, s2_ref, wa_ref, wb_ref, wqi_ref, wki_ref, wwi_ref, wc_ref,
                   pa_ref, pb_ref, qi_ref, ki_ref, wi_ref, pc_ref):
    xn = _rms(x_ref[...], g_ref[...]).astype(BF16)
    ct, s1, s2 = ct_ref[...], s1_ref[...], s2_ref[...]

    def rope(y):
        return y * ct + pltpu.roll(y, LANES - ROPE_DIM // 2, 1) * s1 + pltpu.roll(y, ROPE_DIM // 2, 1) * s2

    def project(w_ref, out_ref, n_cols, mode_of_col):
        for c in range(0, n_cols, 2 * LANES):
            y = jnp.dot(xn, w_ref[:, c:c + 2 * LANES], preferred_element_type=F32)
            for h in range(2):
                col = c + h * LANES
                z = y[:, h * LANES:(h + 1) * LANES]
                mode = mode_of_col(col)
                if mode != "v":
                    z = rope(z)
                if mode == "q":
                    z = z * SCALE
                out_ref[:, col:col + LANES] = z.astype(out_ref.dtype)

    qkv_mode = lambda width: (lambda col: ("q", "k", "v")[(col // width) % 3])
    project(wa_ref, pa_ref, A_IN, qkv_mode(A_OUT))
    project(wb_ref, pb_ref, B_IN, qkv_mode(B_OUT))
    project(wqi_ref, qi_ref, IDX_Q, lambda col: "k")
    project(wc_ref, pc_ref, C_IN, qkv_mode(C_QK))
    ki = jnp.dot(xn, wki_ref[...], preferred_element_type=F32)
    ki_ref[...] = rope(ki).astype(BF16)
    wi = jnp.dot(xn, wwi_ref[...], preferred_element_type=F32)
    wi_ref[...] = (wi * (IDX_HEADS ** -0.5)) * (IDX_DIM ** -0.5)


def _inproj(x2, gain, tables, w, tm=512):
    n = x2.shape[0]
    row = lambda width: pl.BlockSpec((tm, width), lambda i: (i, 0))
    full = lambda a: pl.BlockSpec(a.shape, lambda i: (0, 0))
    outs = [(A_IN, BF16), (B_IN, BF16), (IDX_Q, BF16), (LANES, BF16), (LANES, F32), (C_IN, BF16)]
    return pl.pallas_call(
        _inproj_kernel,
        grid=(n // tm,),
        in_specs=[row(D_MODEL), full(gain), row(LANES), row(LANES), row(LANES)] + [full(a) for a in w],
        out_specs=[row(wd) for wd, _ in outs],
        out_shape=[jax.ShapeDtypeStruct((n, wd), dt) for wd, dt in outs],
        compiler_params=pltpu.CompilerParams(dimension_semantics=("arbitrary",), vmem_limit_bytes=VMEM_LIMIT),
        name="inproj",
    )(x2, gain, *tables, *w)


def _head_masks(width):
    lane = lax.broadcasted_iota(jnp.int32, (1, width), 1)
    return [lane // HEAD_DIM == h for h in range(width // HEAD_DIM)]


def _stack_heads(q):
    zero = jnp.zeros_like(q)
    return jnp.concatenate([jnp.where(hm, q, zero) for hm in _head_masks(q.shape[1])], axis=0)


def _unstack_heads(x, rows):
    hmask = _head_masks(HEAD_DIM * (x.shape[0] // rows))
    out = jnp.where(hmask[0], x[0:rows], 0.0)
    for h in range(1, len(hmask)):
        out = jnp.where(hmask[h], x[h * rows:(h + 1) * rows], out)
    return out


def _lane_tile(x, width):
    reps = width // LANES
    return x if reps == 1 else jnp.concatenate([x] * reps, axis=1)


def _flash_init(m_ref, l_ref, acc_ref):
    m_ref[...] = jnp.full(m_ref.shape, NEG, F32)
    l_ref[...] = jnp.zeros(l_ref.shape, F32)
    acc_ref[...] = jnp.zeros(acc_ref.shape, F32)


def _flash_step(s, v, m_ref, l_ref, acc_ref):
    tk = s.shape[1]
    m_old = m_ref[...]
    m_new = jnp.maximum(m_old, jnp.max(s, axis=1, keepdims=True))
    p = jnp.exp(s - _lane_tile(m_new, tk))
    alpha = jnp.exp(m_old - m_new)
    psum = p[:, :LANES]
    for j in range(1, tk // LANES):
        psum = psum + p[:, j * LANES:(j + 1) * LANES]
    l_ref[...] = alpha * l_ref[...] + psum
    acc_ref[...] = (_lane_tile(alpha, acc_ref.shape[1]) * acc_ref[...]
                    + jnp.dot(p.astype(BF16), v, preferred_element_type=F32))
    m_ref[...] = m_new


def _flash_pipeline(n, logits_fn, values_fn, s_ref, m_ref, l_ref, acc_ref):
    def step(slot, t):
        _flash_step(s_ref[slot], values_fn(t), m_ref, l_ref, acc_ref)

    def pair(u, carry):
        t = 2 * u
        s_ref[1] = logits_fn(t + 1)
        step(0, t)
        s_ref[0] = logits_fn(t + 2)
        step(1, t + 1)
        return carry

    s_ref[0] = logits_fn(0)
    pairs = (n - 1) // 2
    lax.fori_loop(0, pairs, pair, 0)
    last = 2 * pairs

    @pl.when(n - last == 2)
    def _():
        s_ref[1] = logits_fn(last + 1)
        step(0, last)
        step(1, last + 1)

    @pl.when(n - last == 1)
    def _():
        step(0, last)


def _flash_result(l_ref, acc_ref):
    return acc_ref[...] / jnp.sum(l_ref[...], axis=1, keepdims=True)


A_TILE = max(w for w, _ in A_PAIRS)


def _dilated_kernel(*refs):
    group_refs = [refs[5 * g:5 * g + 5] for g in range(A_GROUPS)]
    o_ref, qs_ref, ks_ref, vs_ref, m_sc, l_sc, n_sc = refs[5 * A_GROUPS:]
    j = pl.program_id(1)
    qi = lax.broadcasted_iota(jnp.int32, (Q_BLOCK, 2 * Q_BLOCK), 0)
    kj = lax.broadcasted_iota(jnp.int32, (Q_BLOCK, 2 * Q_BLOCK), 1)
    rel = kj - qi
    band = (rel >= 0) & (rel <= Q_BLOCK)
    bias_band = jnp.where(band, 0.0, NEG)
    bias_first = jnp.where(band & (kj >= Q_BLOCK), 0.0, NEG)
    halves = A_OUT // LANES

    def get(ref, rows):
        return jnp.concatenate([ref[hh, rows, :] for hh in range(halves)], axis=1)

    def put(ref, rows, val):
        for hh in range(halves):
            ref[hh, rows, :] = val[:, hh * LANES:(hh + 1) * LANES]

    for g, ((window, dil), (q_ref, kp_ref, kc_ref, vp_ref, vc_ref)) in enumerate(zip(A_PAIRS, group_refs)):
        assert window == Q_BLOCK * dil
        put(qs_ref, slice(0, A_TILE), q_ref[...].astype(F32))
        put(ks_ref, slice(0, window), kp_ref[...].astype(F32))
        put(ks_ref, slice(window, window + A_TILE), kc_ref[...].astype(F32))
        put(vs_ref, slice(0, window), vp_ref[...].astype(F32))
        put(vs_ref, slice(window, window + A_TILE), vc_ref[...].astype(F32))
        shift = dil.bit_length() - 1

        def problem(p, carry, g=g, window=window, dil=dil, shift=shift):
            sub = p >> shift
            base = sub * window + (p & (dil - 1))
            stride = dil if dil > 1 else None
            q_rows = pl.ds(base, Q_BLOCK, stride=stride)
            k_rows = pl.ds(base, 2 * Q_BLOCK, stride=stride)
            qstack = _stack_heads(get(qs_ref, q_rows).astype(BF16))
            k = get(ks_ref, k_rows).astype(BF16)
            v = get(vs_ref, k_rows).astype(BF16)
            bias = jnp.where((sub > 0) | (j > 0), bias_band, bias_first)
            s = _nt_dot(qstack, k) + jnp.concatenate([bias] * A_HEADS, axis=0)
            m = jnp.max(s, axis=1, keepdims=True)
            e = jnp.exp(s - m)
            l = jnp.sum(e, axis=1, keepdims=True)
            num = jnp.dot(e.astype(BF16), v, preferred_element_type=F32)
            m_n, l_n, num_n = (_unstack_heads(t, Q_BLOCK) for t in (m, l, num))
            if g == 0:
                put(m_sc, q_rows, m_n)
                put(l_sc, q_rows, l_n)
                put(n_sc, q_rows, num_n)
            else:
                m_o = get(m_sc, q_rows)
                m_x = jnp.maximum(m_o, m_n)
                a, b = jnp.exp(m_o - m_x), jnp.exp(m_n - m_x)
                put(m_sc, q_rows, m_x)
                put(l_sc, q_rows, a * get(l_sc, q_rows) + b * l_n)
                put(n_sc, q_rows, a * get(n_sc, q_rows) + b * num_n)
            return carry

        lax.fori_loop(0, A_TILE // Q_BLOCK, problem, 0, unroll=4)

    all_rows = slice(0, A_TILE)
    o_ref[...] = (get(n_sc, all_rows) / get(l_sc, all_rows)).astype(o_ref.dtype)


def _dilated_attention(pa3):
    b, s, _ = pa3.shape
    in_specs = []
    for g, (window, _) in enumerate(A_PAIRS):
        per = A_TILE // window
        cur = lambda bi, j, col: (bi, j, col)
        prev = lambda bi, j, col, per=per: (bi, jnp.maximum(j * per - 1, 0), col)
        in_specs.append(pl.BlockSpec((None, A_TILE, A_OUT), functools.partial(cur, col=3 * g)))
        for col in (3 * g + 1, 3 * g + 2):
            in_specs.append(pl.BlockSpec((None, window, A_OUT), functools.partial(prev, col=col)))
            in_specs.append(pl.BlockSpec((None, A_TILE, A_OUT), functools.partial(cur, col=col)))
    stage = lambda rows: pltpu.VMEM((A_OUT // LANES, rows, LANES), F32)
    return pl.pallas_call(
        _dilated_kernel,
        grid=(b, s // A_TILE),
        in_specs=in_specs,
        out_specs=pl.BlockSpec((None, A_TILE, A_OUT), lambda bi, j: (bi, j, 0)),
        out_shape=jax.ShapeDtypeStruct((b, s, A_OUT), BF16),
        scratch_shapes=[stage(A_TILE), stage(2 * A_TILE), stage(2 * A_TILE),
                        stage(A_TILE), stage(A_TILE), stage(A_TILE)],
        compiler_params=pltpu.CompilerParams(dimension_semantics=("arbitrary", "arbitrary"),
                                             vmem_limit_bytes=VMEM_LIMIT),
        name="dilated_attn",
    )(*([pa3] * (5 * A_GROUPS)))


SEL_TQ = 128
SEL_CHUNK = 512
SCORE_GROUP = 16
ATT_CHUNK = 512


def _sparse_kernel(qi_ref, wi_ref, ki_ref, q_ref, k_ref, v_ref, o_ref, key_ref, keyt_ref, s_ref, m_ref, l_ref,
                   acc_ref, *, topk):
    i = pl.program_id(1)
    qs = i * SEL_TQ
    nch = (qs + SEL_TQ - 1) // SEL_CHUNK + 1
    nat = (qs + SEL_TQ - 1) // ATT_CHUNK + 1
    lane = lax.broadcasted_iota(jnp.int32, (1, LANES), 1)
    qpos = qs + lax.broadcasted_iota(jnp.int32, (SEL_TQ, 1), 0)

    qi = qi_ref[...]
    per_head = []
    for h in range(IDX_HEADS):
        pair = qi[:, (h // 2) * LANES:(h // 2 + 1) * LANES]
        keep = (lane < IDX_DIM) if h % 2 == 0 else (lane >= IDX_DIM)
        per_head.append(jnp.where(keep, pair, jnp.zeros_like(pair)))
    groups = range(0, SEL_TQ, SCORE_GROUP)
    stacked = jnp.concatenate([ph[g:g + SCORE_GROUP] for g in groups for ph in per_head], axis=0)
    w = wi_ref[...]
    wcol = [w[:, h:h + 1] for h in range(IDX_HEADS)]

    def to_key(score):
        bits = lax.bitcast_convert_type(score, jnp.int32)
        return bits ^ ((bits >> 31) & 0x7FFFFFFF)

    def score_chunk(c, group_max):
        start = pl.multiple_of(c * SEL_CHUNK, SEL_CHUNK)
        logits = _nt_dot(stacked, ki_ref[pl.ds(start, SEL_CHUNK), :])
        pieces = []
        for g in groups:
            base = g * IDX_HEADS
            sg = jnp.zeros((SCORE_GROUP, SEL_CHUNK), F32)
            for h in range(IDX_HEADS):
                rows = slice(base + h * SCORE_GROUP, base + (h + 1) * SCORE_GROUP)
                sg = sg + jnp.maximum(logits[rows], 0.0) * wcol[h][g:g + SCORE_GROUP]
            pieces.append(sg)
        score = jnp.concatenate(pieces, axis=0)
        kpos = start + lax.broadcasted_iota(jnp.int32, (1, SEL_CHUNK), 1)
        causal = kpos <= qpos
        key_ref[:, pl.ds(start, SEL_CHUNK)] = jnp.where(causal, to_key(score), INT_MIN)
        score = jnp.where(causal, score, -jnp.inf)
        group_max = list(group_max)
        for jj in range(SEL_CHUNK // LANES):
            group_max[jj % 2] = jnp.maximum(group_max[jj % 2], score[:, jj * LANES:(jj + 1) * LANES])
        return tuple(group_max)

    def transpose_chunk(c):
        start = pl.multiple_of(c * SEL_CHUNK, SEL_CHUNK)
        keyt_ref[pl.ds(start, SEL_CHUNK), :] = key_ref[:, pl.ds(start, SEL_CHUNK)].T

    def score_step(c, group_max):
        transpose_chunk(c - 1)
        return score_chunk(c, group_max)

    neg_inf = jnp.full((SEL_TQ, LANES), -jnp.inf, F32)
    gm_even, gm_odd = lax.fori_loop(1, nch, score_step, score_chunk(0, (neg_inf, neg_inf)))
    transpose_chunk(nch - 1)

    def to_lanes(col):
        return jnp.broadcast_to(col, (SEL_TQ, LANES)).T[0:1, :]

    def to_rows(lanes):
        return jnp.broadcast_to(lanes, (LANES, SEL_TQ)).T[:, 0:1]

    def count(compare, level):
        lanes_acc = 8 * SUBLANES
        def body(c, acc):
            blk = keyt_ref[pl.ds(pl.multiple_of(c * SEL_CHUNK, SEL_CHUNK), SEL_CHUNK), :]
            hit = jnp.where(compare(blk, level), 1.0, 0.0)
            return acc + jnp.sum(hit.reshape(SEL_CHUNK // lanes_acc, lanes_acc, LANES), axis=0)
        acc = lax.fori_loop(0, nch, body, jnp.zeros((lanes_acc, LANES), F32))
        return jnp.sum(acc, axis=0, keepdims=True)

    kf = float(topk)
    assert topk <= 2 * LANES and SEL_TQ == LANES
    searchable = qs >= topk
    lo_f = to_lanes(jnp.minimum(jnp.min(gm_even, axis=1, keepdims=True), jnp.min(gm_odd, axis=1, keepdims=True)))
    hi_f = to_lanes(jnp.maximum(jnp.max(gm_even, axis=1, keepdims=True), jnp.max(gm_odd, axis=1, keepdims=True)))
    lo0 = jnp.where(searchable, to_key(lo_f), INT_MIN + 1)
    hi0 = to_key(hi_f) + 1
    cnt_lo0 = jnp.where(searchable, kf + 1.0, kf) + jnp.zeros((1, LANES), F32)

    def candidate(lo, hi, cnt_lo):
        mid = (lo >> 1) + (hi >> 1) + (lo & hi & 1)
        open_q = (cnt_lo > kf) & (mid > lo)
        cand = jnp.where((lo < 0) & (hi > 0), 0, jnp.where((lo == 0) & (hi > 1), 1, mid))
        open_q = jnp.where(open_q, 1.0, 0.0)
        return cand, open_q, jnp.max(open_q) > 0.0

    def search_step(carry):
        lo, hi, cnt_lo, cnt_hi, cand, open_q, _ = carry
        cnt = count(lambda blk, level: blk >= level, cand)
        up = (open_q > 0.0) & (cnt >= kf)
        down = (open_q > 0.0) & (cnt < kf)
        lo, cnt_lo = jnp.where(up, cand, lo), jnp.where(up, cnt, cnt_lo)
        hi, cnt_hi = jnp.where(down, cand, hi), jnp.where(down, cnt, cnt_hi)
        return (lo, hi, cnt_lo, cnt_hi) + candidate(lo, hi, cnt_lo)

    thr_q, _, cnt_thr, cnt_hi = lax.while_loop(
        lambda carry: carry[6], lambda carry: search_step(search_step(carry)),
        (lo0, hi0, cnt_lo0, jnp.zeros((1, LANES), F32)) + candidate(lo0, hi0, cnt_lo0))[:4]
    has_ties = jnp.max(cnt_thr) > kf
    thr = to_rows(thr_q)

    @pl.when(has_ties)
    def _():
        need = to_rows(jnp.where(cnt_thr > kf, kf - cnt_hi, kf))
        r = lax.broadcasted_iota(jnp.int32, (ATT_CHUNK, ATT_CHUNK), 0)
        cc = lax.broadcasted_iota(jnp.int32, (ATT_CHUNK, ATT_CHUNK), 1)
        before = jnp.where(r < cc, 1.0, 0.0).astype(BF16)

        def body(c, seen):
            cols = pl.ds(pl.multiple_of(c * ATT_CHUNK, ATT_CHUNK), ATT_CHUNK)
            blk = key_ref[:, cols]
            eq = blk == thr
            eqf = jnp.where(eq, 1.0, 0.0)
            rank = seen + jnp.dot(eqf.astype(BF16), before, preferred_element_type=F32)
            key_ref[:, cols] = jnp.where(eq & (rank >= need), INT_MIN, blk)
            return seen + jnp.sum(eqf, axis=1, keepdims=True)
        lax.fori_loop(0, nat, body, jnp.zeros((SEL_TQ, 1), F32))

    qstack = _stack_heads(q_ref[...])

    def logits(c, sel):
        start = pl.multiple_of(c * ATT_CHUNK, ATT_CHUNK)
        bias = jnp.where(sel, 0.0, NEG)
        return _nt_dot(qstack, k_ref[pl.ds(start, ATT_CHUNK), :]) + jnp.concatenate([bias] * B_HEADS, axis=0)

    def values(c):
        return v_ref[pl.ds(pl.multiple_of(c * ATT_CHUNK, ATT_CHUNK), ATT_CHUNK), :]

    def selected(c):
        return key_ref[:, pl.ds(pl.multiple_of(c * ATT_CHUNK, ATT_CHUNK), ATT_CHUNK)] >= thr

    _flash_init(m_ref, l_ref, acc_ref)
    _flash_pipeline(nat, lambda c: logits(c, selected(c)), values, s_ref, m_ref, l_ref, acc_ref)

    o_ref[...] = _unstack_heads(_flash_result(l_ref, acc_ref), SEL_TQ).astype(o_ref.dtype)


def _sparse_attention(qi3, wi3, ki3, pb3):
    b, s, _ = pb3.shape
    topk = min(TOPK_MAX, s // 4)
    rows = B_HEADS * SEL_TQ
    qblk = lambda width, col: pl.BlockSpec((None, SEL_TQ, width), lambda bi, i: (bi, i, col))
    seq = lambda width, col: pl.BlockSpec((None, s, width), lambda bi, i: (bi, 0, col))
    return pl.pallas_call(
        functools.partial(_sparse_kernel, topk=topk),
        grid=(b, s // SEL_TQ),
        in_specs=[qblk(IDX_Q, 0), qblk(LANES, 0), seq(LANES, 0), qblk(B_OUT, 0), seq(B_OUT, 1), seq(B_OUT, 2)],
        out_specs=pl.BlockSpec((None, SEL_TQ, B_OUT), lambda bi, i: (bi, i, 0)),
        out_shape=jax.ShapeDtypeStruct((b, s, B_OUT), BF16),
        scratch_shapes=[pltpu.VMEM((SEL_TQ, s), jnp.int32), pltpu.VMEM((s, SEL_TQ), jnp.int32),
                        pltpu.VMEM((2, rows, ATT_CHUNK), F32),
                        pltpu.VMEM((rows, LANES), F32), pltpu.VMEM((rows, LANES), F32),
                        pltpu.VMEM((rows, B_OUT), F32)],
        compiler_params=pltpu.CompilerParams(dimension_semantics=("arbitrary", "arbitrary"),
                                             vmem_limit_bytes=VMEM_LIMIT),
        name="sparse_attn",
    )(qi3, wi3, ki3, pb3, pb3, pb3)


DIFF_TQ = 512
DIFF_TK = 1024


def _diff_kernel(lam_ref, g_ref, q_ref, k_ref, v_ref, o_ref, s_ref, m_ref, l_ref, acc_ref, *, out_scale):
    i = pl.program_id(2)
    qstack = _stack_heads(q_ref[...])
    _flash_init(m_ref, l_ref, acc_ref)

    def logits(kb):
        return _nt_dot(qstack, k_ref[pl.ds(pl.multiple_of(kb * DIFF_TK, DIFF_TK), DIFF_TK), :])

    def values(kb):
        return v_ref[pl.ds(pl.multiple_of(kb * DIFF_TK, DIFF_TK), DIFF_TK), :]

    n_full = (i * DIFF_TQ) // DIFF_TK

    @pl.when(n_full > 0)
    def _():
        _flash_pipeline(n_full, logits, values, s_ref, m_ref, l_ref, acc_ref)

    def diagonal(width):
        start = pl.multiple_of(n_full * DIFF_TK, DIFF_TK)
        qpos = i * DIFF_TQ + (lax.broadcasted_iota(jnp.int32, (2 * DIFF_TQ, width), 0) & (DIFF_TQ - 1))
        kpos = start + lax.broadcasted_iota(jnp.int32, (2 * DIFF_TQ, width), 1)
        s = _nt_dot(qstack, k_ref[pl.ds(start, width), :]) + jnp.where(kpos <= qpos, 0.0, NEG)
        _flash_step(s, v_ref[pl.ds(start, width), :], m_ref, l_ref, acc_ref)

    assert DIFF_TK == 2 * DIFF_TQ
    first_half = (i * DIFF_TQ) % DIFF_TK == 0
    pl.when(first_half)(lambda: diagonal(DIFF_TQ))
    pl.when(jnp.logical_not(first_half))(lambda: diagonal(DIFF_TK))

    res = _flash_result(l_ref, acc_ref)
    o = res[:DIFF_TQ] - lam_ref[0] * res[DIFF_TQ:]
    o_ref[...] = (_rms(o, g_ref[...]) * out_scale).astype(o_ref.dtype)


def _diff_attention(pc3, lam, subln_g, lam_init):
    b, s, _ = pc3.shape
    nq = C_QK // LANES
    return pl.pallas_call(
        functools.partial(_diff_kernel, out_scale=1.0 - lam_init),
        grid=(b, C_HEADS, s // DIFF_TQ),
        in_specs=[pl.BlockSpec(memory_space=pltpu.SMEM),
                  pl.BlockSpec((1, C_VDIM), lambda bi, h, i: (0, 0)),
                  pl.BlockSpec((None, DIFF_TQ, LANES), lambda bi, h, i: (bi, i, h)),
                  pl.BlockSpec((None, s, LANES), lambda bi, h, i: (bi, 0, nq + h)),
                  pl.BlockSpec((None, s, C_VDIM), lambda bi, h, i: (bi, 0, 2 * nq + h))],
        out_specs=pl.BlockSpec((None, DIFF_TQ, C_VDIM), lambda bi, h, i: (bi, i, h)),
        out_shape=jax.ShapeDtypeStruct((b, s, C_OUT), BF16),
        scratch_shapes=[pltpu.VMEM((2, 2 * DIFF_TQ, DIFF_TK), F32),
                        pltpu.VMEM((2 * DIFF_TQ, LANES), F32), pltpu.VMEM((2 * DIFF_TQ, LANES), F32),
                        pltpu.VMEM((2 * DIFF_TQ, C_VDIM), F32)],
        compiler_params=pltpu.CompilerParams(dimension_semantics=("arbitrary", "arbitrary", "arbitrary"),
                                             vmem_limit_bytes=VMEM_LIMIT),
        name="diff_attn",
    )(lam, subln_g, pc3, pc3, pc3)


def _merge_kernel(x_ref, gpre_ref, gpost_ref, oa_ref, ob_ref, oc_ref, wg_ref, wa_ref, wb_ref, wc_ref, wo_ref,
                  out_ref, y_ref, h_ref):
    x = x_ref[...]
    xn = _rms(x, gpre_ref[...]).astype(BF16)
    branches = ((oa_ref, wa_ref), (ob_ref, wb_ref), (oc_ref, wc_ref))
    cw = 2 * LANES
    for c in range(0, D_MODEL, cw):
        y = jnp.zeros((x.shape[0], cw), F32)
        for j, (o_ref, w_ref) in enumerate(branches):
            logit = jnp.dot(xn, wg_ref[:, j * D_MODEL + c:j * D_MODEL + c + cw], preferred_element_type=F32)
            gate = 1.0 / (1.0 + jnp.exp(-logit))
            y = y + gate * jnp.dot(o_ref[...], w_ref[:, c:c + cw], preferred_element_type=F32)
        y_ref[:, c:c + cw] = y.astype(BF16)
    ssq = jnp.zeros((x.shape[0], 1), F32)
    for c in range(0, D_MODEL, cw):
        h = jnp.dot(y_ref[...], wo_ref[:, c:c + cw], preferred_element_type=F32)
        h_ref[:, c:c + cw] = h
        ssq = ssq + jnp.sum(h * h, axis=1, keepdims=True)
    inv = lax.rsqrt(ssq * (1.0 / D_MODEL) + NORM_EPS)
    out_ref[...] = x + h_ref[...] * inv * gpost_ref[...]


def _merge(x2, gpre, gpost, oa, ob, oc, w, tm=512):
    n = x2.shape[0]
    row = lambda width: pl.BlockSpec((tm, width), lambda i: (i, 0))
    full = lambda a: pl.BlockSpec(a.shape, lambda i: (0, 0))
    return pl.pallas_call(
        _merge_kernel,
        grid=(n // tm,),
        in_specs=[row(D_MODEL), full(gpre), full(gpost), row(A_OUT), row(B_OUT), row(C_OUT)] + [full(a) for a in w],
        out_specs=row(D_MODEL),
        out_shape=jax.ShapeDtypeStruct((n, D_MODEL), F32),
        scratch_shapes=[pltpu.VMEM((tm, D_MODEL), BF16), pltpu.VMEM((tm, D_MODEL), F32)],
        compiler_params=pltpu.CompilerParams(dimension_semantics=("arbitrary",), vmem_limit_bytes=VMEM_LIMIT),
        name="merge",
    )(x2, gpre, gpost, oa, ob, oc, *w)


FFN_CHUNK = 256
CARRY_ROWS = 8


def _ffn_kernel(x_ref, gpre_ref, gpost_ref, wup_ref, cw_ref, cb_ref, wdn_ref, out_ref, carry_ref, act_ref,
                *, tiles_per_seq):
    x = x_ref[...]
    tm = x.shape[0]
    xn = _rms(x, gpre_ref[...]).astype(BF16)
    first = pl.program_id(0) % tiles_per_seq == 0
    row = lax.broadcasted_iota(jnp.int32, (tm, 1), 0)
    assert CONV_WIDTH == 3 and CONV_WIDTH - 1 <= CARRY_ROWS

    def conv(col):
        h = jnp.dot(xn, wup_ref[:, col:col + FFN_CHUNK], preferred_element_type=F32)
        prev = jnp.where(first, 0.0, carry_ref[:, col:col + FFN_CHUNK])
        carry_ref[:, col:col + FFN_CHUNK] = h[tm - CARRY_ROWS:, :]
        p1 = prev[CARRY_ROWS - 1:CARRY_ROWS, :]
        p2 = prev[CARRY_ROWS - 2:CARRY_ROWS - 1, :]
        h1 = jnp.where(row == 0, p1, pltpu.roll(h, 1, 0))
        h2 = jnp.where(row == 0, p2, jnp.where(row == 1, p1, pltpu.roll(h, 2, 0)))
        w = cw_ref[:, col:col + FFN_CHUNK]
        return h2 * w[0:1, :] + h1 * w[1:2, :] + h * w[2:3, :] + cb_ref[:, col:col + FFN_CHUNK]

    for c in range(0, D_FF, FFN_CHUNK):
        g = conv(c)
        u = conv(D_FF + c)
        gelu = 0.5 * g * (1.0 + jnp.tanh(math.sqrt(2.0 / math.pi) * (g + 0.044715 * (g * g * g))))
        act_ref[:, c:c + FFN_CHUNK] = (gelu * u).astype(BF16)
    h = jnp.dot(act_ref[...], wdn_ref[...], preferred_element_type=F32)
    out_ref[...] = x + _rms(h, gpost_ref[...])


def _ffn(x2, gpre, gpost, wup, conv_w, conv_b, wdn, seq_len, tm=512):
    n = x2.shape[0]
    row = lambda width: pl.BlockSpec((tm, width), lambda i: (i, 0))
    full = lambda a: pl.BlockSpec(a.shape, lambda i: (0, 0))
    return pl.pallas_call(
        functools.partial(_ffn_kernel, tiles_per_seq=seq_len // tm),
        grid=(n // tm,),
        in_specs=[row(D_MODEL), full(gpre), full(gpost), full(wup), full(conv_w), full(conv_b), full(wdn)],
        out_specs=row(D_MODEL),
        out_shape=jax.ShapeDtypeStruct((n, D_MODEL), F32),
        scratch_shapes=[pltpu.VMEM((CARRY_ROWS, 2 * D_FF), F32), pltpu.VMEM((tm, D_FF), BF16)],
        compiler_params=pltpu.CompilerParams(dimension_semantics=("arbitrary",), vmem_limit_bytes=VMEM_LIMIT),
        name="conv_ffn",
    )(x2, gpre, gpost, wup, conv_w, conv_b, wdn)


def _rope_lane_tables(positions):
    half = ROPE_DIM // 2
    inv = ROPE_THETA ** (-jnp.arange(0, ROPE_DIM, 2, dtype=F32) / ROPE_DIM)
    ang = positions.astype(F32).reshape(-1, 1) * inv
    cos, sin = jnp.cos(ang), jnp.sin(ang)
    n = ang.shape[0]
    rest = HEAD_DIM - ROPE_DIM
    ones, zeros, zh = jnp.ones((n, rest), F32), jnp.zeros((n, rest), F32), jnp.zeros((n, half), F32)
    ct = jnp.concatenate([cos, cos, ones], axis=1)
    s1 = jnp.concatenate([-sin, zh, zeros], axis=1)
    s2 = jnp.concatenate([zh, sin, zeros], axis=1)
    return tuple(jnp.tile(t, (1, LANES // HEAD_DIM)) for t in (ct, s1, s2))


def _split_w_in(w_in):
    o_b, o_i, o_c, o_g = A_IN, A_IN + B_IN, A_IN + B_IN + IDX_IN, A_IN + B_IN + IDX_IN + C_IN
    wk = w_in[:, o_i + IDX_Q:o_i + IDX_Q + IDX_DIM]
    ww = w_in[:, o_i + IDX_Q + IDX_DIM:o_c]
    proj = (w_in[:, :o_b], w_in[:, o_b:o_i], w_in[:, o_i:o_i + IDX_Q],
            jnp.concatenate([wk, wk], axis=1),
            jnp.pad(ww, ((0, 0), (0, LANES - IDX_HEADS))),
            w_in[:, o_c:o_g])
    return tuple(a.astype(BF16) for a in proj), w_in[:, o_g:].astype(BF16)


def kernel(x, positions, w_in, w_br_a, w_br_b, w_br_c, w_out, lam_q1, lam_k1, lam_q2, lam_k2, subln_g,
           norm_mix_pre, norm_mix_post, norm_ffn_pre, norm_ffn_post, w_ffn_up, conv_w, conv_b, w_ffn_down):
    b, s, d = x.shape
    depth = w_in.shape[0]
    assert d == D_MODEL and s % A_TILE == 0 and s % SEL_CHUNK == 0 and s % DIFF_TK == 0
    tables = _rope_lane_tables(positions)
    x2 = x.reshape(b * s, d)
    vec = lambda a: a.reshape(1, -1)
    for layer in range(depth):
        lam_init = 0.8 - 0.6 * math.exp(-0.3 * layer)
        lam = (jnp.exp(jnp.sum(lam_q1[layer] * lam_k1[layer])) - jnp.exp(jnp.sum(lam_q2[layer] * lam_k2[layer]))
               + lam_init).reshape(1).astype(F32)
        w_proj, w_gate = _split_w_in(w_in[layer])
        pa, pb, qi, ki, wi, pc = _inproj(x2, vec(norm_mix_pre[layer]), tables, w_proj)
        seq = lambda a: a.reshape(b, s, a.shape[-1])
        oa = _dilated_attention(seq(pa))
        ob = _sparse_attention(seq(qi), seq(wi), seq(ki), seq(pb))
        oc = _diff_attention(seq(pc), lam, vec(subln_g[layer]), lam_init)
        flat = lambda a: a.reshape(b * s, a.shape[-1])
        w_merge = (w_gate, w_br_a[layer].astype(BF16), w_br_b[layer].astype(BF16), w_br_c[layer].astype(BF16),
                   w_out[layer].astype(BF16))
        x2 = _merge(x2, vec(norm_mix_pre[layer]), vec(norm_mix_post[layer]), flat(oa), flat(ob), flat(oc), w_merge)
        x2 = _ffn(x2, vec(norm_ffn_pre[layer]), vec(norm_ffn_post[layer]), w_ffn_up[layer].astype(BF16),
                  conv_w[layer], vec(conv_b[layer]), w_ffn_down[layer].astype(BF16), s)
    return x2.reshape(b, s, d)
```

```python
import functools
import math

import jax
import jax.numpy as jnp
from jax import lax
from jax.experimental import pallas as pl
from jax.experimental.pallas import tpu as pltpu

D_MODEL = 1024
HEAD_DIM = 64
ROPE_THETA = 500000.0
ROPE_DIM = HEAD_DIM // 4
NORM_EPS = 1e-6
Q_BLOCK = 128

A_PAIRS = ((128, 1), (512, 4), (2048, 16))
A_GROUPS = len(A_PAIRS)
A_HEADS = 4
A_OUT = A_HEADS * HEAD_DIM
B_HEADS = 4
B_OUT = B_HEADS * HEAD_DIM
IDX_HEADS = 8
IDX_DIM = 64
TOPK_MAX = 256
C_HEADS = 4
C_VDIM = 2 * HEAD_DIM
C_OUT = C_HEADS * C_VDIM
N_BRANCH = 3

A_IN = A_GROUPS * 3 * A_HEADS * HEAD_DIM
B_IN = 3 * B_HEADS * HEAD_DIM
IDX_Q = IDX_HEADS * IDX_DIM
IDX_IN = IDX_Q + IDX_DIM + IDX_HEADS
C_QK = C_HEADS * 2 * HEAD_DIM
C_IN = 2 * C_QK + C_OUT
D_FF = ((8 * D_MODEL // 3 + 127) // 128) * 128
CONV_WIDTH = 3

LANES = 128
SUBLANES = 8
SCALE = HEAD_DIM ** -0.5
NEG = -1e30
INT_MIN = -2 ** 31
VMEM_LIMIT = 56 * 1024 * 1024

F32 = jnp.float32
BF16 = jnp.bfloat16
NT_DIMS = (((1,), (1,)), ((), ()))


def _nt_dot(a, b):
    return lax.dot_general(a, b, NT_DIMS, preferred_element_type=F32)


def _rms(x, g):
    return x * lax.rsqrt(jnp.mean(x * x, axis=-1, keepdims=True) + NORM_EPS) * g


def _inproj_kernel(x_ref, g_ref, ct_ref, s1_ref, s2_ref, wa_ref, wb_ref, wqi_ref, wki_ref, wwi_ref, wc_ref,
                   pa_ref, pb_ref, qi_ref, ki_ref, wi_ref, pc_ref):
    xn = _rms(x_ref[...], g_ref[...]).astype(BF16)
    ct, s1, s2 = ct_ref[...], s1_ref[...], s2_ref[...]

    def rope(y):
        return y * ct + pltpu.roll(y, LANES - ROPE_DIM // 2, 1) * s1 + pltpu.roll(y, ROPE_DIM // 2, 1) * s2

    def project(w_ref, out_ref, n_cols, mode_of_col):
        for c in range(0, n_cols, 2 * LANES):
            y = jnp.dot(xn, w_ref[:, c:c + 2 * LANES], preferred_element_type=F32)
            for h in range(2):
                col = c + h * LANES
                z = y[:, h * LANES:(h + 1) * LANES]
                mode = mode_of_col(col)
                if mode != "v":
                    z = rope(z)
                if mode == "q":
                    z = z * SCALE
                out_ref[:, col:col + LANES] = z.astype(out_ref.dtype)

    qkv_mode = lambda width: (lambda col: ("q", "k", "v")[(col // width) % 3])
    project(wa_ref, pa_ref, A_IN, qkv_mode(A_OUT))
    project(wb_ref, pb_ref, B_IN, qkv_mode(B_OUT))
    project(wqi_ref, qi_ref, IDX_Q, lambda col: "k")
    project(wc_ref, pc_ref, C_IN, qkv_mode(C_QK))
    ki = jnp.dot(xn, wki_ref[...], preferred_element_type=F32)
    ki_ref[...] = rope(ki).astype(BF16)
    wi = jnp.dot(xn, wwi_ref[...], preferred_element_type=F32)
    wi_ref[...] = (wi * (IDX_HEADS ** -0.5)) * (IDX_DIM ** -0.5)


def _inproj(x2, gain, tables, w, tm=512):
    n = x2.shape[0]
    row = lambda width: pl.BlockSpec((tm, width), lambda i: (i, 0))
    full = lambda a: pl.BlockSpec(a.shape, lambda i: (0, 0))
    outs = [(A_IN, BF16), (B_IN, BF16), (IDX_Q, BF16), (LANES, BF16), (LANES, F32), (C_IN, BF16)]
    return pl.pallas_call(
        _inproj_kernel,
        grid=(n // tm,),
        in_specs=[row(D_MODEL), full(gain), row(LANES), row(LANES), row(LANES)] + [full(a) for a in w],
        out_specs=[row(wd) for wd, _ in outs],
        out_shape=[jax.ShapeDtypeStruct((n, wd), dt) for wd, dt in outs],
        compiler_params=pltpu.CompilerParams(dimension_semantics=("arbitrary",), vmem_limit_bytes=VMEM_LIMIT),
        name="inproj",
    )(x2, gain, *tables, *w)


def _head_masks(width):
    lane = lax.broadcasted_iota(jnp.int32, (1, width), 1)
    return [lane // HEAD_DIM == h for h in range(width // HEAD_DIM)]


def _stack_heads(q):
    zero = jnp.zeros_like(q)
    return jnp.concatenate([jnp.where(hm, q, zero) for hm in _head_masks(q.shape[1])], axis=0)


def _unstack_heads(x, rows):
    hmask = _head_masks(HEAD_DIM * (x.shape[0] // rows))
    out = jnp.where(hmask[0], x[0:rows], 0.0)
    for h in range(1, len(hmask)):
        out = jnp.where(hmask[h], x[h * rows:(h + 1) * rows], out)
    return out


def _lane_tile(x, width):
    reps = width // LANES
    return x if reps == 1 else jnp.concatenate([x] * reps, axis=1)


def _flash_init(m_ref, l_ref, acc_ref):
    m_ref[...] = jnp.full(m_ref.shape, NEG, F32)
    l_ref[...] = jnp.zeros(l_ref.shape, F32)
    acc_ref[...] = jnp.zeros(acc_ref.shape, F32)


def _flash_step(s, v, m_ref, l_ref, acc_ref):
    tk = s.shape[1]
    m_old = m_ref[...]
    m_new = jnp.maximum(m_old, jnp.max(s, axis=1, keepdims=True))
    p = jnp.exp(s - _lane_tile(m_new, tk))
    alpha = jnp.exp(m_old - m_new)
    psum = p[:, :LANES]
    for j in range(1, tk // LANES):
        psum = psum + p[:, j * LANES:(j + 1) * LANES]
    l_ref[...] = alpha * l_ref[...] + psum
    acc_ref[...] = (_lane_tile(alpha, acc_ref.shape[1]) * acc_ref[...]
                    + jnp.dot(p.astype(BF16), v, preferred_element_type=F32))
    m_ref[...] = m_new


def _flash_pipeline(n, logits_fn, values_fn, s_ref, m_ref, l_ref, acc_ref):
    def step(slot, t):
        _flash_step(s_ref[slot], values_fn(t), m_ref, l_ref, acc_ref)

    def pair(u, carry):
        t = 2 * u
        s_ref[1] = logits_fn(t + 1)
        step(0, t)
        s_ref[0] = logits_fn(t + 2)
        step(1, t + 1)
        return carry

    s_ref[0] = logits_fn(0)
    pairs = (n - 1) // 2
    lax.fori_loop(0, pairs, pair, 0)
    last = 2 * pairs

    @pl.when(n - last == 2)
    def _():
        s_ref[1] = logits_fn(last + 1)
        step(0, last)
        step(1, last + 1)

    @pl.when(n - last == 1)
    def _():
        step(0, last)


def _flash_result(l_ref, acc_ref):
    return acc_ref[...] / jnp.sum(l_ref[...], axis=1, keepdims=True)


A_TILE = max(w for w, _ in A_PAIRS)


def _dilated_kernel(*refs):
    group_refs = [refs[5 * g:5 * g + 5] for g in range(A_GROUPS)]
    o_ref, qs_ref, ks_ref, vs_ref, m_sc, l_sc, n_sc = refs[5 * A_GROUPS:]
    j = pl.program_id(1)
    qi = lax.broadcasted_iota(jnp.int32, (Q_BLOCK, 2 * Q_BLOCK), 0)
    kj = lax.broadcasted_iota(jnp.int32, (Q_BLOCK, 2 * Q_BLOCK), 1)
    rel = kj - qi
    band = (rel >= 0) & (rel <= Q_BLOCK)
    bias_band = jnp.where(band, 0.0, NEG)
    bias_first = jnp.where(band & (kj >= Q_BLOCK), 0.0, NEG)
    halves = A_OUT // LANES

    def get(ref, rows):
        return jnp.concatenate([ref[hh, rows, :] for hh in range(halves)], axis=1)

    def put(ref, rows, val):
        for hh in range(halves):
            ref[hh, rows, :] = val[:, hh * LANES:(hh + 1) * LANES]

    for g, ((window, dil), (q_ref, kp_ref, kc_ref, vp_ref, vc_ref)) in enumerate(zip(A_PAIRS, group_refs)):
        assert window == Q_BLOCK * dil
        put(qs_ref, slice(0, A_TILE), q_ref[...].astype(F32))
        put(ks_ref, slice(0, window), kp_ref[...].astype(F32))
        put(ks_ref, slice(window, window + A_TILE), kc_ref[...].astype(F32))
        put(vs_ref, slice(0, window), vp_ref[...].astype(F32))
        put(vs_ref, slice(window, window + A_TILE), vc_ref[...].astype(F32))
        shift = dil.bit_length() - 1

        def problem(p, carry, g=g, window=window, dil=dil, shift=shift):
            sub = p >> shift
            base = sub * window + (p & (dil - 1))
            stride = dil if dil > 1 else None
            q_rows = pl.ds(base, Q_BLOCK, stride=stride)
            k_rows = pl.ds(base, 2 * Q_BLOCK, stride=stride)
            qstack = _stack_heads(get(qs_ref, q_rows).astype(BF16))
            k = get(ks_ref, k_rows).astype(BF16)
            v = get(vs_ref, k_rows).astype(BF16)
            bias = jnp.where((sub > 0) | (j > 0), bias_band, bias_first)
            s = _nt_dot(qstack, k) + jnp.concatenate([bias] * A_HEADS, axis=0)
            m = jnp.max(s, axis=1, keepdims=True)
            e = jnp.exp(s - m)
            l = jnp.sum(e, axis=1, keepdims=True)
            num = jnp.dot(e.astype(BF16), v, preferred_element_type=F32)
            m_n, l_n, num_n = (_unstack_heads(t, Q_BLOCK) for t in (m, l, num))
            if g == 0:
                put(m_sc, q_rows, m_n)
                put(l_sc, q_rows, l_n)
                put(n_sc, q_rows, num_n)
            else:
                m_o = get(m_sc, q_rows)
                m_x = jnp.maximum(m_o, m_n)
                a, b = jnp.exp(m_o - m_x), jnp.exp(m_n - m_x)
                put(m_sc, q_rows, m_x)
                put(l_sc, q_rows, a * get(l_sc, q_rows) + b * l_n)
                put(n_sc, q_rows, a * get(n_sc, q_rows) + b * num_n)
            return carry

        lax.fori_loop(0, A_TILE // Q_BLOCK, problem, 0, unroll=4)

    all_rows = slice(0, A_TILE)
    o_ref[...] = (get(n_sc, all_rows) / get(l_sc, all_rows)).astype(o_ref.dtype)


def _dilated_attention(pa3):
    b, s, _ = pa3.shape
    in_specs = []
    for g, (window, _) in enumerate(A_PAIRS):
        per = A_TILE // window
        cur = lambda bi, j, col: (bi, j, col)
        prev = lambda bi, j, col, per=per: (bi, jnp.maximum(j * per - 1, 0), col)
        in_specs.append(pl.BlockSpec((None, A_TILE, A_OUT), functools.partial(cur, col=3 * g)))
        for col in (3 * g + 1, 3 * g + 2):
            in_specs.append(pl.BlockSpec((None, window, A_OUT), functools.partial(prev, col=col)))
            in_specs.append(pl.BlockSpec((None, A_TILE, A_OUT), functools.partial(cur, col=col)))
    stage = lambda rows: pltpu.VMEM((A_OUT // LANES, rows, LANES), F32)
    return pl.pallas_call(
        _dilated_kernel,
        grid=(b, s // A_TILE),
        in_specs=in_specs,
        out_specs=pl.BlockSpec((None, A_TILE, A_OUT), lambda bi, j: (bi, j, 0)),
        out_shape=jax.ShapeDtypeStruct((b, s, A_OUT), BF16),
        scratch_shapes=[stage(A_TILE), stage(2 * A_TILE), stage(2 * A_TILE),
                        stage(A_TILE), stage(A_TILE), stage(A_TILE)],
        compiler_params=pltpu.CompilerParams(dimension_semantics=("arbitrary", "arbitrary"),
                                             vmem_limit_bytes=VMEM_LIMIT),
        name="dilated_attn",
    )(*([pa3] * (5 * A_GROUPS)))


SEL_TQ = 256
SEL_CHUNK = 512
SCORE_GROUP = 16
ATT_CHUNK = 512


def _sparse_kernel(qi_ref, wi_ref, ki_ref, q_ref, k_ref, v_ref, o_ref, key_ref, keyt_ref, s_ref, m_ref, l_ref,
                   acc_ref, *, topk):
    i = pl.program_id(1)
    qs = i * SEL_TQ
    nch = (qs + SEL_TQ - 1) // SEL_CHUNK + 1
    nat = (qs + SEL_TQ - 1) // ATT_CHUNK + 1
    lane = lax.broadcasted_iota(jnp.int32, (1, LANES), 1)
    qpos = qs + lax.broadcasted_iota(jnp.int32, (SEL_TQ, 1), 0)

    qi = qi_ref[...]
    per_head = []
    for h in range(IDX_HEADS):
        pair = qi[:, (h // 2) * LANES:(h // 2 + 1) * LANES]
        keep = (lane < IDX_DIM) if h % 2 == 0 else (lane >= IDX_DIM)
        per_head.append(jnp.where(keep, pair, jnp.zeros_like(pair)))
    groups = range(0, SEL_TQ, SCORE_GROUP)
    stacked = jnp.concatenate([ph[g:g + SCORE_GROUP] for g in groups for ph in per_head], axis=0)
    w = wi_ref[...]
    wcol = [w[:, h:h + 1] for h in range(IDX_HEADS)]

    def to_key(score):
        bits = lax.bitcast_convert_type(score, jnp.int32)
        return bits ^ ((bits >> 31) & 0x7FFFFFFF)

    def score_chunk(c, group_max):
        start = pl.multiple_of(c * SEL_CHUNK, SEL_CHUNK)
        logits = _nt_dot(stacked, ki_ref[pl.ds(start, SEL_CHUNK), :])
        pieces = []
        for g in groups:
            base = g * IDX_HEADS
            sg = jnp.zeros((SCORE_GROUP, SEL_CHUNK), F32)
            for h in range(IDX_HEADS):
                rows = slice(base + h * SCORE_GROUP, base + (h + 1) * SCORE_GROUP)
                sg = sg + jnp.maximum(logits[rows], 0.0) * wcol[h][g:g + SCORE_GROUP]
            pieces.append(sg)
        score = jnp.concatenate(pieces, axis=0)
        kpos = start + lax.broadcasted_iota(jnp.int32, (1, SEL_CHUNK), 1)
        causal = kpos <= qpos
        key_ref[:, pl.ds(start, SEL_CHUNK)] = jnp.where(causal, to_key(score), INT_MIN)
        score = jnp.where(causal, score, -jnp.inf)
        group_max = list(group_max)
        for jj in range(SEL_CHUNK // LANES):
            group_max[jj % 2] = jnp.maximum(group_max[jj % 2], score[:, jj * LANES:(jj + 1) * LANES])
        return tuple(group_max)

    def transpose_chunk(c):
        start = pl.multiple_of(c * SEL_CHUNK, SEL_CHUNK)
        keyt_ref[pl.ds(start, SEL_CHUNK), :] = key_ref[:, pl.ds(start, SEL_CHUNK)].T

    def score_step(c, group_max):
        transpose_chunk(c - 1)
        return score_chunk(c, group_max)

    neg_inf = jnp.full((SEL_TQ, LANES), -jnp.inf, F32)
    gm_even, gm_odd = lax.fori_loop(1, nch, score_step, score_chunk(0, (neg_inf, neg_inf)))
    transpose_chunk(nch - 1)

    def to_lanes(col):
        return jnp.broadcast_to(col, (SEL_TQ, LANES)).T[0:1, :]

    def to_rows(lanes):
        return jnp.broadcast_to(lanes, (LANES, SEL_TQ)).T[:, 0:1]

    def count(compare, level):
        lanes_acc = 8 * SUBLANES
        def body(c, acc):
            blk = keyt_ref[pl.ds(pl.multiple_of(c * SEL_CHUNK, SEL_CHUNK), SEL_CHUNK), :]
            hit = jnp.where(compare(blk, level), 1.0, 0.0)
            return acc + jnp.sum(hit.reshape(SEL_CHUNK // lanes_acc, lanes_acc, SEL_TQ), axis=0)
        acc = lax.fori_loop(0, nch, body, jnp.zeros((lanes_acc, SEL_TQ), F32))
        return jnp.sum(acc, axis=0, keepdims=True)

    kf = float(topk)
    assert topk <= 2 * LANES and SEL_TQ % LANES == 0
    searchable = qs >= topk
    lo_f = to_lanes(jnp.minimum(jnp.min(gm_even, axis=1, keepdims=True), jnp.min(gm_odd, axis=1, keepdims=True)))
    hi_f = to_lanes(jnp.maximum(jnp.max(gm_even, axis=1, keepdims=True), jnp.max(gm_odd, axis=1, keepdims=True)))
    lo0 = jnp.where(searchable, to_key(lo_f), INT_MIN + 1)
    hi0 = to_key(hi_f) + 1
    cnt_lo0 = jnp.where(searchable, kf + 1.0, kf) + jnp.zeros((1, SEL_TQ), F32)

    def candidate(lo, hi, cnt_lo):
        mid = (lo >> 1) + (hi >> 1) + (lo & hi & 1)
        open_q = (cnt_lo > kf) & (mid > lo)
        cand = jnp.where((lo < 0) & (hi > 0), 0, jnp.where((lo == 0) & (hi > 1), 1, mid))
        open_q = jnp.where(open_q, 1.0, 0.0)
        return cand, open_q, jnp.max(open_q) > 0.0

    def search_step(carry):
        lo, hi, cnt_lo, cnt_hi, cand, open_q, _ = carry
        cnt = count(lambda blk, level: blk >= level, cand)
        up = (open_q > 0.0) & (cnt >= kf)
        down = (open_q > 0.0) & (cnt < kf)
        lo, cnt_lo = jnp.where(up, cand, lo), jnp.where(up, cnt, cnt_lo)
        hi, cnt_hi = jnp.where(down, cand, hi), jnp.where(down, cnt, cnt_hi)
        return (lo, hi, cnt_lo, cnt_hi) + candidate(lo, hi, cnt_lo)

    thr_q, _, cnt_thr, cnt_hi = lax.while_loop(
        lambda carry: carry[6], lambda carry: search_step(search_step(carry)),
        (lo0, hi0, cnt_lo0, jnp.zeros((1, SEL_TQ), F32)) + candidate(lo0, hi0, cnt_lo0))[:4]
    has_ties = jnp.max(cnt_thr) > kf
    thr = to_rows(thr_q)

    @pl.when(has_ties)
    def _():
        need = to_rows(jnp.where(cnt_thr > kf, kf - cnt_hi, kf))
        r = lax.broadcasted_iota(jnp.int32, (ATT_CHUNK, ATT_CHUNK), 0)
        cc = lax.broadcasted_iota(jnp.int32, (ATT_CHUNK, ATT_CHUNK), 1)
        before = jnp.where(r < cc, 1.0, 0.0).astype(BF16)

        def body(c, seen):
            cols = pl.ds(pl.multiple_of(c * ATT_CHUNK, ATT_CHUNK), ATT_CHUNK)
            blk = key_ref[:, cols]
            eq = blk == thr
            eqf = jnp.where(eq, 1.0, 0.0)
            rank = seen + jnp.dot(eqf.astype(BF16), before, preferred_element_type=F32)
            key_ref[:, cols] = jnp.where(eq & (rank >= need), INT_MIN, blk)
            return seen + jnp.sum(eqf, axis=1, keepdims=True)
        lax.fori_loop(0, nat, body, jnp.zeros((SEL_TQ, 1), F32))

    qstack = _stack_heads(q_ref[...])

    def logits(c, sel):
        start = pl.multiple_of(c * ATT_CHUNK, ATT_CHUNK)
        bias = jnp.where(sel, 0.0, NEG)
        return _nt_dot(qstack, k_ref[pl.ds(start, ATT_CHUNK), :]) + jnp.concatenate([bias] * B_HEADS, axis=0)

    def values(c):
        return v_ref[pl.ds(pl.multiple_of(c * ATT_CHUNK, ATT_CHUNK), ATT_CHUNK), :]

    def selected(c):
        return key_ref[:, pl.ds(pl.multiple_of(c * ATT_CHUNK, ATT_CHUNK), ATT_CHUNK)] >= thr

    _flash_init(m_ref, l_ref, acc_ref)
    _flash_pipeline(nat, lambda c: logits(c, selected(c)), values, s_ref, m_ref, l_ref, acc_ref)

    o_ref[...] = _unstack_heads(_flash_result(l_ref, acc_ref), SEL_TQ).astype(o_ref.dtype)


def _sparse_attention(qi3, wi3, ki3, pb3):
    b, s, _ = pb3.shape
    topk = min(TOPK_MAX, s // 4)
    rows = B_HEADS * SEL_TQ
    qblk = lambda width, col: pl.BlockSpec((None, SEL_TQ, width), lambda bi, i: (bi, i, col))
    seq = lambda width, col: pl.BlockSpec((None, s, width), lambda bi, i: (bi, 0, col))
    return pl.pallas_call(
        functools.partial(_sparse_kernel, topk=topk),
        grid=(b, s // SEL_TQ),
        in_specs=[qblk(IDX_Q, 0), qblk(LANES, 0), seq(LANES, 0), qblk(B_OUT, 0), seq(B_OUT, 1), seq(B_OUT, 2)],
        out_specs=pl.BlockSpec((None, SEL_TQ, B_OUT), lambda bi, i: (bi, i, 0)),
        out_shape=jax.ShapeDtypeStruct((b, s, B_OUT), BF16),
        scratch_shapes=[pltpu.VMEM((SEL_TQ, s), jnp.int32), pltpu.VMEM((s, SEL_TQ), jnp.int32),
                        pltpu.VMEM((2, rows, ATT_CHUNK), F32),
                        pltpu.VMEM((rows, LANES), F32), pltpu.VMEM((rows, LANES), F32),
                        pltpu.VMEM((rows, B_OUT), F32)],
        compiler_params=pltpu.CompilerParams(dimension_semantics=("arbitrary", "arbitrary"),
                                             vmem_limit_bytes=VMEM_LIMIT),
        name="sparse_attn",
    )(qi3, wi3, ki3, pb3, pb3, pb3)


DIFF_TQ = 512
DIFF_TK = 1024


def _diff_kernel(lam_ref, g_ref, q_ref, k_ref, v_ref, o_ref, s_ref, m_ref, l_ref, acc_ref, *, out_scale):
    i = pl.program_id(2)
    qstack = _stack_heads(q_ref[...])
    _flash_init(m_ref, l_ref, acc_ref)

    def logits(kb):
        return _nt_dot(qstack, k_ref[pl.ds(pl.multiple_of(kb * DIFF_TK, DIFF_TK), DIFF_TK), :])

    def values(kb):
        return v_ref[pl.ds(pl.multiple_of(kb * DIFF_TK, DIFF_TK), DIFF_TK), :]

    n_full = (i * DIFF_TQ) // DIFF_TK

    @pl.when(n_full > 0)
    def _():
        _flash_pipeline(n_full, logits, values, s_ref, m_ref, l_ref, acc_ref)

    def diagonal(width):
        start = pl.multiple_of(n_full * DIFF_TK, DIFF_TK)
        qpos = i * DIFF_TQ + (lax.broadcasted_iota(jnp.int32, (2 * DIFF_TQ, width), 0) & (DIFF_TQ - 1))
        kpos = start + lax.broadcasted_iota(jnp.int32, (2 * DIFF_TQ, width), 1)
        s = _nt_dot(qstack, k_ref[pl.ds(start, width), :]) + jnp.where(kpos <= qpos, 0.0, NEG)
        _flash_step(s, v_ref[pl.ds(start, width), :], m_ref, l_ref, acc_ref)

    assert DIFF_TK == 2 * DIFF_TQ
    first_half = (i * DIFF_TQ) % DIFF_TK == 0
    pl.when(first_half)(lambda: diagonal(DIFF_TQ))
    pl.when(jnp.logical_not(first_half))(lambda: diagonal(DIFF_TK))

    res = _flash_result(l_ref, acc_ref)
    o = res[:DIFF_TQ] - lam_ref[0] * res[DIFF_TQ:]
    o_ref[...] = (_rms(o, g_ref[...]) * out_scale).astype(o_ref.dtype)


def _diff_attention(pc3, lam, subln_g, lam_init):
    b, s, _ = pc3.shape
    nq = C_QK // LANES
    return pl.pallas_call(
        functools.partial(_diff_kernel, out_scale=1.0 - lam_init),
        grid=(b, C_HEADS, s // DIFF_TQ),
        in_specs=[pl.BlockSpec(memory_space=pltpu.SMEM),
                  pl.BlockSpec((1, C_VDIM), lambda bi, h, i: (0, 0)),
                  pl.BlockSpec((None, DIFF_TQ, LANES), lambda bi, h, i: (bi, i, h)),
                  pl.BlockSpec((None, s, LANES), lambda bi, h, i: (bi, 0, nq + h)),
                  pl.BlockSpec((None, s, C_VDIM), lambda bi, h, i: (bi, 0, 2 * nq + h))],
        out_specs=pl.BlockSpec((None, DIFF_TQ, C_VDIM), lambda bi, h, i: (bi, i, h)),
        out_shape=jax.ShapeDtypeStruct((b, s, C_OUT), BF16),
        scratch_shapes=[pltpu.VMEM((2, 2 * DIFF_TQ, DIFF_TK), F32),
                        pltpu.VMEM((2 * DIFF_TQ, LANES), F32), pltpu.VMEM((2 * DIFF_TQ, LANES), F32),
                        pltpu.VMEM((2 * DIFF_TQ, C_VDIM), F32)],
        compiler_params=pltpu.CompilerParams(dimension_semantics=("arbitrary", "arbitrary", "arbitrary"),
                                             vmem_limit_bytes=VMEM_LIMIT),
        name="diff_attn",
    )(lam, subln_g, pc3, pc3, pc3)


def _merge_kernel(x_ref, gpre_ref, gpost_ref, oa_ref, ob_ref, oc_ref, wg_ref, wa_ref, wb_ref, wc_ref, wo_ref,
                  out_ref, y_ref, h_ref):
    x = x_ref[...]
    xn = _rms(x, gpre_ref[...]).astype(BF16)
    branches = ((oa_ref, wa_ref), (ob_ref, wb_ref), (oc_ref, wc_ref))
    cw = 2 * LANES
    for c in range(0, D_MODEL, cw):
        y = jnp.zeros((x.shape[0], cw), F32)
        for j, (o_ref, w_ref) in enumerate(branches):
            logit = jnp.dot(xn, wg_ref[:, j * D_MODEL + c:j * D_MODEL + c + cw], preferred_element_type=F32)
            gate = 1.0 / (1.0 + jnp.exp(-logit))
            y = y + gate * jnp.dot(o_ref[...], w_ref[:, c:c + cw], preferred_element_type=F32)
        y_ref[:, c:c + cw] = y.astype(BF16)
    ssq = jnp.zeros((x.shape[0], 1), F32)
    for c in range(0, D_MODEL, cw):
        h = jnp.dot(y_ref[...], wo_ref[:, c:c + cw], preferred_element_type=F32)
        h_ref[:, c:c + cw] = h
        ssq = ssq + jnp.sum(h * h, axis=1, keepdims=True)
    inv = lax.rsqrt(ssq * (1.0 / D_MODEL) + NORM_EPS)
    out_ref[...] = x + h_ref[...] * inv * gpost_ref[...]


def _merge(x2, gpre, gpost, oa, ob, oc, w, tm=512):
    n = x2.shape[0]
    row = lambda width: pl.BlockSpec((tm, width), lambda i: (i, 0))
    full = lambda a: pl.BlockSpec(a.shape, lambda i: (0, 0))
    return pl.pallas_call(
        _merge_kernel,
        grid=(n // tm,),
        in_specs=[row(D_MODEL), full(gpre), full(gpost), row(A_OUT), row(B_OUT), row(C_OUT)] + [full(a) for a in w],
        out_specs=row(D_MODEL),
        out_shape=jax.ShapeDtypeStruct((n, D_MODEL), F32),
        scratch_shapes=[pltpu.VMEM((tm, D_MODEL), BF16), pltpu.VMEM((tm, D_MODEL), F32)],
        compiler_params=pltpu.CompilerParams(dimension_semantics=("arbitrary",), vmem_limit_bytes=VMEM_LIMIT),
        name="merge",
    )(x2, gpre, gpost, oa, ob, oc, *w)


FFN_CHUNK = 256
CARRY_ROWS = 8


def _ffn_kernel(x_ref, gpre_ref, gpost_ref, wup_ref, cw_ref, cb_ref, wdn_ref, out_ref, carry_ref, act_ref,
                *, tiles_per_seq):
    x = x_ref[...]
    tm = x.shape[0]
    xn = _rms(x, gpre_ref[...]).astype(BF16)
    first = pl.program_id(0) % tiles_per_seq == 0
    row = lax.broadcasted_iota(jnp.int32, (tm, 1), 0)
    assert CONV_WIDTH == 3 and CONV_WIDTH - 1 <= CARRY_ROWS

    def conv(col):
        h = jnp.dot(xn, wup_ref[:, col:col + FFN_CHUNK], preferred_element_type=F32)
        prev = jnp.where(first, 0.0, carry_ref[:, col:col + FFN_CHUNK])
        carry_ref[:, col:col + FFN_CHUNK] = h[tm - CARRY_ROWS:, :]
        p1 = prev[CARRY_ROWS - 1:CARRY_ROWS, :]
        p2 = prev[CARRY_ROWS - 2:CARRY_ROWS - 1, :]
        h1 = jnp.where(row == 0, p1, pltpu.roll(h, 1, 0))
        h2 = jnp.where(row == 0, p2, jnp.where(row == 1, p1, pltpu.roll(h, 2, 0)))
        w = cw_ref[:, col:col + FFN_CHUNK]
        return h2 * w[0:1, :] + h1 * w[1:2, :] + h * w[2:3, :] + cb_ref[:, col:col + FFN_CHUNK]

    for c in range(0, D_FF, FFN_CHUNK):
        g = conv(c)
        u = conv(D_FF + c)
        gelu = 0.5 * g * (1.0 + jnp.tanh(math.sqrt(2.0 / math.pi) * (g + 0.044715 * (g * g * g))))
        act_ref[:, c:c + FFN_CHUNK] = (gelu * u).astype(BF16)
    h = jnp.dot(act_ref[...], wdn_ref[...], preferred_element_type=F32)
    out_ref[...] = x + _rms(h, gpost_ref[...])


def _ffn(x2, gpre, gpost, wup, conv_w, conv_b, wdn, seq_len, tm=512):
    n = x2.shape[0]
    row = lambda width: pl.BlockSpec((tm, width), lambda i: (i, 0))
    full = lambda a: pl.BlockSpec(a.shape, lambda i: (0, 0))
    return pl.pallas_call(
        functools.partial(_ffn_kernel, tiles_per_seq=seq_len // tm),
        grid=(n // tm,),
        in_specs=[row(D_MODEL), full(gpre), full(gpost), full(wup), full(conv_w), full(conv_b), full(wdn)],
        out_specs=row(D_MODEL),
        out_shape=jax.ShapeDtypeStruct((n, D_MODEL), F32),
        scratch_shapes=[pltpu.VMEM((CARRY_ROWS, 2 * D_FF), F32), pltpu.VMEM((tm, D_FF), BF16)],
        compiler_params=pltpu.CompilerParams(dimension_semantics=("arbitrary",), vmem_limit_bytes=VMEM_LIMIT),
        name="conv_ffn",
    )(x2, gpre, gpost, wup, conv_w, conv_b, wdn)


def _rope_lane_tables(positions):
    half = ROPE_DIM // 2
    inv = ROPE_THETA ** (-jnp.arange(0, ROPE_DIM, 2, dtype=F32) / ROPE_DIM)
    ang = positions.astype(F32).reshape(-1, 1) * inv
    cos, sin = jnp.cos(ang), jnp.sin(ang)
    n = ang.shape[0]
    rest = HEAD_DIM - ROPE_DIM
    ones, zeros, zh = jnp.ones((n, rest), F32), jnp.zeros((n, rest), F32), jnp.zeros((n, half), F32)
    ct = jnp.concatenate([cos, cos, ones], axis=1)
    s1 = jnp.concatenate([-sin, zh, zeros], axis=1)
    s2 = jnp.concatenate([zh, sin, zeros], axis=1)
    return tuple(jnp.tile(t, (1, LANES // HEAD_DIM)) for t in (ct, s1, s2))


def _split_w_in(w_in):
    o_b, o_i, o_c, o_g = A_IN, A_IN + B_IN, A_IN + B_IN + IDX_IN, A_IN + B_IN + IDX_IN + C_IN
    wk = w_in[:, o_i + IDX_Q:o_i + IDX_Q + IDX_DIM]
    ww = w_in[:, o_i + IDX_Q + IDX_DIM:o_c]
    proj = (w_in[:, :o_b], w_in[:, o_b:o_i], w_in[:, o_i:o_i + IDX_Q],
            jnp.concatenate([wk, wk], axis=1),
            jnp.pad(ww, ((0, 0), (0, LANES - IDX_HEADS))),
            w_in[:, o_c:o_g])
    return tuple(a.astype(BF16) for a in proj), w_in[:, o_g:].astype(BF16)


def kernel(x, positions, w_in, w_br_a, w_br_b, w_br_c, w_out, lam_q1, lam_k1, lam_q2, lam_k2, subln_g,
           norm_mix_pre, norm_mix_post, norm_ffn_pre, norm_ffn_post, w_ffn_up, conv_w, conv_b, w_ffn_down):
    b, s, d = x.shape
    depth = w_in.shape[0]
    assert d == D_MODEL and s % A_TILE == 0 and s % SEL_CHUNK == 0 and s % DIFF_TK == 0
    tables = _rope_lane_tables(positions)
    x2 = x.reshape(b * s, d)
    vec = lambda a: a.reshape(1, -1)
    for layer in range(depth):
        lam_init = 0.8 - 0.6 * math.exp(-0.3 * layer)
        lam = (jnp.exp(jnp.sum(lam_q1[layer] * lam_k1[layer])) - jnp.exp(jnp.sum(lam_q2[layer] * lam_k2[layer]))
               + lam_init).reshape(1).astype(F32)
        w_proj, w_gate = _split_w_in(w_in[layer])
        pa, pb, qi, ki, wi, pc = _inproj(x2, vec(norm_mix_pre[layer]), tables, w_proj)
        seq = lambda a: a.reshape(b, s, a.shape[-1])
        oa = _dilated_attention(seq(pa))
        ob = _sparse_attention(seq(qi), seq(wi), seq(ki), seq(pb))
        oc = _diff_attention(seq(pc), lam, vec(subln_g[layer]), lam_init)
        flat = lambda a: a.reshape(b * s, a.shape[-1])
        w_merge = (w_gate, w_br_a[layer].astype(BF16), w_br_b[layer].astype(BF16), w_br_c[layer].astype(BF16),
                   w_out[layer].astype(BF16))
        x2 = _merge(x2, vec(norm_mix_pre[layer]), vec(norm_mix_post[layer]), flat(oa), flat(ob), flat(oc), w_merge)
        x2 = _ffn(x2, vec(norm_ffn_pre[layer]), vec(norm_ffn_post[layer]), w_ffn_up[layer].astype(BF16),
                  conv_w[layer], vec(conv_b[layer]), w_ffn_down[layer].astype(BF16), s)
    return x2.reshape(b, s, d)
```

```python
import functools
import math

import jax
import jax.numpy as jnp
from jax import lax
from jax.experimental import pallas as pl
from jax.experimental.pallas import tpu as pltpu

D_MODEL = 1024
HEAD_DIM = 64
ROPE_THETA = 500000.0
ROPE_DIM = HEAD_DIM // 4
NORM_EPS = 1e-6
Q_BLOCK = 128

A_PAIRS = ((128, 1), (512, 4), (2048, 16))
A_GROUPS = len(A_PAIRS)
A_HEADS = 4
A_OUT = A_HEADS * HEAD_DIM
B_HEADS = 4
B_OUT = B_HEADS * HEAD_DIM
IDX_HEADS = 8
IDX_DIM = 64
TOPK_MAX = 256
C_HEADS = 4
C_VDIM = 2 * HEAD_DIM
C_OUT = C_HEADS * C_VDIM
N_BRANCH = 3

A_IN = A_GROUPS * 3 * A_HEADS * HEAD_DIM
B_IN = 3 * B_HEADS * HEAD_DIM
IDX_Q = IDX_HEADS * IDX_DIM
IDX_IN = IDX_Q + IDX_DIM + IDX_HEADS
C_QK = C_HEADS * 2 * HEAD_DIM
C_IN = 2 * C_QK + C_OUT
D_FF = ((8 * D_MODEL // 3 + 127) // 128) * 128
CONV_WIDTH = 3

LANES = 128
SUBLANES = 8
SCALE = HEAD_DIM ** -0.5
NEG = -1e30
INT_MIN = -2 ** 31
VMEM_LIMIT = 56 * 1024 * 1024

F32 = jnp.float32
BF16 = jnp.bfloat16
NT_DIMS = (((1,), (1,)), ((), ()))


def _nt_dot(a, b):
    return lax.dot_general(a, b, NT_DIMS, preferred_element_type=F32)


def _rms(x, g):
    return x * lax.rsqrt(jnp.mean(x * x, axis=-1, keepdims=True) + NORM_EPS) * g


def _inproj_kernel(x_ref, g_ref, ct_ref, s1_ref, s2_ref, wa_ref, wb_ref, wqi_ref, wki_ref, wwi_ref, wc_ref,
                   pa_ref, pb_ref, qi_ref, ki_ref, wi_ref, pc_ref):
    xn = _rms(x_ref[...], g_ref[...]).astype(BF16)
    ct, s1, s2 = ct_ref[...], s1_ref[...], s2_ref[...]

    def rope(y):
        return y * ct + pltpu.roll(y, LANES - ROPE_DIM // 2, 1) * s1 + pltpu.roll(y, ROPE_DIM // 2, 1) * s2

    def project(w_ref, out_ref, n_cols, mode_of_col):
        for c in range(0, n_cols, 2 * LANES):
            y = jnp.dot(xn, w_ref[:, c:c + 2 * LANES], preferred_element_type=F32)
            for h in range(2):
                col = c + h * LANES
                z = y[:, h * LANES:(h + 1) * LANES]
                mode = mode_of_col(col)
                if mode != "v":
                    z = rope(z)
                if mode == "q":
                    z = z * SCALE
                out_ref[:, col:col + LANES] = z.astype(out_ref.dtype)

    qkv_mode = lambda width: (lambda col: ("q", "k", "v")[(col // width) % 3])
    project(wa_ref, pa_ref, A_IN, qkv_mode(A_OUT))
    project(wb_ref, pb_ref, B_IN, qkv_mode(B_OUT))
    project(wqi_ref, qi_ref, IDX_Q, lambda col: "k")
    project(wc_ref, pc_ref, C_IN, qkv_mode(C_QK))
    ki = jnp.dot(xn, wki_ref[...], preferred_element_type=F32)
    ki_ref[...] = rope(ki).astype(BF16)
    wi = jnp.dot(xn, wwi_ref[...], preferred_element_type=F32)
    wi_ref[...] = (wi * (IDX_HEADS ** -0.5)) * (IDX_DIM ** -0.5)


def _inproj(x2, gain, tables, w, tm=512):
    n = x2.shape[0]
    row = lambda width: pl.BlockSpec((tm, width), lambda i: (i, 0))
    full = lambda a: pl.BlockSpec(a.shape, lambda i: (0, 0))
    outs = [(A_IN, BF16), (B_IN, BF16), (IDX_Q, BF16), (LANES, BF16), (LANES, F32), (C_IN, BF16)]
    return pl.pallas_call(
        _inproj_kernel,
        grid=(n // tm,),
        in_specs=[row(D_MODEL), full(gain), row(LANES), row(LANES), row(LANES)] + [full(a) for a in w],
        out_specs=[row(wd) for wd, _ in outs],
        out_shape=[jax.ShapeDtypeStruct((n, wd), dt) for wd, dt in outs],
        compiler_params=pltpu.CompilerParams(dimension_semantics=("arbitrary",), vmem_limit_bytes=VMEM_LIMIT),
        name="inproj",
    )(x2, gain, *tables, *w)


def _head_masks(width):
    lane = lax.broadcasted_iota(jnp.int32, (1, width), 1)
    return [lane // HEAD_DIM == h for h in range(width // HEAD_DIM)]


def _stack_heads(q):
    zero = jnp.zeros_like(q)
    return jnp.concatenate([jnp.where(hm, q, zero) for hm in _head_masks(q.shape[1])], axis=0)


def _unstack_heads(x, rows):
    hmask = _head_masks(HEAD_DIM * (x.shape[0] // rows))
    out = jnp.where(hmask[0], x[0:rows], 0.0)
    for h in range(1, len(hmask)):
        out = jnp.where(hmask[h], x[h * rows:(h + 1) * rows], out)
    return out


def _lane_tile(x, width):
    reps = width // LANES
    return x if reps == 1 else jnp.concatenate([x] * reps, axis=1)


def _flash_init(m_ref, l_ref, acc_ref):
    m_ref[...] = jnp.full(m_ref.shape, NEG, F32)
    l_ref[...] = jnp.zeros(l_ref.shape, F32)
    acc_ref[...] = jnp.zeros(acc_ref.shape, F32)


def _flash_step(s, v, m_ref, l_ref, acc_ref):
    tk = s.shape[1]
    m_old = m_ref[...]
    m_new = jnp.maximum(m_old, jnp.max(s, axis=1, keepdims=True))
    p = jnp.exp(s - _lane_tile(m_new, tk))
    alpha = jnp.exp(m_old - m_new)
    psum = p[:, :LANES]
    for j in range(1, tk // LANES):
        psum = psum + p[:, j * LANES:(j + 1) * LANES]
    l_ref[...] = alpha * l_ref[...] + psum
    acc_ref[...] = (_lane_tile(alpha, acc_ref.shape[1]) * acc_ref[...]
                    + jnp.dot(p.astype(BF16), v, preferred_element_type=F32))
    m_ref[...] = m_new


def _flash_pipeline(n, logits_fn, values_fn, s_ref, m_ref, l_ref, acc_ref):
    def step(slot, t):
        _flash_step(s_ref[slot], values_fn(t), m_ref, l_ref, acc_ref)

    def pair(u, carry):
        t = 2 * u
        s_ref[1] = logits_fn(t + 1)
        step(0, t)
        s_ref[0] = logits_fn(t + 2)
        step(1, t + 1)
        return carry

    s_ref[0] = logits_fn(0)
    pairs = (n - 1) // 2
    lax.fori_loop(0, pairs, pair, 0)
    last = 2 * pairs

    @pl.when(n - last == 2)
    def _():
        s_ref[1] = logits_fn(last + 1)
        step(0, last)
        step(1, last + 1)

    @pl.when(n - last == 1)
    def _():
        step(0, last)


def _flash_result(l_ref, acc_ref):
    return acc_ref[...] / jnp.sum(l_ref[...], axis=1, keepdims=True)


A_TILE = max(w for w, _ in A_PAIRS)


def _dilated_kernel(*refs):
    group_refs = [refs[5 * g:5 * g + 5] for g in range(A_GROUPS)]
    o_ref, qs_ref, ks_ref, vs_ref, m_sc, l_sc, n_sc = refs[5 * A_GROUPS:]
    j = pl.program_id(1)
    qi = lax.broadcasted_iota(jnp.int32, (Q_BLOCK, 2 * Q_BLOCK), 0)
    kj = lax.broadcasted_iota(jnp.int32, (Q_BLOCK, 2 * Q_BLOCK), 1)
    rel = kj - qi
    band = (rel >= 0) & (rel <= Q_BLOCK)
    bias_band = jnp.where(band, 0.0, NEG)
    bias_first = jnp.where(band & (kj >= Q_BLOCK), 0.0, NEG)
    halves = A_OUT // LANES

    def get(ref, rows):
        return jnp.concatenate([ref[hh, rows, :] for hh in range(halves)], axis=1)

    def put(ref, rows, val):
        for hh in range(halves):
            ref[hh, rows, :] = val[:, hh * LANES:(hh + 1) * LANES]

    for g, ((window, dil), (q_ref, kp_ref, kc_ref, vp_ref, vc_ref)) in enumerate(zip(A_PAIRS, group_refs)):
        assert window == Q_BLOCK * dil
        put(qs_ref, slice(0, A_TILE), q_ref[...].astype(F32))
        put(ks_ref, slice(0, window), kp_ref[...].astype(F32))
        put(ks_ref, slice(window, window + A_TILE), kc_ref[...].astype(F32))
        put(vs_ref, slice(0, window), vp_ref[...].astype(F32))
        put(vs_ref, slice(window, window + A_TILE), vc_ref[...].astype(F32))
        shift = dil.bit_length() - 1

        def problem(p, carry, g=g, window=window, dil=dil, shift=shift):
            sub = p >> shift
            base = sub * window + (p & (dil - 1))
            stride = dil if dil > 1 else None
            q_rows = pl.ds(base, Q_BLOCK, stride=stride)
            k_rows = pl.ds(base, 2 * Q_BLOCK, stride=stride)
            qstack = _stack_heads(get(qs_ref, q_rows).astype(BF16))
            k = get(ks_ref, k_rows).astype(BF16)
            v = get(vs_ref, k_rows).astype(BF16)
            bias = jnp.where((sub > 0) | (j > 0), bias_band, bias_first)
            s = _nt_dot(qstack, k) + jnp.concatenate([bias] * A_HEADS, axis=0)
            m = jnp.max(s, axis=1, keepdims=True)
            e = jnp.exp(s - m)
            l = jnp.sum(e, axis=1, keepdims=True)
            num = jnp.dot(e.astype(BF16), v, preferred_element_type=F32)
            m_n, l_n, num_n = (_unstack_heads(t, Q_BLOCK) for t in (m, l, num))
            if g == 0:
                put(m_sc, q_rows, m_n)
                put(l_sc, q_rows, l_n)
                put(n_sc, q_rows, num_n)
            else:
                m_o = get(m_sc, q_rows)
                m_x = jnp.maximum(m_o, m_n)
                a, b = jnp.exp(m_o - m_x), jnp.exp(m_n - m_x)
                put(m_sc, q_rows, m_x)
                put(l_sc, q_rows, a * get(l_sc, q_rows) + b * l_n)
                put(n_sc, q_rows, a * get(n_sc, q_rows) + b * num_n)
            return carry

        lax.fori_loop(0, A_TILE // Q_BLOCK, problem, 0, unroll=4)

    all_rows = slice(0, A_TILE)
    o_ref[...] = (get(n_sc, all_rows) / get(l_sc, all_rows)).astype(o_ref.dtype)


def _dilated_attention(pa3):
    b, s, _ = pa3.shape
    in_specs = []
    for g, (window, _) in enumerate(A_PAIRS):
        per = A_TILE // window
        cur = lambda bi, j, col: (bi, j, col)
        prev = lambda bi, j, col, per=per: (bi, jnp.maximum(j * per - 1, 0), col)
        in_specs.append(pl.BlockSpec((None, A_TILE, A_OUT), functools.partial(cur, col=3 * g)))
        for col in (3 * g + 1, 3 * g + 2):
            in_specs.append(pl.BlockSpec((None, window, A_OUT), functools.partial(prev, col=col)))
            in_specs.append(pl.BlockSpec((None, A_TILE, A_OUT), functools.partial(cur, col=col)))
    stage = lambda rows: pltpu.VMEM((A_OUT // LANES, rows, LANES), F32)
    return pl.pallas_call(
        _dilated_kernel,
        grid=(b, s // A_TILE),
        in_specs=in_specs,
        out_specs=pl.BlockSpec((None, A_TILE, A_OUT), lambda bi, j: (bi, j, 0)),
        out_shape=jax.ShapeDtypeStruct((b, s, A_OUT), BF16),
        scratch_shapes=[stage(A_TILE), stage(2 * A_TILE), stage(2 * A_TILE),
                        stage(A_TILE), stage(A_TILE), stage(A_TILE)],
        compiler_params=pltpu.CompilerParams(dimension_semantics=("arbitrary", "arbitrary"),
                                             vmem_limit_bytes=VMEM_LIMIT),
        name="dilated_attn",
    )(*([pa3] * (5 * A_GROUPS)))


SEL_TQ = 256
SEL_CHUNK = 512
SCORE_GROUP = 16
ATT_CHUNK = 512


def _sparse_kernel(qi_ref, wi_ref, ki_ref, q_ref, k_ref, v_ref, o_ref, key_ref, keyt_ref, s_ref, m_ref, l_ref,
                   acc_ref, *, topk):
    i = pl.program_id(1)
    qs = i * SEL_TQ
    nch = (qs + SEL_TQ - 1) // SEL_CHUNK + 1
    nat = (qs + SEL_TQ - 1) // ATT_CHUNK + 1
    lane = lax.broadcasted_iota(jnp.int32, (1, LANES), 1)
    qpos = qs + lax.broadcasted_iota(jnp.int32, (SEL_TQ, 1), 0)

    qi = qi_ref[...]
    per_head = []
    for h in range(IDX_HEADS):
        pair = qi[:, (h // 2) * LANES:(h // 2 + 1) * LANES]
        keep = (lane < IDX_DIM) if h % 2 == 0 else (lane >= IDX_DIM)
        per_head.append(jnp.where(keep, pair, jnp.zeros_like(pair)))
    groups = range(0, SEL_TQ, SCORE_GROUP)
    stacked = jnp.concatenate([ph[g:g + SCORE_GROUP] for g in groups for ph in per_head], axis=0)
    w = wi_ref[...]
    wcol = [w[:, h:h + 1] for h in range(IDX_HEADS)]

    def to_key(score):
        bits = lax.bitcast_convert_type(score, jnp.int32)
        return bits ^ ((bits >> 31) & 0x7FFFFFFF)

    def score_chunk(c, group_max):
        start = pl.multiple_of(c * SEL_CHUNK, SEL_CHUNK)
        logits = _nt_dot(stacked, ki_ref[pl.ds(start, SEL_CHUNK), :])
        pieces = []
        for g in groups:
            base = g * IDX_HEADS
            sg = jnp.zeros((SCORE_GROUP, SEL_CHUNK), F32)
            for h in range(IDX_HEADS):
                rows = slice(base + h * SCORE_GROUP, base + (h + 1) * SCORE_GROUP)
                sg = sg + jnp.maximum(logits[rows], 0.0) * wcol[h][g:g + SCORE_GROUP]
            pieces.append(sg)
        score = jnp.concatenate(pieces, axis=0)
        kpos = start + lax.broadcasted_iota(jnp.int32, (1, SEL_CHUNK), 1)
        causal = kpos <= qpos
        key_ref[:, pl.ds(start, SEL_CHUNK)] = jnp.where(causal, to_key(score), INT_MIN)
        score = jnp.where(causal, score, -jnp.inf)
        group_max = list(group_max)
        for jj in range(SEL_CHUNK // LANES):
            group_max[jj % 2] = jnp.maximum(group_max[jj % 2], score[:, jj * LANES:(jj + 1) * LANES])
        return tuple(group_max)

    def transpose_chunk(c):
        start = pl.multiple_of(c * SEL_CHUNK, SEL_CHUNK)
        keyt_ref[pl.ds(start, SEL_CHUNK), :] = key_ref[:, pl.ds(start, SEL_CHUNK)].T

    def score_step(c, group_max):
        transpose_chunk(c - 1)
        return score_chunk(c, group_max)

    neg_inf = jnp.full((SEL_TQ, LANES), -jnp.inf, F32)
    gm_even, gm_odd = lax.fori_loop(1, nch, score_step, score_chunk(0, (neg_inf, neg_inf)))
    transpose_chunk(nch - 1)

    def to_lanes(col):
        return jnp.broadcast_to(col, (SEL_TQ, LANES)).T[0:1, :]

    def to_rows(lanes):
        return jnp.broadcast_to(lanes, (LANES, SEL_TQ)).T[:, 0:1]

    lane_blocks = [slice(h, h + LANES) for h in range(0, SEL_TQ, LANES)]

    def count(compare, level, active=None):
        lanes_acc = 8 * SUBLANES
        parts = []
        for blk_no, lanes in enumerate(lane_blocks):
            def body(c, acc, lanes=lanes):
                blk = keyt_ref[pl.ds(pl.multiple_of(c * SEL_CHUNK, SEL_CHUNK), SEL_CHUNK), lanes]
                hit = jnp.where(compare(blk, level[:, lanes]), 1.0, 0.0)
                return acc + jnp.sum(hit.reshape(SEL_CHUNK // lanes_acc, lanes_acc, LANES), axis=0)
            trips = nch if active is None else jnp.where(active[blk_no], nch, 0)
            acc = lax.fori_loop(0, trips, body, jnp.zeros((lanes_acc, LANES), F32))
            parts.append(jnp.sum(acc, axis=0, keepdims=True))
        return jnp.concatenate(parts, axis=1)

    kf = float(topk)
    assert topk <= 2 * LANES and SEL_TQ % LANES == 0
    searchable = qs >= topk
    lo_f = to_lanes(jnp.minimum(jnp.min(gm_even, axis=1, keepdims=True), jnp.min(gm_odd, axis=1, keepdims=True)))
    hi_f = to_lanes(jnp.maximum(jnp.max(gm_even, axis=1, keepdims=True), jnp.max(gm_odd, axis=1, keepdims=True)))
    lo0 = jnp.where(searchable, to_key(lo_f), INT_MIN + 1)
    hi0 = to_key(hi_f) + 1
    cnt_lo0 = jnp.where(searchable, kf + 1.0, kf) + jnp.zeros((1, SEL_TQ), F32)

    def candidate(lo, hi, cnt_lo):
        mid = (lo >> 1) + (hi >> 1) + (lo & hi & 1)
        open_q = (cnt_lo > kf) & (mid > lo)
        cand = jnp.where((lo < 0) & (hi > 0), 0, jnp.where((lo == 0) & (hi > 1), 1, mid))
        open_q = jnp.where(open_q, 1.0, 0.0)
        block_open = tuple(jnp.max(open_q[:, lanes]) > 0.0 for lanes in lane_blocks)
        return (cand, open_q, functools.reduce(jnp.logical_or, block_open)) + block_open

    def search_step(carry, active):
        lo, hi, cnt_lo, cnt_hi, cand, open_q = carry[:6]
        cnt = count(lambda blk, level: blk >= level, cand, active)
        up = (open_q > 0.0) & (cnt >= kf)
        down = (open_q > 0.0) & (cnt < kf)
        lo, cnt_lo = jnp.where(up, cand, lo), jnp.where(up, cnt, cnt_lo)
        hi, cnt_hi = jnp.where(down, cand, hi), jnp.where(down, cnt, cnt_hi)
        return (lo, hi, cnt_lo, cnt_hi) + candidate(lo, hi, cnt_lo)

    def search_trip(carry):
        active = carry[7:]
        return search_step(search_step(carry, active), active)

    thr_q, _, cnt_thr, cnt_hi = lax.while_loop(
        lambda carry: carry[6], search_trip,
        (lo0, hi0, cnt_lo0, jnp.zeros((1, SEL_TQ), F32)) + candidate(lo0, hi0, cnt_lo0))[:4]
    has_ties = jnp.max(cnt_thr) > kf
    thr = to_rows(thr_q)

    @pl.when(has_ties)
    def _():
        need = to_rows(jnp.where(cnt_thr > kf, kf - cnt_hi, kf))
        r = lax.broadcasted_iota(jnp.int32, (ATT_CHUNK, ATT_CHUNK), 0)
        cc = lax.broadcasted_iota(jnp.int32, (ATT_CHUNK, ATT_CHUNK), 1)
        before = jnp.where(r < cc, 1.0, 0.0).astype(BF16)

        def body(c, seen):
            cols = pl.ds(pl.multiple_of(c * ATT_CHUNK, ATT_CHUNK), ATT_CHUNK)
            blk = key_ref[:, cols]
            eq = blk == thr
            eqf = jnp.where(eq, 1.0, 0.0)
            rank = seen + jnp.dot(eqf.astype(BF16), before, preferred_element_type=F32)
            key_ref[:, cols] = jnp.where(eq & (rank >= need), INT_MIN, blk)
            return seen + jnp.sum(eqf, axis=1, keepdims=True)
        lax.fori_loop(0, nat, body, jnp.zeros((SEL_TQ, 1), F32))

    qstack = _stack_heads(q_ref[...])

    def logits(c, sel):
        start = pl.multiple_of(c * ATT_CHUNK, ATT_CHUNK)
        bias = jnp.where(sel, 0.0, NEG)
        return _nt_dot(qstack, k_ref[pl.ds(start, ATT_CHUNK), :]) + jnp.concatenate([bias] * B_HEADS, axis=0)

    def values(c):
        return v_ref[pl.ds(pl.multiple_of(c * ATT_CHUNK, ATT_CHUNK), ATT_CHUNK), :]

    def selected(c):
        return key_ref[:, pl.ds(pl.multiple_of(c * ATT_CHUNK, ATT_CHUNK), ATT_CHUNK)] >= thr

    _flash_init(m_ref, l_ref, acc_ref)
    _flash_pipeline(nat, lambda c: logits(c, selected(c)), values, s_ref, m_ref, l_ref, acc_ref)

    o_ref[...] = _unstack_heads(_flash_result(l_ref, acc_ref), SEL_TQ).astype(o_ref.dtype)


def _sparse_attention(qi3, wi3, ki3, pb3):
    b, s, _ = pb3.shape
    topk = min(TOPK_MAX, s // 4)
    rows = B_HEADS * SEL_TQ
    qblk = lambda width, col: pl.BlockSpec((None, SEL_TQ, width), lambda bi, i: (bi, i, col))
    seq = lambda width, col: pl.BlockSpec((None, s, width), lambda bi, i: (bi, 0, col))
    return pl.pallas_call(
        functools.partial(_sparse_kernel, topk=topk),
        grid=(b, s // SEL_TQ),
        in_specs=[qblk(IDX_Q, 0), qblk(LANES, 0), seq(LANES, 0), qblk(B_OUT, 0), seq(B_OUT, 1), seq(B_OUT, 2)],
        out_specs=pl.BlockSpec((None, SEL_TQ, B_OUT), lambda bi, i: (bi, i, 0)),
        out_shape=jax.ShapeDtypeStruct((b, s, B_OUT), BF16),
        scratch_shapes=[pltpu.VMEM((SEL_TQ, s), jnp.int32), pltpu.VMEM((s, SEL_TQ), jnp.int32),
                        pltpu.VMEM((2, rows, ATT_CHUNK), F32),
                        pltpu.VMEM((rows, LANES), F32), pltpu.VMEM((rows, LANES), F32),
                        pltpu.VMEM((rows, B_OUT), F32)],
        compiler_params=pltpu.CompilerParams(dimension_semantics=("arbitrary", "arbitrary"),
                                             vmem_limit_bytes=VMEM_LIMIT),
        name="sparse_attn",
    )(qi3, wi3, ki3, pb3, pb3, pb3)


DIFF_TQ = 512
DIFF_TK = 1024


def _diff_kernel(lam_ref, g_ref, q_ref, k_ref, v_ref, o_ref, s_ref, m_ref, l_ref, acc_ref, *, out_scale):
    i = pl.program_id(2)
    qstack = _stack_heads(q_ref[...])
    _flash_init(m_ref, l_ref, acc_ref)

    def logits(kb):
        return _nt_dot(qstack, k_ref[pl.ds(pl.multiple_of(kb * DIFF_TK, DIFF_TK), DIFF_TK), :])

    def values(kb):
        return v_ref[pl.ds(pl.multiple_of(kb * DIFF_TK, DIFF_TK), DIFF_TK), :]

    n_full = (i * DIFF_TQ) // DIFF_TK

    @pl.when(n_full > 0)
    def _():
        _flash_pipeline(n_full, logits, values, s_ref, m_ref, l_ref, acc_ref)

    def diagonal(width):
        start = pl.multiple_of(n_full * DIFF_TK, DIFF_TK)
        qpos = i * DIFF_TQ + (lax.broadcasted_iota(jnp.int32, (2 * DIFF_TQ, width), 0) & (DIFF_TQ - 1))
        kpos = start + lax.broadcasted_iota(jnp.int32, (2 * DIFF_TQ, width), 1)
        s = _nt_dot(qstack, k_ref[pl.ds(start, width), :]) + jnp.where(kpos <= qpos, 0.0, NEG)
        _flash_step(s, v_ref[pl.ds(start, width), :], m_ref, l_ref, acc_ref)

    assert DIFF_TK == 2 * DIFF_TQ
    first_half = (i * DIFF_TQ) % DIFF_TK == 0
    pl.when(first_half)(lambda: diagonal(DIFF_TQ))
    pl.when(jnp.logical_not(first_half))(lambda: diagonal(DIFF_TK))

    res = _flash_result(l_ref, acc_ref)
    o = res[:DIFF_TQ] - lam_ref[0] * res[DIFF_TQ:]
    o_ref[...] = (_rms(o, g_ref[...]) * out_scale).astype(o_ref.dtype)


def _diff_attention(pc3, lam, subln_g, lam_init):
    b, s, _ = pc3.shape
    nq = C_QK // LANES
    return pl.pallas_call(
        functools.partial(_diff_kernel, out_scale=1.0 - lam_init),
        grid=(b, C_HEADS, s // DIFF_TQ),
        in_specs=[pl.BlockSpec(memory_space=pltpu.SMEM),
                  pl.BlockSpec((1, C_VDIM), lambda bi, h, i: (0, 0)),
                  pl.BlockSpec((None, DIFF_TQ, LANES), lambda bi, h, i: (bi, i, h)),
                  pl.BlockSpec((None, s, LANES), lambda bi, h, i: (bi, 0, nq + h)),
                  pl.BlockSpec((None, s, C_VDIM), lambda bi, h, i: (bi, 0, 2 * nq + h))],
        out_specs=pl.BlockSpec((None, DIFF_TQ, C_VDIM), lambda bi, h, i: (bi, i, h)),
        out_shape=jax.ShapeDtypeStruct((b, s, C_OUT), BF16),
        scratch_shapes=[pltpu.VMEM((2, 2 * DIFF_TQ, DIFF_TK), F32),
                        pltpu.VMEM((2 * DIFF_TQ, LANES), F32), pltpu.VMEM((2 * DIFF_TQ, LANES), F32),
                        pltpu.VMEM((2 * DIFF_TQ, C_VDIM), F32)],
        compiler_params=pltpu.CompilerParams(dimension_semantics=("arbitrary", "arbitrary", "arbitrary"),
                                             vmem_limit_bytes=VMEM_LIMIT),
        name="diff_attn",
    )(lam, subln_g, pc3, pc3, pc3)


def _merge_kernel(x_ref, gpre_ref, gpost_ref, oa_ref, ob_ref, oc_ref, wg_ref, wa_ref, wb_ref, wc_ref, wo_ref,
                  out_ref, y_ref, h_ref):
    x = x_ref[...]
    xn = _rms(x, gpre_ref[...]).astype(BF16)
    branches = ((oa_ref, wa_ref), (ob_ref, wb_ref), (oc_ref, wc_ref))
    cw = 2 * LANES
    for c in range(0, D_MODEL, cw):
        y = jnp.zeros((x.shape[0], cw), F32)
        for j, (o_ref, w_ref) in enumerate(branches):
            logit = jnp.dot(xn, wg_ref[:, j * D_MODEL + c:j * D_MODEL + c + cw], preferred_element_type=F32)
            gate = 1.0 / (1.0 + jnp.exp(-logit))
            y = y + gate * jnp.dot(o_ref[...], w_ref[:, c:c + cw], preferred_element_type=F32)
        y_ref[:, c:c + cw] = y.astype(BF16)
    ssq = jnp.zeros((x.shape[0], 1), F32)
    for c in range(0, D_MODEL, cw):
        h = jnp.dot(y_ref[...], wo_ref[:, c:c + cw], preferred_element_type=F32)
        h_ref[:, c:c + cw] = h
        ssq = ssq + jnp.sum(h * h, axis=1, keepdims=True)
    inv = lax.rsqrt(ssq * (1.0 / D_MODEL) + NORM_EPS)
    out_ref[...] = x + h_ref[...] * inv * gpost_ref[...]


def _merge(x2, gpre, gpost, oa, ob, oc, w, tm=512):
    n = x2.shape[0]
    row = lambda width: pl.BlockSpec((tm, width), lambda i: (i, 0))
    full = lambda a: pl.BlockSpec(a.shape, lambda i: (0, 0))
    return pl.pallas_call(
        _merge_kernel,
        grid=(n // tm,),
        in_specs=[row(D_MODEL), full(gpre), full(gpost), row(A_OUT), row(B_OUT), row(C_OUT)] + [full(a) for a in w],
        out_specs=row(D_MODEL),
        out_shape=jax.ShapeDtypeStruct((n, D_MODEL), F32),
        scratch_shapes=[pltpu.VMEM((tm, D_MODEL), BF16), pltpu.VMEM((tm, D_MODEL), F32)],
        compiler_params=pltpu.CompilerParams(dimension_semantics=("arbitrary",), vmem_limit_bytes=VMEM_LIMIT),
        name="merge",
    )(x2, gpre, gpost, oa, ob, oc, *w)


FFN_CHUNK = 256
CARRY_ROWS = 8


def _ffn_kernel(x_ref, gpre_ref, gpost_ref, wup_ref, cw_ref, cb_ref, wdn_ref, out_ref, carry_ref, act_ref,
                *, tiles_per_seq):
    x = x_ref[...]
    tm = x.shape[0]
    xn = _rms(x, gpre_ref[...]).astype(BF16)
    first = pl.program_id(0) % tiles_per_seq == 0
    row = lax.broadcasted_iota(jnp.int32, (tm, 1), 0)
    assert CONV_WIDTH == 3 and CONV_WIDTH - 1 <= CARRY_ROWS

    def conv(col):
        h = jnp.dot(xn, wup_ref[:, col:col + FFN_CHUNK], preferred_element_type=F32)
        prev = jnp.where(first, 0.0, carry_ref[:, col:col + FFN_CHUNK])
        carry_ref[:, col:col + FFN_CHUNK] = h[tm - CARRY_ROWS:, :]
        p1 = prev[CARRY_ROWS - 1:CARRY_ROWS, :]
        p2 = prev[CARRY_ROWS - 2:CARRY_ROWS - 1, :]
        h1 = jnp.where(row == 0, p1, pltpu.roll(h, 1, 0))
        h2 = jnp.where(row == 0, p2, jnp.where(row == 1, p1, pltpu.roll(h, 2, 0)))
        w = cw_ref[:, col:col + FFN_CHUNK]
        return h2 * w[0:1, :] + h1 * w[1:2, :] + h * w[2:3, :] + cb_ref[:, col:col + FFN_CHUNK]

    for c in range(0, D_FF, FFN_CHUNK):
        g = conv(c)
        u = conv(D_FF + c)
        gelu = 0.5 * g * (1.0 + jnp.tanh(math.sqrt(2.0 / math.pi) * (g + 0.044715 * (g * g * g))))
        act_ref[:, c:c + FFN_CHUNK] = (gelu * u).astype(BF16)
    h = jnp.dot(act_ref[...], wdn_ref[...], preferred_element_type=F32)
    out_ref[...] = x + _rms(h, gpost_ref[...])


def _ffn(x2, gpre, gpost, wup, conv_w, conv_b, wdn, seq_len, tm=512):
    n = x2.shape[0]
    row = lambda width: pl.BlockSpec((tm, width), lambda i: (i, 0))
    full = lambda a: pl.BlockSpec(a.shape, lambda i: (0, 0))
    return pl.pallas_call(
        functools.partial(_ffn_kernel, tiles_per_seq=seq_len // tm),
        grid=(n // tm,),
        in_specs=[row(D_MODEL), full(gpre), full(gpost), full(wup), full(conv_w), full(conv_b), full(wdn)],
        out_specs=row(D_MODEL),
        out_shape=jax.ShapeDtypeStruct((n, D_MODEL), F32),
        scratch_shapes=[pltpu.VMEM((CARRY_ROWS, 2 * D_FF), F32), pltpu.VMEM((tm, D_FF), BF16)],
        compiler_params=pltpu.CompilerParams(dimension_semantics=("arbitrary",), vmem_limit_bytes=VMEM_LIMIT),
        name="conv_ffn",
    )(x2, gpre, gpost, wup, conv_w, conv_b, wdn)


def _rope_lane_tables(positions):
    half = ROPE_DIM // 2
    inv = ROPE_THETA ** (-jnp.arange(0, ROPE_DIM, 2, dtype=F32) / ROPE_DIM)
    ang = positions.astype(F32).reshape(-1, 1) * inv
    cos, sin = jnp.cos(ang), jnp.sin(ang)
    n = ang.shape[0]
    rest = HEAD_DIM - ROPE_DIM
    ones, zeros, zh = jnp.ones((n, rest), F32), jnp.zeros((n, rest), F32), jnp.zeros((n, half), F32)
    ct = jnp.concatenate([cos, cos, ones], axis=1)
    s1 = jnp.concatenate([-sin, zh, zeros], axis=1)
    s2 = jnp.concatenate([zh, sin, zeros], axis=1)
    return tuple(jnp.tile(t, (1, LANES // HEAD_DIM)) for t in (ct, s1, s2))


def _split_w_in(w_in):
    o_b, o_i, o_c, o_g = A_IN, A_IN + B_IN, A_IN + B_IN + IDX_IN, A_IN + B_IN + IDX_IN + C_IN
    wk = w_in[:, o_i + IDX_Q:o_i + IDX_Q + IDX_DIM]
    ww = w_in[:, o_i + IDX_Q + IDX_DIM:o_c]
    proj = (w_in[:, :o_b], w_in[:, o_b:o_i], w_in[:, o_i:o_i + IDX_Q],
            jnp.concatenate([wk, wk], axis=1),
            jnp.pad(ww, ((0, 0), (0, LANES - IDX_HEADS))),
            w_in[:, o_c:o_g])
    return tuple(a.astype(BF16) for a in proj), w_in[:, o_g:].astype(BF16)


def kernel(x, positions, w_in, w_br_a, w_br_b, w_br_c, w_out, lam_q1, lam_k1, lam_q2, lam_k2, subln_g,
           norm_mix_pre, norm_mix_post, norm_ffn_pre, norm_ffn_post, w_ffn_up, conv_w, conv_b, w_ffn_down):
    b, s, d = x.shape
    depth = w_in.shape[0]
    assert d == D_MODEL and s % A_TILE == 0 and s % SEL_CHUNK == 0 and s % DIFF_TK == 0
    tables = _rope_lane_tables(positions)
    x2 = x.reshape(b * s, d)
    vec = lambda a: a.reshape(1, -1)
    for layer in range(depth):
        lam_init = 0.8 - 0.6 * math.exp(-0.3 * layer)
        lam = (jnp.exp(jnp.sum(lam_q1[layer] * lam_k1[layer])) - jnp.exp(jnp.sum(lam_q2[layer] * lam_k2[layer]))
               + lam_init).reshape(1).astype(F32)
        w_proj, w_gate = _split_w_in(w_in[layer])
        pa, pb, qi, ki, wi, pc = _inproj(x2, vec(norm_mix_pre[layer]), tables, w_proj)
        seq = lambda a: a.reshape(b, s, a.shape[-1])
        oa = _dilated_attention(seq(pa))
        ob = _sparse_attention(seq(qi), seq(wi), seq(ki), seq(pb))
        oc = _diff_attention(seq(pc), lam, vec(subln_g[layer]), lam_init)
        flat = lambda a: a.reshape(b * s, a.shape[-1])
        w_merge = (w_gate, w_br_a[layer].astype(BF16), w_br_b[layer].astype(BF16), w_br_c[layer].astype(BF16),
                   w_out[layer].astype(BF16))
        x2 = _merge(x2, vec(norm_mix_pre[layer]), vec(norm_mix_post[layer]), flat(oa), flat(ob), flat(oc), w_merge)
        x2 = _ffn(x2, vec(norm_ffn_pre[layer]), vec(norm_ffn_post[layer]), w_ffn_up[layer].astype(BF16),
                  conv_w[layer], vec(conv_b[layer]), w_ffn_down[layer].astype(BF16), s)
    return x2.reshape(b, s, d)
```

```python
import functools
import math

import jax
import jax.numpy as jnp
from jax import lax
from jax.experimental import pallas as pl
from jax.experimental.pallas import tpu as pltpu

D_MODEL = 1024
HEAD_DIM = 64
ROPE_THETA = 500000.0
ROPE_DIM = HEAD_DIM // 4
NORM_EPS = 1e-6
Q_BLOCK = 128

A_PAIRS = ((128, 1), (512, 4), (2048, 16))
A_GROUPS = len(A_PAIRS)
A_HEADS = 4
A_OUT = A_HEADS * HEAD_DIM
B_HEADS = 4
B_OUT = B_HEADS * HEAD_DIM
IDX_HEADS = 8
IDX_DIM = 64
TOPK_MAX = 256
C_HEADS = 4
C_VDIM = 2 * HEAD_DIM
C_OUT = C_HEADS * C_VDIM
N_BRANCH = 3

A_IN = A_GROUPS * 3 * A_HEADS * HEAD_DIM
B_IN = 3 * B_HEADS * HEAD_DIM
IDX_Q = IDX_HEADS * IDX_DIM
IDX_IN = IDX_Q + IDX_DIM + IDX_HEADS
C_QK = C_HEADS * 2 * HEAD_DIM
C_IN = 2 * C_QK + C_OUT
D_FF = ((8 * D_MODEL // 3 + 127) // 128) * 128
CONV_WIDTH = 3

LANES = 128
SUBLANES = 8
SCALE = HEAD_DIM ** -0.5
NEG = -1e30
INT_MIN = -2 ** 31
VMEM_LIMIT = 56 * 1024 * 1024

F32 = jnp.float32
BF16 = jnp.bfloat16
NT_DIMS = (((1,), (1,)), ((), ()))


def _nt_dot(a, b):
    return lax.dot_general(a, b, NT_DIMS, preferred_element_type=F32)


def _rms(x, g):
    return x * lax.rsqrt(jnp.mean(x * x, axis=-1, keepdims=True) + NORM_EPS) * g


def _inproj_kernel(x_ref, g_ref, ct_ref, s1_ref, s2_ref, wa_ref, wb_ref, wqi_ref, wki_ref, wwi_ref, wc_ref,
                   pa_ref, pb_ref, qi_ref, ki_ref, wi_ref, pc_ref):
    xn = _rms(x_ref[...], g_ref[...]).astype(BF16)
    ct, s1, s2 = ct_ref[...], s1_ref[...], s2_ref[...]

    def rope(y):
        return y * ct + pltpu.roll(y, LANES - ROPE_DIM // 2, 1) * s1 + pltpu.roll(y, ROPE_DIM // 2, 1) * s2

    def project(w_ref, out_ref, n_cols, mode_of_col):
        for c in range(0, n_cols, 2 * LANES):
            y = jnp.dot(xn, w_ref[:, c:c + 2 * LANES], preferred_element_type=F32)
            for h in range(2):
                col = c + h * LANES
                z = y[:, h * LANES:(h + 1) * LANES]
                mode = mode_of_col(col)
                if mode != "v":
                    z = rope(z)
                if mode == "q":
                    z = z * SCALE
                out_ref[:, col:col + LANES] = z.astype(out_ref.dtype)

    qkv_mode = lambda width: (lambda col: ("q", "k", "v")[(col // width) % 3])
    project(wa_ref, pa_ref, A_IN, qkv_mode(A_OUT))
    project(wb_ref, pb_ref, B_IN, qkv_mode(B_OUT))
    project(wqi_ref, qi_ref, IDX_Q, lambda col: "k")
    project(wc_ref, pc_ref, C_IN, qkv_mode(C_QK))
    ki = jnp.dot(xn, wki_ref[...], preferred_element_type=F32)
    ki_ref[...] = rope(ki).astype(BF16)
    wi = jnp.dot(xn, wwi_ref[...], preferred_element_type=F32)
    wi_ref[...] = (wi * (IDX_HEADS ** -0.5)) * (IDX_DIM ** -0.5)


def _inproj(x2, gain, tables, w, tm=512):
    n = x2.shape[0]
    row = lambda width: pl.BlockSpec((tm, width), lambda i: (i, 0))
    full = lambda a: pl.BlockSpec(a.shape, lambda i: (0, 0))
    outs = [(A_IN, BF16), (B_IN, BF16), (IDX_Q, BF16), (LANES, BF16), (LANES, F32), (C_IN, BF16)]
    return pl.pallas_call(
        _inproj_kernel,
        grid=(n // tm,),
        in_specs=[row(D_MODEL), full(gain), row(LANES), row(LANES), row(LANES)] + [full(a) for a in w],
        out_specs=[row(wd) for wd, _ in outs],
        out_shape=[jax.ShapeDtypeStruct((n, wd), dt) for wd, dt in outs],
        compiler_params=pltpu.CompilerParams(dimension_semantics=("arbitrary",), vmem_limit_bytes=VMEM_LIMIT),
        name="inproj",
    )(x2, gain, *tables, *w)


def _head_masks(width):
    lane = lax.broadcasted_iota(jnp.int32, (1, width), 1)
    return [lane // HEAD_DIM == h for h in range(width // HEAD_DIM)]


def _stack_heads(q):
    zero = jnp.zeros_like(q)
    return jnp.concatenate([jnp.where(hm, q, zero) for hm in _head_masks(q.shape[1])], axis=0)


def _unstack_heads(x, rows):
    hmask = _head_masks(HEAD_DIM * (x.shape[0] // rows))
    out = jnp.where(hmask[0], x[0:rows], 0.0)
    for h in range(1, len(hmask)):
        out = jnp.where(hmask[h], x[h * rows:(h + 1) * rows], out)
    return out


def _lane_tile(x, width):
    reps = width // LANES
    return x if reps == 1 else jnp.concatenate([x] * reps, axis=1)


def _flash_init(m_ref, l_ref, acc_ref):
    m_ref[...] = jnp.full(m_ref.shape, NEG, F32)
    l_ref[...] = jnp.zeros(l_ref.shape, F32)
    acc_ref[...] = jnp.zeros(acc_ref.shape, F32)


def _flash_step(s, v, m_ref, l_ref, acc_ref):
    tk = s.shape[1]
    m_old = m_ref[...]
    m_new = jnp.maximum(m_old, jnp.max(s, axis=1, keepdims=True))
    p = jnp.exp(s - _lane_tile(m_new, tk))
    alpha = jnp.exp(m_old - m_new)
    psum = p[:, :LANES]
    for j in range(1, tk // LANES):
        psum = psum + p[:, j * LANES:(j + 1) * LANES]
    l_ref[...] = alpha * l_ref[...] + psum
    acc_ref[...] = (_lane_tile(alpha, acc_ref.shape[1]) * acc_ref[...]
                    + jnp.dot(p.astype(BF16), v, preferred_element_type=F32))
    m_ref[...] = m_new


def _flash_pipeline(n, logits_fn, values_fn, s_ref, m_ref, l_ref, acc_ref):
    def step(slot, t):
        _flash_step(s_ref[slot], values_fn(t), m_ref, l_ref, acc_ref)

    def pair(u, carry):
        t = 2 * u
        s_ref[1] = logits_fn(t + 1)
        step(0, t)
        s_ref[0] = logits_fn(t + 2)
        step(1, t + 1)
        return carry

    s_ref[0] = logits_fn(0)
    pairs = (n - 1) // 2
    lax.fori_loop(0, pairs, pair, 0)
    last = 2 * pairs

    @pl.when(n - last == 2)
    def _():
        s_ref[1] = logits_fn(last + 1)
        step(0, last)
        step(1, last + 1)

    @pl.when(n - last == 1)
    def _():
        step(0, last)


def _flash_result(l_ref, acc_ref):
    return acc_ref[...] / jnp.sum(l_ref[...], axis=1, keepdims=True)


A_TILE = max(w for w, _ in A_PAIRS)


def _dilated_kernel(*refs):
    group_refs = [refs[5 * g:5 * g + 5] for g in range(A_GROUPS)]
    o_ref, qs_ref, ks_ref, vs_ref, m_sc, l_sc, n_sc = refs[5 * A_GROUPS:]
    j = pl.program_id(1)
    qi = lax.broadcasted_iota(jnp.int32, (Q_BLOCK, 2 * Q_BLOCK), 0)
    kj = lax.broadcasted_iota(jnp.int32, (Q_BLOCK, 2 * Q_BLOCK), 1)
    rel = kj - qi
    band = (rel >= 0) & (rel <= Q_BLOCK)
    bias_band = jnp.where(band, 0.0, NEG)
    bias_first = jnp.where(band & (kj >= Q_BLOCK), 0.0, NEG)
    halves = A_OUT // LANES

    def get(ref, rows):
        return jnp.concatenate([ref[hh, rows, :] for hh in range(halves)], axis=1)

    def put(ref, rows, val):
        for hh in range(halves):
            ref[hh, rows, :] = val[:, hh * LANES:(hh + 1) * LANES]

    for g, ((window, dil), (q_ref, kp_ref, kc_ref, vp_ref, vc_ref)) in enumerate(zip(A_PAIRS, group_refs)):
        assert window == Q_BLOCK * dil
        put(qs_ref, slice(0, A_TILE), q_ref[...].astype(F32))
        put(ks_ref, slice(0, window), kp_ref[...].astype(F32))
        put(ks_ref, slice(window, window + A_TILE), kc_ref[...].astype(F32))
        put(vs_ref, slice(0, window), vp_ref[...].astype(F32))
        put(vs_ref, slice(window, window + A_TILE), vc_ref[...].astype(F32))
        shift = dil.bit_length() - 1

        def problem(p, carry, g=g, window=window, dil=dil, shift=shift):
            sub = p >> shift
            base = sub * window + (p & (dil - 1))
            stride = dil if dil > 1 else None
            q_rows = pl.ds(base, Q_BLOCK, stride=stride)
            k_rows = pl.ds(base, 2 * Q_BLOCK, stride=stride)
            qstack = _stack_heads(get(qs_ref, q_rows).astype(BF16))
            k = get(ks_ref, k_rows).astype(BF16)
            v = get(vs_ref, k_rows).astype(BF16)
            bias = jnp.where((sub > 0) | (j > 0), bias_band, bias_first)
            s = _nt_dot(qstack, k) + jnp.concatenate([bias] * A_HEADS, axis=0)
            m = jnp.max(s, axis=1, keepdims=True)
            e = jnp.exp(s - m)
            l = jnp.sum(e, axis=1, keepdims=True)
            num = jnp.dot(e.astype(BF16), v, preferred_element_type=F32)
            m_n, l_n, num_n = (_unstack_heads(t, Q_BLOCK) for t in (m, l, num))
            if g == 0:
                put(m_sc, q_rows, m_n)
                put(l_sc, q_rows, l_n)
                put(n_sc, q_rows, num_n)
            else:
                m_o = get(m_sc, q_rows)
                m_x = jnp.maximum(m_o, m_n)
                a, b = jnp.exp(m_o - m_x), jnp.exp(m_n - m_x)
                put(m_sc, q_rows, m_x)
                put(l_sc, q_rows, a * get(l_sc, q_rows) + b * l_n)
                put(n_sc, q_rows, a * get(n_sc, q_rows) + b * num_n)
            return carry

        lax.fori_loop(0, A_TILE // Q_BLOCK, problem, 0, unroll=8)

    all_rows = slice(0, A_TILE)
    o_ref[...] = (get(n_sc, all_rows) / get(l_sc, all_rows)).astype(o_ref.dtype)


def _dilated_attention(pa3):
    b, s, _ = pa3.shape
    in_specs = []
    for g, (window, _) in enumerate(A_PAIRS):
        per = A_TILE // window
        cur = lambda bi, j, col: (bi, j, col)
        prev = lambda bi, j, col, per=per: (bi, jnp.maximum(j * per - 1, 0), col)
        in_specs.append(pl.BlockSpec((None, A_TILE, A_OUT), functools.partial(cur, col=3 * g)))
        for col in (3 * g + 1, 3 * g + 2):
            in_specs.append(pl.BlockSpec((None, window, A_OUT), functools.partial(prev, col=col)))
            in_specs.append(pl.BlockSpec((None, A_TILE, A_OUT), functools.partial(cur, col=col)))
    stage = lambda rows: pltpu.VMEM((A_OUT // LANES, rows, LANES), F32)
    return pl.pallas_call(
        _dilated_kernel,
        grid=(b, s // A_TILE),
        in_specs=in_specs,
        out_specs=pl.BlockSpec((None, A_TILE, A_OUT), lambda bi, j: (bi, j, 0)),
        out_shape=jax.ShapeDtypeStruct((b, s, A_OUT), BF16),
        scratch_shapes=[stage(A_TILE), stage(2 * A_TILE), stage(2 * A_TILE),
                        stage(A_TILE), stage(A_TILE), stage(A_TILE)],
        compiler_params=pltpu.CompilerParams(dimension_semantics=("arbitrary", "arbitrary"),
                                             vmem_limit_bytes=VMEM_LIMIT),
        name="dilated_attn",
    )(*([pa3] * (5 * A_GROUPS)))


SEL_TQ = 256
SEL_CHUNK = 512
SCORE_GROUP = 16
ATT_CHUNK = 512


def _sparse_kernel(qi_ref, wi_ref, ki_ref, q_ref, k_ref, v_ref, o_ref, key_ref, keyt_ref, s_ref, m_ref, l_ref,
                   acc_ref, *, topk):
    i = pl.program_id(1)
    qs = i * SEL_TQ
    nch = (qs + SEL_TQ - 1) // SEL_CHUNK + 1
    nat = (qs + SEL_TQ - 1) // ATT_CHUNK + 1
    lane = lax.broadcasted_iota(jnp.int32, (1, LANES), 1)
    qpos = qs + lax.broadcasted_iota(jnp.int32, (SEL_TQ, 1), 0)

    qi = qi_ref[...]
    per_head = []
    for h in range(IDX_HEADS):
        pair = qi[:, (h // 2) * LANES:(h // 2 + 1) * LANES]
        keep = (lane < IDX_DIM) if h % 2 == 0 else (lane >= IDX_DIM)
        per_head.append(jnp.where(keep, pair, jnp.zeros_like(pair)))
    groups = range(0, SEL_TQ, SCORE_GROUP)
    stacked = jnp.concatenate([ph[g:g + SCORE_GROUP] for g in groups for ph in per_head], axis=0)
    w = wi_ref[...]
    wcol = [w[:, h:h + 1] for h in range(IDX_HEADS)]

    def to_key(score):
        bits = lax.bitcast_convert_type(score, jnp.int32)
        return bits ^ ((bits >> 31) & 0x7FFFFFFF)

    def score_chunk(c, group_max):
        start = pl.multiple_of(c * SEL_CHUNK, SEL_CHUNK)
        logits = _nt_dot(stacked, ki_ref[pl.ds(start, SEL_CHUNK), :])
        pieces = []
        for g in groups:
            base = g * IDX_HEADS
            sg = jnp.zeros((SCORE_GROUP, SEL_CHUNK), F32)
            for h in range(IDX_HEADS):
                rows = slice(base + h * SCORE_GROUP, base + (h + 1) * SCORE_GROUP)
                sg = sg + jnp.maximum(logits[rows], 0.0) * wcol[h][g:g + SCORE_GROUP]
            pieces.append(sg)
        score = jnp.concatenate(pieces, axis=0)
        kpos = start + lax.broadcasted_iota(jnp.int32, (1, SEL_CHUNK), 1)
        causal = kpos <= qpos
        key_ref[:, pl.ds(start, SEL_CHUNK)] = jnp.where(causal, to_key(score), INT_MIN)
        score = jnp.where(causal, score, -jnp.inf)
        group_max = list(group_max)
        for jj in range(SEL_CHUNK // LANES):
            group_max[jj % 2] = jnp.maximum(group_max[jj % 2], score[:, jj * LANES:(jj + 1) * LANES])
        return tuple(group_max)

    def transpose_chunk(c):
        start = pl.multiple_of(c * SEL_CHUNK, SEL_CHUNK)
        keyt_ref[pl.ds(start, SEL_CHUNK), :] = key_ref[:, pl.ds(start, SEL_CHUNK)].T

    def score_step(c, group_max):
        transpose_chunk(c - 1)
        return score_chunk(c, group_max)

    neg_inf = jnp.full((SEL_TQ, LANES), -jnp.inf, F32)
    gm_even, gm_odd = lax.fori_loop(1, nch, score_step, score_chunk(0, (neg_inf, neg_inf)))
    transpose_chunk(nch - 1)

    def to_lanes(col):
        return jnp.broadcast_to(col, (SEL_TQ, LANES)).T[0:1, :]

    def to_rows(lanes):
        return jnp.broadcast_to(lanes, (LANES, SEL_TQ)).T[:, 0:1]

    lane_blocks = [slice(h, h + LANES) for h in range(0, SEL_TQ, LANES)]

    def count(compare, level, active=None):
        lanes_acc = 8 * SUBLANES
        parts = []
        for blk_no, lanes in enumerate(lane_blocks):
            def body(c, acc, lanes=lanes):
                blk = keyt_ref[pl.ds(pl.multiple_of(c * SEL_CHUNK, SEL_CHUNK), SEL_CHUNK), lanes]
                hit = jnp.where(compare(blk, level[:, lanes]), 1.0, 0.0)
                return acc + jnp.sum(hit.reshape(SEL_CHUNK // lanes_acc, lanes_acc, LANES), axis=0)
            trips = nch if active is None else jnp.where(active[blk_no], nch, 0)
            acc = lax.fori_loop(0, trips, body, jnp.zeros((lanes_acc, LANES), F32))
            parts.append(jnp.sum(acc, axis=0, keepdims=True))
        return jnp.concatenate(parts, axis=1)

    kf = float(topk)
    assert topk <= 2 * LANES and SEL_TQ % LANES == 0
    searchable = qs >= topk
    lo_f = to_lanes(jnp.minimum(jnp.min(gm_even, axis=1, keepdims=True), jnp.min(gm_odd, axis=1, keepdims=True)))
    hi_f = to_lanes(jnp.maximum(jnp.max(gm_even, axis=1, keepdims=True), jnp.max(gm_odd, axis=1, keepdims=True)))
    lo0 = jnp.where(searchable, to_key(lo_f), INT_MIN + 1)
    hi0 = to_key(hi_f) + 1
    cnt_lo0 = jnp.where(searchable, kf + 1.0, kf) + jnp.zeros((1, SEL_TQ), F32)

    def candidate(lo, hi, cnt_lo):
        mid = (lo >> 1) + (hi >> 1) + (lo & hi & 1)
        open_q = (cnt_lo > kf) & (mid > lo)
        cand = jnp.where((lo < 0) & (hi > 0), 0, jnp.where((lo == 0) & (hi > 1), 1, mid))
        open_q = jnp.where(open_q, 1.0, 0.0)
        block_open = tuple(jnp.max(open_q[:, lanes]) > 0.0 for lanes in lane_blocks)
        return (cand, open_q, functools.reduce(jnp.logical_or, block_open)) + block_open

    def search_step(carry, active):
        lo, hi, cnt_lo, cnt_hi, cand, open_q = carry[:6]
        cnt = count(lambda blk, level: blk >= level, cand, active)
        up = (open_q > 0.0) & (cnt >= kf)
        down = (open_q > 0.0) & (cnt < kf)
        lo, cnt_lo = jnp.where(up, cand, lo), jnp.where(up, cnt, cnt_lo)
        hi, cnt_hi = jnp.where(down, cand, hi), jnp.where(down, cnt, cnt_hi)
        return (lo, hi, cnt_lo, cnt_hi) + candidate(lo, hi, cnt_lo)

    def search_trip(carry):
        active = carry[7:]
        return search_step(search_step(carry, active), active)

    thr_q, _, cnt_thr, cnt_hi = lax.while_loop(
        lambda carry: carry[6], search_trip,
        (lo0, hi0, cnt_lo0, jnp.zeros((1, SEL_TQ), F32)) + candidate(lo0, hi0, cnt_lo0))[:4]
    has_ties = jnp.max(cnt_thr) > kf
    thr = to_rows(thr_q)

    @pl.when(has_ties)
    def _():
        need = to_rows(jnp.where(cnt_thr > kf, kf - cnt_hi, kf))
        r = lax.broadcasted_iota(jnp.int32, (ATT_CHUNK, ATT_CHUNK), 0)
        cc = lax.broadcasted_iota(jnp.int32, (ATT_CHUNK, ATT_CHUNK), 1)
        before = jnp.where(r < cc, 1.0, 0.0).astype(BF16)

        def body(c, seen):
            cols = pl.ds(pl.multiple_of(c * ATT_CHUNK, ATT_CHUNK), ATT_CHUNK)
            blk = key_ref[:, cols]
            eq = blk == thr
            eqf = jnp.where(eq, 1.0, 0.0)
            rank = seen + jnp.dot(eqf.astype(BF16), before, preferred_element_type=F32)
            key_ref[:, cols] = jnp.where(eq & (rank >= need), INT_MIN, blk)
            return seen + jnp.sum(eqf, axis=1, keepdims=True)
        lax.fori_loop(0, nat, body, jnp.zeros((SEL_TQ, 1), F32))

    qstack = _stack_heads(q_ref[...])

    def logits(c, sel):
        start = pl.multiple_of(c * ATT_CHUNK, ATT_CHUNK)
        bias = jnp.where(sel, 0.0, NEG)
        return _nt_dot(qstack, k_ref[pl.ds(start, ATT_CHUNK), :]) + jnp.concatenate([bias] * B_HEADS, axis=0)

    def values(c):
        return v_ref[pl.ds(pl.multiple_of(c * ATT_CHUNK, ATT_CHUNK), ATT_CHUNK), :]

    def selected(c):
        return key_ref[:, pl.ds(pl.multiple_of(c * ATT_CHUNK, ATT_CHUNK), ATT_CHUNK)] >= thr

    _flash_init(m_ref, l_ref, acc_ref)
    _flash_pipeline(nat, lambda c: logits(c, selected(c)), values, s_ref, m_ref, l_ref, acc_ref)

    o_ref[...] = _unstack_heads(_flash_result(l_ref, acc_ref), SEL_TQ).astype(o_ref.dtype)


def _sparse_attention(qi3, wi3, ki3, pb3):
    b, s, _ = pb3.shape
    topk = min(TOPK_MAX, s // 4)
    rows = B_HEADS * SEL_TQ
    qblk = lambda width, col: pl.BlockSpec((None, SEL_TQ, width), lambda bi, i: (bi, i, col))
    seq = lambda width, col: pl.BlockSpec((None, s, width), lambda bi, i: (bi, 0, col))
    return pl.pallas_call(
        functools.partial(_sparse_kernel, topk=topk),
        grid=(b, s // SEL_TQ),
        in_specs=[qblk(IDX_Q, 0), qblk(LANES, 0), seq(LANES, 0), qblk(B_OUT, 0), seq(B_OUT, 1), seq(B_OUT, 2)],
        out_specs=pl.BlockSpec((None, SEL_TQ, B_OUT), lambda bi, i: (bi, i, 0)),
        out_shape=jax.ShapeDtypeStruct((b, s, B_OUT), BF16),
        scratch_shapes=[pltpu.VMEM((SEL_TQ, s), jnp.int32), pltpu.VMEM((s, SEL_TQ), jnp.int32),
                        pltpu.VMEM((2, rows, ATT_CHUNK), F32),
                        pltpu.VMEM((rows, LANES), F32), pltpu.VMEM((rows, LANES), F32),
                        pltpu.VMEM((rows, B_OUT), F32)],
        compiler_params=pltpu.CompilerParams(dimension_semantics=("arbitrary", "arbitrary"),
                                             vmem_limit_bytes=VMEM_LIMIT),
        name="sparse_attn",
    )(qi3, wi3, ki3, pb3, pb3, pb3)


DIFF_TQ = 512
DIFF_TK = 1024


def _diff_kernel(lam_ref, g_ref, q_ref, k_ref, v_ref, o_ref, s_ref, m_ref, l_ref, acc_ref, *, out_scale):
    i = pl.program_id(2)
    qstack = _stack_heads(q_ref[...])
    _flash_init(m_ref, l_ref, acc_ref)

    def logits(kb):
        return _nt_dot(qstack, k_ref[pl.ds(pl.multiple_of(kb * DIFF_TK, DIFF_TK), DIFF_TK), :])

    def values(kb):
        return v_ref[pl.ds(pl.multiple_of(kb * DIFF_TK, DIFF_TK), DIFF_TK), :]

    n_full = (i * DIFF_TQ) // DIFF_TK

    @pl.when(n_full > 0)
    def _():
        _flash_pipeline(n_full, logits, values, s_ref, m_ref, l_ref, acc_ref)

    def diagonal(width):
        start = pl.multiple_of(n_full * DIFF_TK, DIFF_TK)
        qpos = i * DIFF_TQ + (lax.broadcasted_iota(jnp.int32, (2 * DIFF_TQ, width), 0) & (DIFF_TQ - 1))
        kpos = start + lax.broadcasted_iota(jnp.int32, (2 * DIFF_TQ, width), 1)
        s = _nt_dot(qstack, k_ref[pl.ds(start, width), :]) + jnp.where(kpos <= qpos, 0.0, NEG)
        _flash_step(s, v_ref[pl.ds(start, width), :], m_ref, l_ref, acc_ref)

    assert DIFF_TK == 2 * DIFF_TQ
    first_half = (i * DIFF_TQ) % DIFF_TK == 0
    pl.when(first_half)(lambda: diagonal(DIFF_TQ))
    pl.when(jnp.logical_not(first_half))(lambda: diagonal(DIFF_TK))

    res = _flash_result(l_ref, acc_ref)
    o = res[:DIFF_TQ] - lam_ref[0] * res[DIFF_TQ:]
    o_ref[...] = (_rms(o, g_ref[...]) * out_scale).astype(o_ref.dtype)


def _diff_attention(pc3, lam, subln_g, lam_init):
    b, s, _ = pc3.shape
    nq = C_QK // LANES
    return pl.pallas_call(
        functools.partial(_diff_kernel, out_scale=1.0 - lam_init),
        grid=(b, C_HEADS, s // DIFF_TQ),
        in_specs=[pl.BlockSpec(memory_space=pltpu.SMEM),
                  pl.BlockSpec((1, C_VDIM), lambda bi, h, i: (0, 0)),
                  pl.BlockSpec((None, DIFF_TQ, LANES), lambda bi, h, i: (bi, i, h)),
                  pl.BlockSpec((None, s, LANES), lambda bi, h, i: (bi, 0, nq + h)),
                  pl.BlockSpec((None, s, C_VDIM), lambda bi, h, i: (bi, 0, 2 * nq + h))],
        out_specs=pl.BlockSpec((None, DIFF_TQ, C_VDIM), lambda bi, h, i: (bi, i, h)),
        out_shape=jax.ShapeDtypeStruct((b, s, C_OUT), BF16),
        scratch_shapes=[pltpu.VMEM((2, 2 * DIFF_TQ, DIFF_TK), F32),
                        pltpu.VMEM((2 * DIFF_TQ, LANES), F32), pltpu.VMEM((2 * DIFF_TQ, LANES), F32),
                        pltpu.VMEM((2 * DIFF_TQ, C_VDIM), F32)],
        compiler_params=pltpu.CompilerParams(dimension_semantics=("arbitrary", "arbitrary", "arbitrary"),
                                             vmem_limit_bytes=VMEM_LIMIT),
        name="diff_attn",
    )(lam, subln_g, pc3, pc3, pc3)


def _merge_kernel(x_ref, gpre_ref, gpost_ref, oa_ref, ob_ref, oc_ref, wg_ref, wa_ref, wb_ref, wc_ref, wo_ref,
                  out_ref, y_ref, h_ref):
    x = x_ref[...]
    xn = _rms(x, gpre_ref[...]).astype(BF16)
    branches = ((oa_ref, wa_ref), (ob_ref, wb_ref), (oc_ref, wc_ref))
    cw = 2 * LANES
    for c in range(0, D_MODEL, cw):
        y = jnp.zeros((x.shape[0], cw), F32)
        for j, (o_ref, w_ref) in enumerate(branches):
            logit = jnp.dot(xn, wg_ref[:, j * D_MODEL + c:j * D_MODEL + c + cw], preferred_element_type=F32)
            gate = 1.0 / (1.0 + jnp.exp(-logit))
            y = y + gate * jnp.dot(o_ref[...], w_ref[:, c:c + cw], preferred_element_type=F32)
        y_ref[:, c:c + cw] = y.astype(BF16)
    ssq = jnp.zeros((x.shape[0], 1), F32)
    for c in range(0, D_MODEL, cw):
        h = jnp.dot(y_ref[...], wo_ref[:, c:c + cw], preferred_element_type=F32)
        h_ref[:, c:c + cw] = h
        ssq = ssq + jnp.sum(h * h, axis=1, keepdims=True)
    inv = lax.rsqrt(ssq * (1.0 / D_MODEL) + NORM_EPS)
    out_ref[...] = x + h_ref[...] * inv * gpost_ref[...]


def _merge(x2, gpre, gpost, oa, ob, oc, w, tm=512):
    n = x2.shape[0]
    row = lambda width: pl.BlockSpec((tm, width), lambda i: (i, 0))
    full = lambda a: pl.BlockSpec(a.shape, lambda i: (0, 0))
    return pl.pallas_call(
        _merge_kernel,
        grid=(n // tm,),
        in_specs=[row(D_MODEL), full(gpre), full(gpost), row(A_OUT), row(B_OUT), row(C_OUT)] + [full(a) for a in w],
        out_specs=row(D_MODEL),
        out_shape=jax.ShapeDtypeStruct((n, D_MODEL), F32),
        scratch_shapes=[pltpu.VMEM((tm, D_MODEL), BF16), pltpu.VMEM((tm, D_MODEL), F32)],
        compiler_params=pltpu.CompilerParams(dimension_semantics=("arbitrary",), vmem_limit_bytes=VMEM_LIMIT),
        name="merge",
    )(x2, gpre, gpost, oa, ob, oc, *w)


FFN_CHUNK = 256
CARRY_ROWS = 8


def _ffn_kernel(x_ref, gpre_ref, gpost_ref, wup_ref, cw_ref, cb_ref, wdn_ref, out_ref, carry_ref, act_ref,
                *, tiles_per_seq):
    x = x_ref[...]
    tm = x.shape[0]
    xn = _rms(x, gpre_ref[...]).astype(BF16)
    first = pl.program_id(0) % tiles_per_seq == 0
    row = lax.broadcasted_iota(jnp.int32, (tm, 1), 0)
    assert CONV_WIDTH == 3 and CONV_WIDTH - 1 <= CARRY_ROWS

    def conv(col):
        h = jnp.dot(xn, wup_ref[:, col:col + FFN_CHUNK], preferred_element_type=F32)
        prev = jnp.where(first, 0.0, carry_ref[:, col:col + FFN_CHUNK])
        carry_ref[:, col:col + FFN_CHUNK] = h[tm - CARRY_ROWS:, :]
        p1 = prev[CARRY_ROWS - 1:CARRY_ROWS, :]
        p2 = prev[CARRY_ROWS - 2:CARRY_ROWS - 1, :]
        h1 = jnp.where(row == 0, p1, pltpu.roll(h, 1, 0))
        h2 = jnp.where(row == 0, p2, jnp.where(row == 1, p1, pltpu.roll(h, 2, 0)))
        w = cw_ref[:, col:col + FFN_CHUNK]
        return h2 * w[0:1, :] + h1 * w[1:2, :] + h * w[2:3, :] + cb_ref[:, col:col + FFN_CHUNK]

    for c in range(0, D_FF, FFN_CHUNK):
        g = conv(c)
        u = conv(D_FF + c)
        gelu = 0.5 * g * (1.0 + jnp.tanh(math.sqrt(2.0 / math.pi) * (g + 0.044715 * (g * g * g))))
        act_ref[:, c:c + FFN_CHUNK] = (gelu * u).astype(BF16)
    h = jnp.dot(act_ref[...], wdn_ref[...], preferred_element_type=F32)
    out_ref[...] = x + _rms(h, gpost_ref[...])


def _ffn(x2, gpre, gpost, wup, conv_w, conv_b, wdn, seq_len, tm=512):
    n = x2.shape[0]
    row = lambda width: pl.BlockSpec((tm, width), lambda i: (i, 0))
    full = lambda a: pl.BlockSpec(a.shape, lambda i: (0, 0))
    return pl.pallas_call(
        functools.partial(_ffn_kernel, tiles_per_seq=seq_len // tm),
        grid=(n // tm,),
        in_specs=[row(D_MODEL), full(gpre), full(gpost), full(wup), full(conv_w), full(conv_b), full(wdn)],
        out_specs=row(D_MODEL),
        out_shape=jax.ShapeDtypeStruct((n, D_MODEL), F32),
        scratch_shapes=[pltpu.VMEM((CARRY_ROWS, 2 * D_FF), F32), pltpu.VMEM((tm, D_FF), BF16)],
        compiler_params=pltpu.CompilerParams(dimension_semantics=("arbitrary",), vmem_limit_bytes=VMEM_LIMIT),
        name="conv_ffn",
    )(x2, gpre, gpost, wup, conv_w, conv_b, wdn)


def _rope_lane_tables(positions):
    half = ROPE_DIM // 2
    inv = ROPE_THETA ** (-jnp.arange(0, ROPE_DIM, 2, dtype=F32) / ROPE_DIM)
    ang = positions.astype(F32).reshape(-1, 1) * inv
    cos, sin = jnp.cos(ang), jnp.sin(ang)
    n = ang.shape[0]
    rest = HEAD_DIM - ROPE_DIM
    ones, zeros, zh = jnp.ones((n, rest), F32), jnp.zeros((n, rest), F32), jnp.zeros((n, half), F32)
    ct = jnp.concatenate([cos, cos, ones], axis=1)
    s1 = jnp.concatenate([-sin, zh, zeros], axis=1)
    s2 = jnp.concatenate([zh, sin, zeros], axis=1)
    return tuple(jnp.tile(t, (1, LANES // HEAD_DIM)) for t in (ct, s1, s2))


def _split_w_in(w_in):
    o_b, o_i, o_c, o_g = A_IN, A_IN + B_IN, A_IN + B_IN + IDX_IN, A_IN + B_IN + IDX_IN + C_IN
    wk = w_in[:, o_i + IDX_Q:o_i + IDX_Q + IDX_DIM]
    ww = w_in[:, o_i + IDX_Q + IDX_DIM:o_c]
    proj = (w_in[:, :o_b], w_in[:, o_b:o_i], w_in[:, o_i:o_i + IDX_Q],
            jnp.concatenate([wk, wk], axis=1),
            jnp.pad(ww, ((0, 0), (0, LANES - IDX_HEADS))),
            w_in[:, o_c:o_g])
    return tuple(a.astype(BF16) for a in proj), w_in[:, o_g:].astype(BF16)


def kernel(x, positions, w_in, w_br_a, w_br_b, w_br_c, w_out, lam_q1, lam_k1, lam_q2, lam_k2, subln_g,
           norm_mix_pre, norm_mix_post, norm_ffn_pre, norm_ffn_post, w_ffn_up, conv_w, conv_b, w_ffn_down):
    b, s, d = x.shape
    depth = w_in.shape[0]
    assert d == D_MODEL and s % A_TILE == 0 and s % SEL_CHUNK == 0 and s % DIFF_TK == 0
    tables = _rope_lane_tables(positions)
    x2 = x.reshape(b * s, d)
    vec = lambda a: a.reshape(1, -1)
    for layer in range(depth):
        lam_init = 0.8 - 0.6 * math.exp(-0.3 * layer)
        lam = (jnp.exp(jnp.sum(lam_q1[layer] * lam_k1[layer])) - jnp.exp(jnp.sum(lam_q2[layer] * lam_k2[layer]))
               + lam_init).reshape(1).astype(F32)
        w_proj, w_gate = _split_w_in(w_in[layer])
        pa, pb, qi, ki, wi, pc = _inproj(x2, vec(norm_mix_pre[layer]), tables, w_proj)
        seq = lambda a: a.reshape(b, s, a.shape[-1])
        oa = _dilated_attention(seq(pa))
        ob = _sparse_attention(seq(qi), seq(wi), seq(ki), seq(pb))
        oc = _diff_attention(seq(pc), lam, vec(subln_g[layer]), lam_init)
        flat = lambda a: a.reshape(b * s, a.shape[-1])
        w_merge = (w_gate, w_br_a[layer].astype(BF16), w_br_b[layer].astype(BF16), w_br_c[layer].astype(BF16),
                   w_out[layer].astype(BF16))
        x2 = _merge(x2, vec(norm_mix_pre[layer]), vec(norm_mix_post[layer]), flat(oa), flat(ob), flat(oc), w_merge)
        x2 = _ffn(x2, vec(norm_ffn_pre[layer]), vec(norm_ffn_post[layer]), w_ffn_up[layer].astype(BF16),
                  conv_w[layer], vec(conv_b[layer]), w_ffn_down[layer].astype(BF16), s)
    return x2.reshape(b, s, d)
```
